```python
import jax, jax.numpy as jnp
from jax import lax
import numpy as np

D_MODEL = 2048
BATCH = 4
SEQ = 4096
DEPTH = 4

N_MIXERS = 3
EPS = 1e-6
SB_HEADS = 16
SB_HEAD_DIM = D_MODEL // SB_HEADS
SB_BLOCK = 128
GM_WIDTH = D_MODEL
GM_GROUPS = 16
GM_GROUP_DIM = GM_WIDTH // GM_GROUPS
GM_CHUNK = 128
RET_HEADS = 8
RET_QK_DIM = D_MODEL // RET_HEADS
RET_V_DIM = 2 * D_MODEL // RET_HEADS
RET_CHUNK = 128
ROPE_BASE = 10000.0
D_FF = -(-8 * D_MODEL // (3 * 256)) * 256

kernel_name = "hybrid_sb_gmlp_retention_trunk"


def rmsnorm(x, g):
    xf = x.astype(jnp.float32)
    y = xf * lax.rsqrt(jnp.mean(xf * xf, axis=-1, keepdims=True) + EPS)
    return (y * g.astype(jnp.float32)).astype(x.dtype)


def layernorm(x, g):
    xf = x.astype(jnp.float32)
    mu = jnp.mean(xf, axis=-1, keepdims=True)
    xc = xf - mu
    y = xc * lax.rsqrt(jnp.mean(xc * xc, axis=-1, keepdims=True) + EPS)
    return (y * g.astype(jnp.float32)).astype(x.dtype)


def stick_breaking_mixer(h, w_qkv, w_o):
    B, S, D = h.shape
    q, k, v = jnp.split(h @ w_qkv, 3, axis=-1)
    q = q.reshape(B, S, SB_HEADS, SB_HEAD_DIM) * (SB_HEAD_DIM ** -0.5)
    k = k.reshape(B, S, SB_HEADS, SB_HEAD_DIM)
    v = v.reshape(B, S, SB_HEADS, SB_HEAD_DIM)
    nb = S // SB_BLOCK
    qb = q.reshape(B, nb, SB_BLOCK, SB_HEADS, SB_HEAD_DIM).transpose(1, 0, 3, 2, 4)
    key_pos = jnp.arange(S)

    def block(args):
        qi, i = args
        z = jnp.einsum('bhtd,bshd->bhts', qi, k, preferred_element_type=jnp.float32)
        q_pos = i * SB_BLOCK + jnp.arange(SB_BLOCK)
        causal = key_pos[None, :] < q_pos[:, None]
        log_keep = jnp.where(causal, jax.nn.log_sigmoid(-z), 0.0)
        tail = lax.cumsum(log_keep, axis=3, reverse=True) - log_keep
        a = jnp.where(causal, jnp.exp(jax.nn.log_sigmoid(z) + tail), 0.0)
        return jnp.einsum('bhts,bshd->bthd', a.astype(v.dtype), v)

    o = lax.map(block, (qb, jnp.arange(nb)))
    o = o.transpose(1, 0, 2, 3, 4).reshape(B, S, D)
    return o @ w_o


def chunked_gmlp_mixer(h, w_in, v_norm, w_s, b_s, w_o):
    B, S, D = h.shape
    u, v = jnp.split(jax.nn.gelu(h @ w_in), 2, axis=-1)
    v = layernorm(v, v_norm)
    nc = S // GM_CHUNK
    v = v.reshape(B, nc, GM_CHUNK, GM_GROUPS, GM_GROUP_DIM)
    mask = jnp.tril(jnp.ones((GM_CHUNK, GM_CHUNK), dtype=bool))
    ws = jnp.where(mask[None], w_s, 0.0).astype(v.dtype)
    mixed = jnp.einsum('gts,bcsgd->bctgd', ws, v) + b_s.T[None, None, :, :, None]
    return (u * mixed.reshape(B, S, GM_WIDTH)) @ w_o


def rotary(x, pos):
    half = x.shape[-1] // 2
    inv = ROPE_BASE ** (-jnp.linspace(0.0, 1.0, half, dtype=jnp.float32))
    ang = pos.astype(jnp.float32)[:, None] * inv[None, :]
    cos = jnp.cos(ang)[None, :, None, :]
    sin = jnp.sin(ang)[None, :, None, :]
    x1, x2 = x[..., :half], x[..., half:]
    return jnp.concatenate([x1 * cos - x2 * sin, x1 * sin + x2 * cos], axis=-1).astype(x.dtype)


def retention_mixer(h, w_qkvg, gn_gain, w_o):
    B, S, D = h.shape
    q, k, v, g = jnp.split(h @ w_qkvg, [D, 2 * D, 4 * D], axis=-1)
    pos = jnp.arange(S)
    q = rotary(q.reshape(B, S, RET_HEADS, RET_QK_DIM), pos)
    k = rotary(k.reshape(B, S, RET_HEADS, RET_QK_DIM), pos) * (RET_QK_DIM ** -0.5)
    v = v.reshape(B, S, RET_HEADS, RET_V_DIM)
    log_gamma = jnp.log(1.0 - 2.0 ** (-5.0 - jnp.arange(RET_HEADS, dtype=jnp.float32)))
    C = RET_CHUNK
    idx = jnp.arange(C, dtype=jnp.float32)
    diff = idx[:, None] - idx[None, :]
    intra_decay = jnp.where(diff[None] >= 0,
                            jnp.exp(jnp.maximum(diff, 0.0)[None] * log_gamma[:, None, None]), 0.0)
    query_decay = jnp.exp((idx + 1.0)[None, :] * log_gamma[:, None])
    key_decay = jnp.exp((C - 1.0 - idx)[None, :] * log_gamma[:, None])
    chunk_decay = jnp.exp(C * log_gamma)
    nc = S // C

    def to_chunks(t):
        return t.reshape(B, nc, C, RET_HEADS, t.shape[-1]).transpose(1, 0, 3, 2, 4).astype(jnp.float32)

    def step(state, inp):
        qc, kc, vc = inp
        scores = jnp.einsum('bhtk,bhsk->bhts', qc, kc) * intra_decay[None]
        inner = jnp.einsum('bhts,bhsv->bhtv', scores, vc)
        cross = jnp.einsum('bhtk,bhkv->bhtv', qc, state) * query_decay[None, :, :, None]
        new_state = state * chunk_decay[None, :, None, None] + jnp.einsum(
            'bhsk,bhsv->bhkv', kc * key_decay[None, :, :, None], vc)
        return new_state, inner + cross

    state0 = jnp.zeros((B, RET_HEADS, RET_QK_DIM, RET_V_DIM), jnp.float32)
    _, o = lax.scan(step, state0, (to_chunks(q), to_chunks(k), to_chunks(v)))
    o = o.transpose(1, 0, 3, 2, 4).reshape(B, S, RET_HEADS, RET_V_DIM)
    o = rmsnorm(o, gn_gain.reshape(RET_HEADS, RET_V_DIM)).reshape(B, S, 2 * D).astype(h.dtype)
    return (jax.nn.silu(g) * o) @ w_o


def swiglu_ffn(h, w_gate_up, w_down):
    gate, up = jnp.split(h @ w_gate_up, 2, axis=-1)
    return (jax.nn.silu(gate) * up) @ w_down


def setup_inputs(seed: int = 0) -> dict:
    key = jax.random.key(seed)
    ks = jax.random.split(key, 20)
    D, f32 = D_MODEL, jnp.float32
    n_sb = len(range(0, DEPTH, N_MIXERS))
    n_gm = len(range(1, DEPTH, N_MIXERS))
    n_ret = len(range(2, DEPTH, N_MIXERS))

    def w(k, shape, fan_in):
        return jax.random.normal(k, shape, f32) * (fan_in ** -0.5)

    def gain(k, shape):
        return 1.0 + 0.02 * jax.random.normal(k, shape, f32)

    return {
        "x": jax.random.normal(ks[0], (BATCH, SEQ, D), f32),
        "sb_norm": gain(ks[1], (n_sb, D)),
        "sb_w_qkv": w(ks[2], (n_sb, D, 3 * D), D),
        "sb_w_o": w(ks[3], (n_sb, D, D), D),
        "gm_norm": gain(ks[4], (n_gm, D)),
        "gm_w_in": w(ks[5], (n_gm, D, 2 * GM_WIDTH), D),
        "gm_v_norm": gain(ks[6], (n_gm, GM_WIDTH)),
        "gm_w_s": w(ks[7], (n_gm, GM_GROUPS, GM_CHUNK, GM_CHUNK), GM_CHUNK),
        "gm_b_s": 1.0 + 0.1 * jax.random.normal(ks[8], (n_gm, GM_GROUPS, GM_CHUNK), f32),
        "gm_w_o": w(ks[9], (n_gm, GM_WIDTH, D), GM_WIDTH),
        "ret_norm": gain(ks[10], (n_ret, D)),
        "ret_w_qkvg": w(ks[11], (n_ret, D, 6 * D), D),
        "ret_gn": gain(ks[12], (n_ret, 2 * D)),
        "ret_w_o": w(ks[13], (n_ret, 2 * D, D), 2 * D),
        "ffn_norm": gain(ks[14], (DEPTH, D)),
        "ffn_w_gate_up": w(ks[15], (DEPTH, D, 2 * D_FF), D),
        "ffn_w_down": w(ks[16], (DEPTH, D_FF, D), D_FF),
        "final_norm": gain(ks[17], (D,)),
    }


def reference(x, sb_norm, sb_w_qkv, sb_w_o, gm_norm, gm_w_in, gm_v_norm, gm_w_s, gm_b_s, gm_w_o,
              ret_norm, ret_w_qkvg, ret_gn, ret_w_o, ffn_norm, ffn_w_gate_up, ffn_w_down, final_norm):
    h = x
    for i in range(DEPTH):
        kind, j = i % N_MIXERS, i // N_MIXERS
        if kind == 0:
            h = h + stick_breaking_mixer(rmsnorm(h, sb_norm[j]), sb_w_qkv[j], sb_w_o[j])
        elif kind == 1:
            h = h + chunked_gmlp_mixer(rmsnorm(h, gm_norm[j]), gm_w_in[j], gm_v_norm[j],
                                       gm_w_s[j], gm_b_s[j], gm_w_o[j])
        else:
            h = h + retention_mixer(rmsnorm(h, ret_norm[j]), ret_w_qkvg[j], ret_gn[j], ret_w_o[j])
        h = h + swiglu_ffn(rmsnorm(h, ffn_norm[i]), ffn_w_gate_up[i], ffn_w_down[i])
    return rmsnorm(h, final_norm)
```

```python
import functools

import jax
import jax.numpy as jnp
from jax import lax
from jax.experimental import pallas as pl
from jax.experimental.pallas import tpu as pltpu

F32 = jnp.float32
BF16 = jnp.bfloat16

EPS = 1e-6
N_MIXERS = 3
SB_HEADS = 16
SB_HEAD_DIM = 128
SB_TILE = 256
GM_GROUPS = 16
GM_GROUP_DIM = 128
GM_CHUNK = 128
RET_HEADS = 8
RET_QK_DIM = 256
RET_V_DIM = 512
RET_CHUNK = 128
ROPE_BASE = 10000.0

V7X_VMEM_LIMIT_BYTES = 60000 * 1024
NORM_ROWS = 256


def _nbytes(shape, dtype):
    n = 1
    for s in shape:
        n *= s
    return n * jnp.dtype(dtype).itemsize


def _params(n_axes, vmem_bytes):
    return pltpu.CompilerParams(
        dimension_semantics=("arbitrary",) * n_axes,
        vmem_limit_bytes=int(min(vmem_bytes, V7X_VMEM_LIMIT_BYTES)),
    )


def _rms_rows(xf, gain):
    y = xf * lax.rsqrt(jnp.mean(xf * xf, axis=-1, keepdims=True) + EPS)
    return y * gain


def _norm_into(x_ref, g_ref, xn_ref):
    rows = x_ref.shape[0]
    step = min(NORM_ROWS, rows)

    def body(r, carry):
        sl = pl.ds(pl.multiple_of(r * step, step), step)
        xn_ref[sl, :] = _rms_rows(x_ref[sl, :], g_ref[...]).astype(xn_ref.dtype)
        return carry

    lax.fori_loop(0, rows // step, body, 0)


def _mm_body(*refs, norm, act, has_res):
    it = iter(refs)
    x_ref = next(it)
    g_ref = next(it) if norm else None
    w_ref = next(it)
    r_ref = next(it) if has_res else None
    o_ref = next(it)
    xn_ref = next(it) if norm else None

    if norm:
        @pl.when(pl.program_id(1) == 0)
        def _():
            _norm_into(x_ref, g_ref, xn_ref)
        lhs = xn_ref[...]
    else:
        lhs = x_ref[...]
    acc = jnp.dot(lhs, w_ref[...], preferred_element_type=F32)
    if act == "gelu":
        acc = jax.nn.gelu(acc)
    if has_res:
        acc = acc + r_ref[...]
    o_ref[...] = acc.astype(o_ref.dtype)


def _matmul(x, w_stack, layer, *, gain=None, residual=None, act=None,
            out_dtype, tm=1024, tn=1024):
    m, k = x.shape
    n = w_stack.shape[-1]
    norm = gain is not None
    has_res = residual is not None
    in_specs = [pl.BlockSpec((tm, k), lambda i, j: (i, 0))]
    args = [x]
    if norm:
        in_specs.append(pl.BlockSpec((1, k), lambda i, j: (0, 0)))
        args.append(gain.reshape(1, k))
    in_specs.append(pl.BlockSpec((None, k, tn), lambda i, j: (layer, 0, j)))
    args.append(w_stack)
    if has_res:
        in_specs.append(pl.BlockSpec((tm, tn), lambda i, j: (i, j)))
        args.append(residual)
    scratch = [pltpu.VMEM((tm, k), BF16)] if norm else []
    vmem = (2 * _nbytes((tm, k), x.dtype) + 2 * _nbytes((k, tn), BF16)
            + 2 * _nbytes((tm, tn), out_dtype) + 2 * _nbytes((tm, tn), F32)
            + (2 * _nbytes((tm, tn), F32) if has_res else 0)
            + (_nbytes((tm, k), BF16) + 4 * _nbytes((NORM_ROWS, k), F32) if norm else 0))
    return pl.pallas_call(
        functools.partial(_mm_body, norm=norm, act=act, has_res=has_res),
        grid=(m // tm, n // tn),
        in_specs=in_specs,
        out_specs=pl.BlockSpec((tm, tn), lambda i, j: (i, j)),
        out_shape=jax.ShapeDtypeStruct((m, n), out_dtype),
        scratch_shapes=scratch,
        compiler_params=_params(2, vmem),
    )(*args)


def _ffn_body(*refs, final):
    if final:
        h_ref, g_ref, wg_ref, wu_ref, wd_ref, fg_ref, o_ref, xn_ref = refs
    else:
        h_ref, g_ref, wg_ref, wu_ref, wd_ref, o_ref, xn_ref = refs
        fg_ref = None
    f = pl.program_id(1)

    @pl.when(f == 0)
    def _():
        _norm_into(h_ref, g_ref, xn_ref)
        o_ref[...] = h_ref[...]

    xn = xn_ref[...]
    gate = jnp.dot(xn, wg_ref[...], preferred_element_type=F32)
    up = jnp.dot(xn, wu_ref[...], preferred_element_type=F32)
    act = (jax.nn.silu(gate) * up).astype(BF16)
    o_ref[...] += jnp.dot(act, wd_ref[...], preferred_element_type=F32)

    if final:
        @pl.when(f == pl.num_programs(1) - 1)
        def _():
            rows = o_ref.shape[0]
            step = min(NORM_ROWS, rows)

            def body(r, carry):
                sl = pl.ds(pl.multiple_of(r * step, step), step)
                o_ref[sl, :] = _rms_rows(o_ref[sl, :], fg_ref[...])
                return carry

            lax.fori_loop(0, rows // step, body, 0)


def _ffn(h, gain, w_gu_stack, w_d_stack, layer, *, final_gain=None, tm=512, tf=512):
    m, d = h.shape
    ff = w_d_stack.shape[1]
    nf = ff // tf
    final = final_gain is not None
    in_specs = [
        pl.BlockSpec((tm, d), lambda i, f: (i, 0)),
        pl.BlockSpec((1, d), lambda i, f: (0, 0)),
        pl.BlockSpec((None, d, tf), lambda i, f: (layer, 0, f)),
        pl.BlockSpec((None, d, tf), lambda i, f: (layer, 0, nf + f)),
        pl.BlockSpec((None, tf, d), lambda i, f: (layer, f, 0)),
    ]
    args = [h, gain.reshape(1, d), w_gu_stack, w_gu_stack, w_d_stack]
    if final:
        in_specs.append(pl.BlockSpec((1, d), lambda i, f: (0, 0)))
        args.append(final_gain.reshape(1, d))
    vmem = (4 * _nbytes((tm, d), F32) + _nbytes((tm, d), BF16)
            + 6 * _nbytes((d, tf), BF16) + 3 * _nbytes((tm, tf), F32)
            + 2 * _nbytes((tm, d), F32) + 4 * _nbytes((NORM_ROWS, d), F32))
    return pl.pallas_call(
        functools.partial(_ffn_body, final=final),
        grid=(m // tm, nf),
        in_specs=in_specs,
        out_specs=pl.BlockSpec((tm, d), lambda i, f: (i, 0)),
        out_shape=jax.ShapeDtypeStruct((m, d), F32),
        scratch_shapes=[pltpu.VMEM((tm, d), BF16)],
        compiler_params=_params(2, vmem),
    )(*args)


def _sb_body(q_ref, k_ref, v_ref, o_ref):
    t = SB_TILE
    i = pl.program_id(2)
    q = q_ref[...]
    scale = SB_HEAD_DIM ** -0.5
    row = lax.broadcasted_iota(jnp.int32, (t, t), 0)
    col = lax.broadcasted_iota(jnp.int32, (t, t), 1)
    suffix = (row >= col).astype(BF16)
    suffix2 = jnp.concatenate([suffix, suffix], axis=0)
    causal = col < row

    def tile(kb, run, acc, masked):
        sl = pl.ds(pl.multiple_of(kb * t, t), t)
        ks = k_ref[sl, :]
        vs = v_ref[sl, :]
        z = lax.dot_general(q, ks, (((1,), (1,)), ((), ())),
                            preferred_element_type=F32) * scale
        log_keep = -(jnp.maximum(z, 0.0) + jnp.log(1.0 + jnp.exp(-jnp.abs(z))))
        if masked:
            log_keep = jnp.where(causal, log_keep, 0.0)
        hi = log_keep.astype(BF16)
        lo = (log_keep - hi.astype(F32)).astype(BF16)
        csum = jnp.dot(jnp.concatenate([hi, lo], axis=1), suffix2,
                       preferred_element_type=F32)
        a = jnp.exp(z + csum + run)
        if masked:
            a = jnp.where(causal, a, 0.0)
        acc = acc + jnp.dot(a.astype(BF16), vs, preferred_element_type=F32)
        run = run + csum[:, :1]
        return run, acc

    run0 = jnp.zeros((t, 1), F32)
    acc0 = jnp.zeros((t, SB_HEAD_DIM), F32)
    run, acc = tile(i, run0, acc0, True)

    def step(n, carry):
        return tile(i - 1 - n, carry[0], carry[1], False)

    run, acc = lax.fori_loop(0, i, step, (run, acc))
    o_ref[...] = acc.astype(o_ref.dtype)


def _sb_attention(qkv):
    b, s, _ = qkv.shape
    t = SB_TILE
    hd = SB_HEAD_DIM
    vmem = (4 * _nbytes((s, hd), BF16) + 4 * _nbytes((t, hd), BF16)
            + 12 * _nbytes((t, t), F32))
    return pl.pallas_call(
        _sb_body,
        grid=(b, SB_HEADS, s // t),
        in_specs=[
            pl.BlockSpec((None, t, hd), lambda bi, h, i: (bi, i, h)),
            pl.BlockSpec((None, s, hd), lambda bi, h, i: (bi, 0, SB_HEADS + h)),
            pl.BlockSpec((None, s, hd), lambda bi, h, i: (bi, 0, 2 * SB_HEADS + h)),
        ],
        out_specs=pl.BlockSpec((None, t, hd), lambda bi, h, i: (bi, i, h)),
        out_shape=jax.ShapeDtypeStruct((b, s, SB_HEADS * hd), BF16),
        compiler_params=_params(3, 2 * vmem),
    )(qkv, qkv, qkv)


def _gm_body(u_ref, v_ref, vg_ref, ws_ref, bs_ref, o_ref, vn_ref):
    rows = u_ref.shape[0]
    c = GM_CHUNK
    row = lax.broadcasted_iota(jnp.int32, (c, c), 0)
    col = lax.broadcasted_iota(jnp.int32, (c, c), 1)
    lower = row >= col
    for ci in range(rows // c):
        sl = slice(ci * c, (ci + 1) * c)
        vf = v_ref[sl, :].astype(F32)
        mu = jnp.mean(vf, axis=-1, keepdims=True)
        xc = vf - mu
        y = xc * lax.rsqrt(jnp.mean(xc * xc, axis=-1, keepdims=True) + EPS)
        vn_ref[...] = (y * vg_ref[...]).astype(BF16)
        for g in range(GM_GROUPS):
            gs = slice(g * GM_GROUP_DIM, (g + 1) * GM_GROUP_DIM)
            ws = jnp.where(lower, ws_ref[g], 0.0).astype(BF16)
            mixed = jnp.dot(ws, vn_ref[:, gs], preferred_element_type=F32) + bs_ref[:, gs]
            o_ref[sl, gs] = (u_ref[sl, gs].astype(F32) * mixed).astype(o_ref.dtype)


def _gm_gate(uv, v_gain, w_s, b_full, *, tm=512):
    m, two_w = uv.shape
    w = two_w // 2
    c = GM_CHUNK
    vmem = (6 * _nbytes((tm, w), BF16) + 2 * _nbytes((GM_GROUPS, c, c), F32)
            + 2 * _nbytes((c, w), F32) + 8 * _nbytes((c, w), F32))
    return pl.pallas_call(
        _gm_body,
        grid=(m // tm,),
        in_specs=[
            pl.BlockSpec((tm, w), lambda i: (i, 0)),
            pl.BlockSpec((tm, w), lambda i: (i, 1)),
            pl.BlockSpec((1, w), lambda i: (0, 0)),
            pl.BlockSpec((GM_GROUPS, c, c), lambda i: (0, 0, 0)),
            pl.BlockSpec((c, w), lambda i: (0, 0)),
        ],
        out_specs=pl.BlockSpec((tm, w), lambda i: (i, 0)),
        out_shape=jax.ShapeDtypeStruct((m, w), BF16),
        scratch_shapes=[pltpu.VMEM((c, w), BF16)],
        compiler_params=_params(1, 2 * vmem),
    )(uv, uv, v_gain.reshape(1, w), w_s, b_full)


def _ret_body(q_ref, k_ref, v_ref, g_ref, cos_ref, sin_ref, intra_ref, qd_ref, kd_ref,
              cd_ref, gn_ref, o_ref, state_ref):
    c = RET_CHUNK
    half = RET_QK_DIM // 2

    @pl.when(pl.program_id(2) == 0)
    def _():
        state_ref[...] = jnp.zeros_like(state_ref)

    def rotate(x, cos, sin):
        x1, x2 = x[:, :half], x[:, half:]
        return jnp.concatenate([x1 * cos - x2 * sin, x1 * sin + x2 * cos], axis=1)

    for ci in range(q_ref.shape[0] // c):
        sl = slice(ci * c, (ci + 1) * c)
        cos = cos_ref[sl, :]
        sin = sin_ref[sl, :]
        qr = rotate(q_ref[sl, :].astype(F32), cos, sin)
        kr = rotate(k_ref[sl, :].astype(F32), cos, sin) * (RET_QK_DIM ** -0.5)
        qb = qr.astype(BF16)
        kb = kr.astype(BF16)
        vb = v_ref[sl, :]
        scores = lax.dot_general(qb, kb, (((1,), (1,)), ((), ())),
                                 preferred_element_type=F32) * intra_ref[...]
        inner = jnp.dot(scores.astype(BF16), vb, preferred_element_type=F32)
        state = state_ref[...]
        cross = jnp.dot(qb, state.astype(BF16), preferred_element_type=F32) * qd_ref[...]
        kdec = (kr * kd_ref[...]).astype(BF16)
        state_ref[...] = state * cd_ref[...] + lax.dot_general(
            kdec, vb, (((0,), (0,)), ((), ())), preferred_element_type=F32)
        o = inner + cross
        y = _rms_rows(o, gn_ref[...])
        o_ref[sl, :] = (jax.nn.silu(g_ref[sl, :].astype(F32)) * y).astype(o_ref.dtype)


def _retention(qkvg, gn_gain, *, tc=512):
    b, s, _ = qkvg.shape
    hq, hv, c = RET_QK_DIM, RET_V_DIM, RET_CHUNK
    nh = RET_HEADS
    half = hq // 2
    inv = ROPE_BASE ** (-jnp.linspace(0.0, 1.0, half, dtype=F32))
    ang = jnp.arange(s).astype(F32)[:, None] * inv[None, :]
    cos, sin = jnp.cos(ang), jnp.sin(ang)
    log_gamma = jnp.log(1.0 - 2.0 ** (-5.0 - jnp.arange(nh, dtype=F32)))
    idx = jnp.arange(c, dtype=F32)
    diff = idx[:, None] - idx[None, :]
    intra = jnp.where(diff[None] >= 0,
                      jnp.exp(jnp.maximum(diff, 0.0)[None] * log_gamma[:, None, None]), 0.0)
    qd = jnp.exp((idx + 1.0)[None, :] * log_gamma[:, None])[:, :, None]
    kd = jnp.exp((c - 1.0 - idx)[None, :] * log_gamma[:, None])[:, :, None]
    cd = jnp.broadcast_to(jnp.exp(c * log_gamma)[:, None, None], (nh, 1, hv))

    vmem = (4 * _nbytes((tc, hq), BF16) + 6 * _nbytes((tc, hv), BF16)
            + 4 * _nbytes((tc, half), F32) + 2 * _nbytes((c, c), F32)
            + 4 * _nbytes((c, 128), F32) + 3 * _nbytes((hq, hv), F32)
            + 10 * _nbytes((c, hv), F32))
    return pl.pallas_call(
        _ret_body,
        grid=(b, nh, s // tc),
        in_specs=[
            pl.BlockSpec((None, tc, hq), lambda bi, h, ci: (bi, ci, h)),
            pl.BlockSpec((None, tc, hq), lambda bi, h, ci: (bi, ci, nh + h)),
            pl.BlockSpec((None, tc, hv), lambda bi, h, ci: (bi, ci, nh + h)),
            pl.BlockSpec((None, tc, hv), lambda bi, h, ci: (bi, ci, 2 * nh + h)),
            pl.BlockSpec((tc, half), lambda bi, h, ci: (ci, 0)),
            pl.BlockSpec((tc, half), lambda bi, h, ci: (ci, 0)),
            pl.BlockSpec((None, c, c), lambda bi, h, ci: (h, 0, 0)),
            pl.BlockSpec((None, c, 1), lambda bi, h, ci: (h, 0, 0)),
            pl.BlockSpec((None, c, 1), lambda bi, h, ci: (h, 0, 0)),
            pl.BlockSpec((None, 1, hv), lambda bi, h, ci: (h, 0, 0)),
            pl.BlockSpec((1, hv), lambda bi, h, ci: (0, h)),
        ],
        out_specs=pl.BlockSpec((None, tc, hv), lambda bi, h, ci: (bi, ci, h)),
        out_shape=jax.ShapeDtypeStruct((b, s, nh * hv), BF16),
        scratch_shapes=[pltpu.VMEM((hq, hv), F32)],
        compiler_params=_params(3, 2 * vmem),
    )(qkvg, qkvg, qkvg, qkvg, cos, sin, intra, qd, kd, cd, gn_gain.reshape(1, nh * hv))


def kernel(x, sb_norm, sb_w_qkv, sb_w_o, gm_norm, gm_w_in, gm_v_norm, gm_w_s, gm_b_s, gm_w_o,
           ret_norm, ret_w_qkvg, ret_gn, ret_w_o, ffn_norm, ffn_w_gate_up, ffn_w_down,
           final_norm):
    b, s, d = x.shape
    n = b * s
    depth = ffn_norm.shape[0]
    sb_w_qkv, sb_w_o, gm_w_in, gm_w_o, ret_w_qkvg, ret_w_o, ffn_w_gate_up, ffn_w_down = (
        w.astype(BF16) for w in (sb_w_qkv, sb_w_o, gm_w_in, gm_w_o, ret_w_qkvg, ret_w_o,
                                 ffn_w_gate_up, ffn_w_down))
    h = x.reshape(n, d)
    for i in range(depth):
        kind, j = i % N_MIXERS, i // N_MIXERS
        if kind == 0:
            qkv = _matmul(h, sb_w_qkv, j, gain=sb_norm[j], out_dtype=BF16)
            o = _sb_attention(qkv.reshape(b, s, 3 * d))
            h = _matmul(o.reshape(n, d), sb_w_o, j, residual=h, out_dtype=F32)
        elif kind == 1:
            uv = _matmul(h, gm_w_in, j, gain=gm_norm[j], act="gelu", out_dtype=BF16)
            b_full = jnp.repeat(gm_b_s[j].T, GM_GROUP_DIM, axis=1)
            gated = _gm_gate(uv, gm_v_norm[j], gm_w_s[j], b_full)
            h = _matmul(gated, gm_w_o, j, residual=h, out_dtype=F32)
        else:
            qkvg = _matmul(h, ret_w_qkvg, j, gain=ret_norm[j], out_dtype=BF16)
            o = _retention(qkvg.reshape(b, s, 6 * d), ret_gn[j])
            h = _matmul(o.reshape(n, 2 * d), ret_w_o, j, residual=h, out_dtype=F32, tn=512)
        h = _ffn(h, ffn_norm[i], ffn_w_gate_up, ffn_w_down, i,
                 final_gain=final_norm if i == depth - 1 else None)
    return h.reshape(b, s, d)
```

```python
import functools

import jax
import jax.numpy as jnp
from jax import lax
from jax.experimental import pallas as pl
from jax.experimental.pallas import tpu as pltpu

F32 = jnp.float32
BF16 = jnp.bfloat16

EPS = 1e-6
N_MIXERS = 3
SB_HEADS = 16
SB_HEAD_DIM = 128
SB_TILE = 256
SB_HEADS_PER_STEP = 4
LOG2_E = 1.4426950408889634
GM_GROUPS = 16
GM_GROUP_DIM = 128
GM_CHUNK = 128
RET_HEADS = 8
RET_QK_DIM = 256
RET_V_DIM = 512
RET_CHUNK = 128
ROPE_BASE = 10000.0

V7X_VMEM_LIMIT_BYTES = 60000 * 1024
NORM_ROWS = 256


def _nbytes(shape, dtype):
    n = 1
    for s in shape:
        n *= s
    return n * jnp.dtype(dtype).itemsize


def _params(n_axes, vmem_bytes):
    return pltpu.CompilerParams(
        dimension_semantics=("arbitrary",) * n_axes,
        vmem_limit_bytes=int(min(vmem_bytes, V7X_VMEM_LIMIT_BYTES)),
    )


def _rms_rows(xf, gain):
    y = xf * lax.rsqrt(jnp.mean(xf * xf, axis=-1, keepdims=True) + EPS)
    return y * gain


def _norm_into(x_ref, g_ref, xn_ref):
    rows = x_ref.shape[0]
    step = min(NORM_ROWS, rows)

    def body(r, carry):
        sl = pl.ds(pl.multiple_of(r * step, step), step)
        xn_ref[sl, :] = _rms_rows(x_ref[sl, :], g_ref[...]).astype(xn_ref.dtype)
        return carry

    lax.fori_loop(0, rows // step, body, 0)


def _mm_body(*refs, norm, act, has_res, has_scale):
    it = iter(refs)
    x_ref = next(it)
    g_ref = next(it) if norm else None
    w_ref = next(it)
    s_ref = next(it) if has_scale else None
    r_ref = next(it) if has_res else None
    o_ref = next(it)
    xn_ref = next(it) if norm else None

    if norm:
        @pl.when(pl.program_id(1) == 0)
        def _():
            _norm_into(x_ref, g_ref, xn_ref)
        lhs = xn_ref[...]
    else:
        lhs = x_ref[...]
    acc = jnp.dot(lhs, w_ref[...], preferred_element_type=F32)
    if act == "gelu":
        acc = jax.nn.gelu(acc)
    if has_scale:
        acc = acc * s_ref[...]
    if has_res:
        acc = acc + r_ref[...]
    o_ref[...] = acc.astype(o_ref.dtype)


def _matmul(x, w_stack, layer, *, gain=None, residual=None, act=None, col_scale=None,
            out_dtype, tm=1024, tn=1024):
    m, k = x.shape
    n = w_stack.shape[-1]
    norm = gain is not None
    has_res = residual is not None
    has_scale = col_scale is not None
    in_specs = [pl.BlockSpec((tm, k), lambda i, j: (i, 0))]
    args = [x]
    if norm:
        in_specs.append(pl.BlockSpec((1, k), lambda i, j: (0, 0)))
        args.append(gain.reshape(1, k))
    in_specs.append(pl.BlockSpec((None, k, tn), lambda i, j: (layer, 0, j)))
    args.append(w_stack)
    if has_scale:
        in_specs.append(pl.BlockSpec((1, tn), lambda i, j: (0, j)))
        args.append(col_scale.reshape(1, n))
    if has_res:
        in_specs.append(pl.BlockSpec((tm, tn), lambda i, j: (i, j)))
        args.append(residual)
    scratch = [pltpu.VMEM((tm, k), BF16)] if norm else []
    vmem = (2 * _nbytes((tm, k), x.dtype) + 2 * _nbytes((k, tn), BF16)
            + 2 * _nbytes((tm, tn), out_dtype) + 2 * _nbytes((tm, tn), F32)
            + (2 * _nbytes((tm, tn), F32) if has_res else 0)
            + (_nbytes((tm, k), BF16) + 4 * _nbytes((NORM_ROWS, k), F32) if norm else 0))
    return pl.pallas_call(
        functools.partial(_mm_body, norm=norm, act=act, has_res=has_res,
                          has_scale=has_scale),
        grid=(m // tm, n // tn),
        in_specs=in_specs,
        out_specs=pl.BlockSpec((tm, tn), lambda i, j: (i, j)),
        out_shape=jax.ShapeDtypeStruct((m, n), out_dtype),
        scratch_shapes=scratch,
        compiler_params=_params(2, vmem),
    )(*args)


def _ffn_body(*refs, final):
    if final:
        h_ref, g_ref, wg_ref, wu_ref, wd_ref, fg_ref, o_ref, xn_ref = refs
    else:
        h_ref, g_ref, wg_ref, wu_ref, wd_ref, o_ref, xn_ref = refs
        fg_ref = None
    f = pl.program_id(1)

    @pl.when(f == 0)
    def _():
        _norm_into(h_ref, g_ref, xn_ref)
        o_ref[...] = h_ref[...]

    xn = xn_ref[...]
    gate = jnp.dot(xn, wg_ref[...], preferred_element_type=F32)
    up = jnp.dot(xn, wu_ref[...], preferred_element_type=F32)
    act = (jax.nn.silu(gate) * up).astype(BF16)
    o_ref[...] += jnp.dot(act, wd_ref[...], preferred_element_type=F32)

    if final:
        @pl.when(f == pl.num_programs(1) - 1)
        def _():
            rows = o_ref.shape[0]
            step = min(NORM_ROWS, rows)

            def body(r, carry):
                sl = pl.ds(pl.multiple_of(r * step, step), step)
                o_ref[sl, :] = _rms_rows(o_ref[sl, :], fg_ref[...])
                return carry

            lax.fori_loop(0, rows // step, body, 0)


def _ffn(h, gain, w_gu_stack, w_d_stack, layer, *, final_gain=None, tm=512, tf=512):
    m, d = h.shape
    ff = w_d_stack.shape[1]
    nf = ff // tf
    final = final_gain is not None
    in_specs = [
        pl.BlockSpec((tm, d), lambda i, f: (i, 0)),
        pl.BlockSpec((1, d), lambda i, f: (0, 0)),
        pl.BlockSpec((None, d, tf), lambda i, f: (layer, 0, f)),
        pl.BlockSpec((None, d, tf), lambda i, f: (layer, 0, nf + f)),
        pl.BlockSpec((None, tf, d), lambda i, f: (layer, f, 0)),
    ]
    args = [h, gain.reshape(1, d), w_gu_stack, w_gu_stack, w_d_stack]
    if final:
        in_specs.append(pl.BlockSpec((1, d), lambda i, f: (0, 0)))
        args.append(final_gain.reshape(1, d))
    vmem = (4 * _nbytes((tm, d), F32) + _nbytes((tm, d), BF16)
            + 6 * _nbytes((d, tf), BF16) + 3 * _nbytes((tm, tf), F32)
            + 2 * _nbytes((tm, d), F32) + 4 * _nbytes((NORM_ROWS, d), F32))
    return pl.pallas_call(
        functools.partial(_ffn_body, final=final),
        grid=(m // tm, nf),
        in_specs=in_specs,
        out_specs=pl.BlockSpec((tm, d), lambda i, f: (i, 0)),
        out_shape=jax.ShapeDtypeStruct((m, d), F32),
        scratch_shapes=[pltpu.VMEM((tm, d), BF16)],
        compiler_params=_params(2, vmem),
    )(*args)


def _sb_body(q_ref, k_ref, v_ref, o_ref):
    t = SB_TILE
    hd = SB_HEAD_DIM
    i = pl.program_id(2)
    sign_bit = jnp.int32(-2 ** 31)
    row = lax.broadcasted_iota(jnp.int32, (t, t), 0)
    col = lax.broadcasted_iota(jnp.int32, (t, t), 1)
    suffix = (row >= col).astype(BF16)
    suffix2 = jnp.concatenate([suffix, suffix], axis=0)
    causal = col < row

    def tile(kb, carry, masked):
        sl = pl.ds(pl.multiple_of(kb * t, t), t)
        heads = range(SB_HEADS_PER_STEP)
        hs = [slice(hh * hd, (hh + 1) * hd) for hh in heads]
        ws = [lax.dot_general(q_ref[:, hs[hh]], k_ref[sl, hs[hh]], (((1,), (1,)), ((), ())),
                              preferred_element_type=F32) for hh in heads]
        csums = []
        for hh in heads:
            w = ws[hh]
            neg_abs = lax.bitcast_convert_type(
                lax.bitcast_convert_type(w, jnp.int32) | sign_bit, F32)
            keep = jnp.maximum(w, 0.0) + jnp.log(1.0 + jnp.exp2(neg_abs)) * LOG2_E
            if masked:
                keep = jnp.where(causal, keep, 0.0)
            hi = keep.astype(BF16)
            lo = (keep - hi.astype(F32)).astype(BF16)
            csums.append(jnp.dot(jnp.concatenate([hi, lo], axis=1), suffix2,
                                 preferred_element_type=F32))
        out = []
        for hh in heads:
            run, acc = carry[hh]
            a = jnp.exp2(ws[hh] - csums[hh] - run)
            if masked:
                a = jnp.where(causal, a, 0.0)
            acc = acc + jnp.dot(a.astype(BF16), v_ref[sl, hs[hh]],
                                preferred_element_type=F32)
            out.append((run + csums[hh][:, :1], acc))
        return tuple(out)

    zero = (jnp.zeros((t, 1), F32), jnp.zeros((t, hd), F32))
    carry = tile(i, (zero,) * SB_HEADS_PER_STEP, True)
    carry = lax.fori_loop(0, i, lambda n, c: tile(i - 1 - n, c, False), carry)
    for hh in range(SB_HEADS_PER_STEP):
        o_ref[:, hh * hd:(hh + 1) * hd] = carry[hh][1].astype(o_ref.dtype)


def _sb_attention(qkv):
    b, s, _ = qkv.shape
    t = SB_TILE
    wd = SB_HEAD_DIM * SB_HEADS_PER_STEP
    groups = SB_HEADS // SB_HEADS_PER_STEP
    vmem = (4 * _nbytes((s, wd), BF16) + 4 * _nbytes((t, wd), BF16)
            + 12 * SB_HEADS_PER_STEP * _nbytes((t, t), F32))
    return pl.pallas_call(
        _sb_body,
        grid=(b, groups, s // t),
        in_specs=[
            pl.BlockSpec((None, t, wd), lambda bi, h, i: (bi, i, h)),
            pl.BlockSpec((None, s, wd), lambda bi, h, i: (bi, 0, groups + h)),
            pl.BlockSpec((None, s, wd), lambda bi, h, i: (bi, 0, 2 * groups + h)),
        ],
        out_specs=pl.BlockSpec((None, t, wd), lambda bi, h, i: (bi, i, h)),
        out_shape=jax.ShapeDtypeStruct((b, s, SB_HEADS * SB_HEAD_DIM), BF16),
        compiler_params=_params(3, 2 * vmem),
    )(qkv, qkv, qkv)


def _gm_body(u_ref, v_ref, vg_ref, ws_ref, bs_ref, o_ref, vn_ref):
    rows = u_ref.shape[0]
    c = GM_CHUNK
    row = lax.broadcasted_iota(jnp.int32, (c, c), 0)
    col = lax.broadcasted_iota(jnp.int32, (c, c), 1)
    lower = row >= col
    for ci in range(rows // c):
        sl = slice(ci * c, (ci + 1) * c)
        vf = v_ref[sl, :].astype(F32)
        mu = jnp.mean(vf, axis=-1, keepdims=True)
        xc = vf - mu
        y = xc * lax.rsqrt(jnp.mean(xc * xc, axis=-1, keepdims=True) + EPS)
        vn_ref[...] = (y * vg_ref[...]).astype(BF16)
        for g in range(GM_GROUPS):
            gs = slice(g * GM_GROUP_DIM, (g + 1) * GM_GROUP_DIM)
            ws = jnp.where(lower, ws_ref[g], 0.0).astype(BF16)
            mixed = jnp.dot(ws, vn_ref[:, gs], preferred_element_type=F32) + bs_ref[:, gs]
            o_ref[sl, gs] = (u_ref[sl, gs].astype(F32) * mixed).astype(o_ref.dtype)


def _gm_gate(uv, v_gain, w_s, b_full, *, tm=512):
    m, two_w = uv.shape
    w = two_w // 2
    c = GM_CHUNK
    vmem = (6 * _nbytes((tm, w), BF16) + 2 * _nbytes((GM_GROUPS, c, c), F32)
            + 2 * _nbytes((c, w), F32) + 8 * _nbytes((c, w), F32))
    return pl.pallas_call(
        _gm_body,
        grid=(m // tm,),
        in_specs=[
            pl.BlockSpec((tm, w), lambda i: (i, 0)),
            pl.BlockSpec((tm, w), lambda i: (i, 1)),
            pl.BlockSpec((1, w), lambda i: (0, 0)),
            pl.BlockSpec((GM_GROUPS, c, c), lambda i: (0, 0, 0)),
            pl.BlockSpec((c, w), lambda i: (0, 0)),
        ],
        out_specs=pl.BlockSpec((tm, w), lambda i: (i, 0)),
        out_shape=jax.ShapeDtypeStruct((m, w), BF16),
        scratch_shapes=[pltpu.VMEM((c, w), BF16)],
        compiler_params=_params(1, 2 * vmem),
    )(uv, uv, v_gain.reshape(1, w), w_s, b_full)


def _ret_body(q_ref, k_ref, v_ref, g_ref, cos_ref, sin_ref, intra_ref, qd_ref, kd_ref,
              cd_ref, gn_ref, o_ref, state_ref):
    c = RET_CHUNK
    half = RET_QK_DIM // 2

    @pl.when(pl.program_id(2) == 0)
    def _():
        state_ref[...] = jnp.zeros_like(state_ref)

    def rotate(x, cos, sin):
        x1, x2 = x[:, :half], x[:, half:]
        return jnp.concatenate([x1 * cos - x2 * sin, x1 * sin + x2 * cos], axis=1)

    for ci in range(q_ref.shape[0] // c):
        sl = slice(ci * c, (ci + 1) * c)
        cos = cos_ref[sl, :]
        sin = sin_ref[sl, :]
        qr = rotate(q_ref[sl, :].astype(F32), cos, sin)
        kr = rotate(k_ref[sl, :].astype(F32), cos, sin) * (RET_QK_DIM ** -0.5)
        qb = qr.astype(BF16)
        kb = kr.astype(BF16)
        vb = v_ref[sl, :]
        scores = lax.dot_general(qb, kb, (((1,), (1,)), ((), ())),
                                 preferred_element_type=F32) * intra_ref[...]
        inner = jnp.dot(scores.astype(BF16), vb, preferred_element_type=F32)
        state = state_ref[...]
        cross = jnp.dot(qb, state.astype(BF16), preferred_element_type=F32) * qd_ref[...]
        kdec = (kr * kd_ref[...]).astype(BF16)
        state_ref[...] = state * cd_ref[...] + lax.dot_general(
            kdec, vb, (((0,), (0,)), ((), ())), preferred_element_type=F32)
        o = inner + cross
        y = _rms_rows(o, gn_ref[...])
        o_ref[sl, :] = (jax.nn.silu(g_ref[sl, :].astype(F32)) * y).astype(o_ref.dtype)


def _retention(qkvg, gn_gain, *, tc=512):
    b, s, _ = qkvg.shape
    hq, hv, c = RET_QK_DIM, RET_V_DIM, RET_CHUNK
    nh = RET_HEADS
    half = hq // 2
    inv = ROPE_BASE ** (-jnp.linspace(0.0, 1.0, half, dtype=F32))
    ang = jnp.arange(s).astype(F32)[:, None] * inv[None, :]
    cos, sin = jnp.cos(ang), jnp.sin(ang)
    log_gamma = jnp.log(1.0 - 2.0 ** (-5.0 - jnp.arange(nh, dtype=F32)))
    idx = jnp.arange(c, dtype=F32)
    diff = idx[:, None] - idx[None, :]
    intra = jnp.where(diff[None] >= 0,
                      jnp.exp(jnp.maximum(diff, 0.0)[None] * log_gamma[:, None, None]), 0.0)
    qd = jnp.exp((idx + 1.0)[None, :] * log_gamma[:, None])[:, :, None]
    kd = jnp.exp((c - 1.0 - idx)[None, :] * log_gamma[:, None])[:, :, None]
    cd = jnp.broadcast_to(jnp.exp(c * log_gamma)[:, None, None], (nh, 1, hv))

    vmem = (4 * _nbytes((tc, hq), BF16) + 6 * _nbytes((tc, hv), BF16)
            + 4 * _nbytes((tc, half), F32) + 2 * _nbytes((c, c), F32)
            + 4 * _nbytes((c, 128), F32) + 3 * _nbytes((hq, hv), F32)
            + 10 * _nbytes((c, hv), F32))
    return pl.pallas_call(
        _ret_body,
        grid=(b, nh, s // tc),
        in_specs=[
            pl.BlockSpec((None, tc, hq), lambda bi, h, ci: (bi, ci, h)),
            pl.BlockSpec((None, tc, hq), lambda bi, h, ci: (bi, ci, nh + h)),
            pl.BlockSpec((None, tc, hv), lambda bi, h, ci: (bi, ci, nh + h)),
            pl.BlockSpec((None, tc, hv), lambda bi, h, ci: (bi, ci, 2 * nh + h)),
            pl.BlockSpec((tc, half), lambda bi, h, ci: (ci, 0)),
            pl.BlockSpec((tc, half), lambda bi, h, ci: (ci, 0)),
            pl.BlockSpec((None, c, c), lambda bi, h, ci: (h, 0, 0)),
            pl.BlockSpec((None, c, 1), lambda bi, h, ci: (h, 0, 0)),
            pl.BlockSpec((None, c, 1), lambda bi, h, ci: (h, 0, 0)),
            pl.BlockSpec((None, 1, hv), lambda bi, h, ci: (h, 0, 0)),
            pl.BlockSpec((1, hv), lambda bi, h, ci: (0, h)),
        ],
        out_specs=pl.BlockSpec((None, tc, hv), lambda bi, h, ci: (bi, ci, h)),
        out_shape=jax.ShapeDtypeStruct((b, s, nh * hv), BF16),
        scratch_shapes=[pltpu.VMEM((hq, hv), F32)],
        compiler_params=_params(3, 2 * vmem),
    )(qkvg, qkvg, qkvg, qkvg, cos, sin, intra, qd, kd, cd, gn_gain.reshape(1, nh * hv))


def kernel(x, sb_norm, sb_w_qkv, sb_w_o, gm_norm, gm_w_in, gm_v_norm, gm_w_s, gm_b_s, gm_w_o,
           ret_norm, ret_w_qkvg, ret_gn, ret_w_o, ffn_norm, ffn_w_gate_up, ffn_w_down,
           final_norm):
    b, s, d = x.shape
    n = b * s
    depth = ffn_norm.shape[0]
    sb_w_qkv, sb_w_o, gm_w_in, gm_w_o, ret_w_qkvg, ret_w_o, ffn_w_gate_up, ffn_w_down = (
        w.astype(BF16) for w in (sb_w_qkv, sb_w_o, gm_w_in, gm_w_o, ret_w_qkvg, ret_w_o,
                                 ffn_w_gate_up, ffn_w_down))
    h = x.reshape(n, d)
    for i in range(depth):
        kind, j = i % N_MIXERS, i // N_MIXERS
        if kind == 0:
            q_scale = jnp.concatenate([
                jnp.full((d,), (SB_HEAD_DIM ** -0.5) * LOG2_E, F32), jnp.ones((2 * d,), F32)])
            qkv = _matmul(h, sb_w_qkv, j, gain=sb_norm[j], col_scale=q_scale, out_dtype=BF16)
            o = _sb_attention(qkv.reshape(b, s, 3 * d))
            h = _matmul(o.reshape(n, d), sb_w_o, j, residual=h, out_dtype=F32)
        elif kind == 1:
            uv = _matmul(h, gm_w_in, j, gain=gm_norm[j], act="gelu", out_dtype=BF16)
            b_full = jnp.repeat(gm_b_s[j].T, GM_GROUP_DIM, axis=1)
            gated = _gm_gate(uv, gm_v_norm[j], gm_w_s[j], b_full)
            h = _matmul(gated, gm_w_o, j, residual=h, out_dtype=F32)
        else:
            qkvg = _matmul(h, ret_w_qkvg, j, gain=ret_norm[j], out_dtype=BF16)
            o = _retention(qkvg.reshape(b, s, 6 * d), ret_gn[j])
            h = _matmul(o.reshape(n, 2 * d), ret_w_o, j, residual=h, out_dtype=F32, tn=512)
        h = _ffn(h, ffn_norm[i], ffn_w_gate_up, ffn_w_down, i,
                 final_gain=final_norm if i == depth - 1 else None)
    return h.reshape(b, s, d)
```

```python
import functools

import jax
import jax.numpy as jnp
from jax import lax
from jax.experimental import pallas as pl
from jax.experimental.pallas import tpu as pltpu

F32 = jnp.float32
BF16 = jnp.bfloat16

EPS = 1e-6
N_MIXERS = 3
SB_HEADS = 16
SB_HEAD_DIM = 128
SB_TILE = 256
SB_HEADS_PER_STEP = 4
LOG2_E = 1.4426950408889634
GM_GROUPS = 16
GM_GROUP_DIM = 128
GM_CHUNK = 128
RET_HEADS = 8
RET_QK_DIM = 256
RET_V_DIM = 512
RET_CHUNK = 128
ROPE_BASE = 10000.0

V7X_VMEM_LIMIT_BYTES = 60000 * 1024
NORM_ROWS = 256


def _nbytes(shape, dtype):
    n = 1
    for s in shape:
        n *= s
    return n * jnp.dtype(dtype).itemsize


def _params(n_axes, vmem_bytes):
    return pltpu.CompilerParams(
        dimension_semantics=("arbitrary",) * n_axes,
        vmem_limit_bytes=int(min(vmem_bytes, V7X_VMEM_LIMIT_BYTES)),
    )


def _rms_rows(xf, gain):
    y = xf * lax.rsqrt(jnp.mean(xf * xf, axis=-1, keepdims=True) + EPS)
    return y * gain


def _norm_into(x_ref, g_ref, xn_ref):
    rows = x_ref.shape[0]
    step = min(NORM_ROWS, rows)

    def body(r, carry):
        sl = pl.ds(pl.multiple_of(r * step, step), step)
        xn_ref[sl, :] = _rms_rows(x_ref[sl, :], g_ref[...]).astype(xn_ref.dtype)
        return carry

    lax.fori_loop(0, rows // step, body, 0)


def _mm_body(*refs, norm, act, has_res, has_scale):
    it = iter(refs)
    x_ref = next(it)
    g_ref = next(it) if norm else None
    w_ref = next(it)
    s_ref = next(it) if has_scale else None
    r_ref = next(it) if has_res else None
    o_ref = next(it)
    xn_ref = next(it) if norm else None

    if norm:
        @pl.when(pl.program_id(1) == 0)
        def _():
            _norm_into(x_ref, g_ref, xn_ref)
        lhs = xn_ref[...]
    else:
        lhs = x_ref[...]
    acc = jnp.dot(lhs, w_ref[...], preferred_element_type=F32)
    if act == "gelu":
        acc = jax.nn.gelu(acc)
    if has_scale:
        acc = acc * s_ref[...]
    if has_res:
        acc = acc + r_ref[...]
    o_ref[...] = acc.astype(o_ref.dtype)


def _matmul(x, w_stack, layer, *, gain=None, residual=None, act=None, col_scale=None,
            out_dtype, tm=1024, tn=1024):
    m, k = x.shape
    n = w_stack.shape[-1]
    norm = gain is not None
    has_res = residual is not None
    has_scale = col_scale is not None
    in_specs = [pl.BlockSpec((tm, k), lambda i, j: (i, 0))]
    args = [x]
    if norm:
        in_specs.append(pl.BlockSpec((1, k), lambda i, j: (0, 0)))
        args.append(gain.reshape(1, k))
    in_specs.append(pl.BlockSpec((None, k, tn), lambda i, j: (layer, 0, j)))
    args.append(w_stack)
    if has_scale:
        in_specs.append(pl.BlockSpec((1, tn), lambda i, j: (0, j)))
        args.append(col_scale.reshape(1, n))
    if has_res:
        in_specs.append(pl.BlockSpec((tm, tn), lambda i, j: (i, j)))
        args.append(residual)
    scratch = [pltpu.VMEM((tm, k), BF16)] if norm else []
    vmem = (2 * _nbytes((tm, k), x.dtype) + 2 * _nbytes((k, tn), BF16)
            + 2 * _nbytes((tm, tn), out_dtype) + 2 * _nbytes((tm, tn), F32)
            + (2 * _nbytes((tm, tn), F32) if has_res else 0)
            + (_nbytes((tm, k), BF16) + 4 * _nbytes((NORM_ROWS, k), F32) if norm else 0))
    return pl.pallas_call(
        functools.partial(_mm_body, norm=norm, act=act, has_res=has_res,
                          has_scale=has_scale),
        grid=(m // tm, n // tn),
        in_specs=in_specs,
        out_specs=pl.BlockSpec((tm, tn), lambda i, j: (i, j)),
        out_shape=jax.ShapeDtypeStruct((m, n), out_dtype),
        scratch_shapes=scratch,
        compiler_params=_params(2, vmem),
    )(*args)


def _ffn_body(*refs, final):
    if final:
        h_ref, g_ref, wg_ref, wu_ref, wd_ref, fg_ref, o_ref, xn_ref = refs
    else:
        h_ref, g_ref, wg_ref, wu_ref, wd_ref, o_ref, xn_ref = refs
        fg_ref = None
    f = pl.program_id(1)

    @pl.when(f == 0)
    def _():
        _norm_into(h_ref, g_ref, xn_ref)
        o_ref[...] = h_ref[...]

    xn = xn_ref[...]
    gate = jnp.dot(xn, wg_ref[...], preferred_element_type=F32)
    up = jnp.dot(xn, wu_ref[...], preferred_element_type=F32)
    act = (jax.nn.silu(gate) * up).astype(BF16)
    o_ref[...] += jnp.dot(act, wd_ref[...], preferred_element_type=F32)

    if final:
        @pl.when(f == pl.num_programs(1) - 1)
        def _():
            rows = o_ref.shape[0]
            step = min(NORM_ROWS, rows)

            def body(r, carry):
                sl = pl.ds(pl.multiple_of(r * step, step), step)
                o_ref[sl, :] = _rms_rows(o_ref[sl, :], fg_ref[...])
                return carry

            lax.fori_loop(0, rows // step, body, 0)


def _ffn(h, gain, w_gu_stack, w_d_stack, layer, *, final_gain=None, tm=512, tf=512):
    m, d = h.shape
    ff = w_d_stack.shape[1]
    nf = ff // tf
    final = final_gain is not None
    in_specs = [
        pl.BlockSpec((tm, d), lambda i, f: (i, 0)),
        pl.BlockSpec((1, d), lambda i, f: (0, 0)),
        pl.BlockSpec((None, d, tf), lambda i, f: (layer, 0, f)),
        pl.BlockSpec((None, d, tf), lambda i, f: (layer, 0, nf + f)),
        pl.BlockSpec((None, tf, d), lambda i, f: (layer, f, 0)),
    ]
    args = [h, gain.reshape(1, d), w_gu_stack, w_gu_stack, w_d_stack]
    if final:
        in_specs.append(pl.BlockSpec((1, d), lambda i, f: (0, 0)))
        args.append(final_gain.reshape(1, d))
    vmem = (4 * _nbytes((tm, d), F32) + _nbytes((tm, d), BF16)
            + 6 * _nbytes((d, tf), BF16) + 3 * _nbytes((tm, tf), F32)
            + 2 * _nbytes((tm, d), F32) + 4 * _nbytes((NORM_ROWS, d), F32))
    return pl.pallas_call(
        functools.partial(_ffn_body, final=final),
        grid=(m // tm, nf),
        in_specs=in_specs,
        out_specs=pl.BlockSpec((tm, d), lambda i, f: (i, 0)),
        out_shape=jax.ShapeDtypeStruct((m, d), F32),
        scratch_shapes=[pltpu.VMEM((tm, d), BF16)],
        compiler_params=_params(2, vmem),
    )(*args)


def _sb_body(q_ref, k_ref, v_ref, o_ref):
    t = SB_TILE
    hd = SB_HEAD_DIM
    i = pl.program_id(2)
    sign_bit = jnp.int32(-2 ** 31)
    row = lax.broadcasted_iota(jnp.int32, (t, t), 0)
    col = lax.broadcasted_iota(jnp.int32, (t, t), 1)
    after = (row > col).astype(BF16)
    causal = col < row

    def tile(kb, carry, masked):
        sl = pl.ds(pl.multiple_of(kb * t, t), t)
        heads = range(SB_HEADS_PER_STEP)
        hs = [slice(hh * hd, (hh + 1) * hd) for hh in heads]
        ws = [lax.dot_general(q_ref[:, hs[hh]], k_ref[sl, hs[hh]], (((1,), (1,)), ((), ())),
                              preferred_element_type=F32) for hh in heads]
        own, tails = [], []
        for hh in heads:
            w = ws[hh]
            neg_abs = lax.bitcast_convert_type(
                lax.bitcast_convert_type(w, jnp.int32) | sign_bit, F32)
            keep = jnp.maximum(w, 0.0) + jnp.log2(1.0 + jnp.exp2(neg_abs))
            if masked:
                keep = jnp.where(causal, keep, 0.0)
            own.append((w - keep, keep[:, :1]))
            tails.append(jnp.dot(keep.astype(BF16), after,
                                 preferred_element_type=F32))
        out = []
        for hh in heads:
            run, acc = carry[hh]
            a = jnp.exp2(own[hh][0] - tails[hh] - run)
            if masked:
                a = jnp.where(causal, a, 0.0)
            acc = acc + jnp.dot(a.astype(BF16), v_ref[sl, hs[hh]],
                                preferred_element_type=F32)
            out.append((run + (tails[hh][:, :1] + own[hh][1]), acc))
        return tuple(out)

    zero = (jnp.zeros((t, 1), F32), jnp.zeros((t, hd), F32))
    carry = tile(i, (zero,) * SB_HEADS_PER_STEP, True)
    carry = lax.fori_loop(0, i, lambda n, c: tile(i - 1 - n, c, False), carry)
    for hh in range(SB_HEADS_PER_STEP):
        o_ref[:, hh * hd:(hh + 1) * hd] = carry[hh][1].astype(o_ref.dtype)


def _sb_attention(qkv):
    b, s, _ = qkv.shape
    t = SB_TILE
    wd = SB_HEAD_DIM * SB_HEADS_PER_STEP
    groups = SB_HEADS // SB_HEADS_PER_STEP
    vmem = (4 * _nbytes((s, wd), BF16) + 4 * _nbytes((t, wd), BF16)
            + 12 * SB_HEADS_PER_STEP * _nbytes((t, t), F32))
    return pl.pallas_call(
        _sb_body,
        grid=(b, groups, s // t),
        in_specs=[
            pl.BlockSpec((None, t, wd), lambda bi, h, i: (bi, i, h)),
            pl.BlockSpec((None, s, wd), lambda bi, h, i: (bi, 0, groups + h)),
            pl.BlockSpec((None, s, wd), lambda bi, h, i: (bi, 0, 2 * groups + h)),
        ],
        out_specs=pl.BlockSpec((None, t, wd), lambda bi, h, i: (bi, i, h)),
        out_shape=jax.ShapeDtypeStruct((b, s, SB_HEADS * SB_HEAD_DIM), BF16),
        compiler_params=_params(3, 2 * vmem),
    )(qkv, qkv, qkv)


def _gm_body(u_ref, v_ref, vg_ref, ws_ref, bs_ref, o_ref, vn_ref):
    rows = u_ref.shape[0]
    c = GM_CHUNK
    row = lax.broadcasted_iota(jnp.int32, (c, c), 0)
    col = lax.broadcasted_iota(jnp.int32, (c, c), 1)
    lower = row >= col
    for ci in range(rows // c):
        sl = slice(ci * c, (ci + 1) * c)
        vf = v_ref[sl, :].astype(F32)
        mu = jnp.mean(vf, axis=-1, keepdims=True)
        xc = vf - mu
        y = xc * lax.rsqrt(jnp.mean(xc * xc, axis=-1, keepdims=True) + EPS)
        vn_ref[...] = (y * vg_ref[...]).astype(BF16)
        for g in range(GM_GROUPS):
            gs = slice(g * GM_GROUP_DIM, (g + 1) * GM_GROUP_DIM)
            ws = jnp.where(lower, ws_ref[g], 0.0).astype(BF16)
            mixed = jnp.dot(ws, vn_ref[:, gs], preferred_element_type=F32) + bs_ref[:, gs]
            o_ref[sl, gs] = (u_ref[sl, gs].astype(F32) * mixed).astype(o_ref.dtype)


def _gm_gate(uv, v_gain, w_s, b_full, *, tm=512):
    m, two_w = uv.shape
    w = two_w // 2
    c = GM_CHUNK
    vmem = (6 * _nbytes((tm, w), BF16) + 2 * _nbytes((GM_GROUPS, c, c), F32)
            + 2 * _nbytes((c, w), F32) + 8 * _nbytes((c, w), F32))
    return pl.pallas_call(
        _gm_body,
        grid=(m // tm,),
        in_specs=[
            pl.BlockSpec((tm, w), lambda i: (i, 0)),
            pl.BlockSpec((tm, w), lambda i: (i, 1)),
            pl.BlockSpec((1, w), lambda i: (0, 0)),
            pl.BlockSpec((GM_GROUPS, c, c), lambda i: (0, 0, 0)),
            pl.BlockSpec((c, w), lambda i: (0, 0)),
        ],
        out_specs=pl.BlockSpec((tm, w), lambda i: (i, 0)),
        out_shape=jax.ShapeDtypeStruct((m, w), BF16),
        scratch_shapes=[pltpu.VMEM((c, w), BF16)],
        compiler_params=_params(1, 2 * vmem),
    )(uv, uv, v_gain.reshape(1, w), w_s, b_full)


def _ret_body(q_ref, k_ref, v_ref, g_ref, cos_ref, sin_ref, intra_ref, qd_ref, kd_ref,
              cd_ref, gn_ref, o_ref, state_ref):
    c = RET_CHUNK
    half = RET_QK_DIM // 2

    @pl.when(pl.program_id(2) == 0)
    def _():
        state_ref[...] = jnp.zeros_like(state_ref)

    def rotate(x, cos, sin):
        x1, x2 = x[:, :half], x[:, half:]
        return jnp.concatenate([x1 * cos - x2 * sin, x1 * sin + x2 * cos], axis=1)

    for ci in range(q_ref.shape[0] // c):
        sl = slice(ci * c, (ci + 1) * c)
        cos = cos_ref[sl, :]
        sin = sin_ref[sl, :]
        qr = rotate(q_ref[sl, :].astype(F32), cos, sin)
        kr = rotate(k_ref[sl, :].astype(F32), cos, sin) * (RET_QK_DIM ** -0.5)
        qb = qr.astype(BF16)
        kb = kr.astype(BF16)
        vb = v_ref[sl, :]
        scores = lax.dot_general(qb, kb, (((1,), (1,)), ((), ())),
                                 preferred_element_type=F32) * intra_ref[...]
        inner = jnp.dot(scores.astype(BF16), vb, preferred_element_type=F32)
        state = state_ref[...]
        cross = jnp.dot(qb, state.astype(BF16), preferred_element_type=F32) * qd_ref[...]
        kdec = (kr * kd_ref[...]).astype(BF16)
        state_ref[...] = state * cd_ref[...] + lax.dot_general(
            kdec, vb, (((0,), (0,)), ((), ())), preferred_element_type=F32)
        o = inner + cross
        y = _rms_rows(o, gn_ref[...])
        o_ref[sl, :] = (jax.nn.silu(g_ref[sl, :].astype(F32)) * y).astype(o_ref.dtype)


def _retention(qkvg, gn_gain, *, tc=512):
    b, s, _ = qkvg.shape
    hq, hv, c = RET_QK_DIM, RET_V_DIM, RET_CHUNK
    nh = RET_HEADS
    half = hq // 2
    inv = ROPE_BASE ** (-jnp.linspace(0.0, 1.0, half, dtype=F32))
    ang = jnp.arange(s).astype(F32)[:, None] * inv[None, :]
    cos, sin = jnp.cos(ang), jnp.sin(ang)
    log_gamma = jnp.log(1.0 - 2.0 ** (-5.0 - jnp.arange(nh, dtype=F32)))
    idx = jnp.arange(c, dtype=F32)
    diff = idx[:, None] - idx[None, :]
    intra = jnp.where(diff[None] >= 0,
                      jnp.exp(jnp.maximum(diff, 0.0)[None] * log_gamma[:, None, None]), 0.0)
    qd = jnp.exp((idx + 1.0)[None, :] * log_gamma[:, None])[:, :, None]
    kd = jnp.exp((c - 1.0 - idx)[None, :] * log_gamma[:, None])[:, :, None]
    cd = jnp.broadcast_to(jnp.exp(c * log_gamma)[:, None, None], (nh, 1, hv))

    vmem = (4 * _nbytes((tc, hq), BF16) + 6 * _nbytes((tc, hv), BF16)
            + 4 * _nbytes((tc, half), F32) + 2 * _nbytes((c, c), F32)
            + 4 * _nbytes((c, 128), F32) + 3 * _nbytes((hq, hv), F32)
            + 10 * _nbytes((c, hv), F32))
    return pl.pallas_call(
        _ret_body,
        grid=(b, nh, s // tc),
        in_specs=[
            pl.BlockSpec((None, tc, hq), lambda bi, h, ci: (bi, ci, h)),
            pl.BlockSpec((None, tc, hq), lambda bi, h, ci: (bi, ci, nh + h)),
            pl.BlockSpec((None, tc, hv), lambda bi, h, ci: (bi, ci, nh + h)),
            pl.BlockSpec((None, tc, hv), lambda bi, h, ci: (bi, ci, 2 * nh + h)),
            pl.BlockSpec((tc, half), lambda bi, h, ci: (ci, 0)),
            pl.BlockSpec((tc, half), lambda bi, h, ci: (ci, 0)),
            pl.BlockSpec((None, c, c), lambda bi, h, ci: (h, 0, 0)),
            pl.BlockSpec((None, c, 1), lambda bi, h, ci: (h, 0, 0)),
            pl.BlockSpec((None, c, 1), lambda bi, h, ci: (h, 0, 0)),
            pl.BlockSpec((None, 1, hv), lambda bi, h, ci: (h, 0, 0)),
            pl.BlockSpec((1, hv), lambda bi, h, ci: (0, h)),
        ],
        out_specs=pl.BlockSpec((None, tc, hv), lambda bi, h, ci: (bi, ci, h)),
        out_shape=jax.ShapeDtypeStruct((b, s, nh * hv), BF16),
        scratch_shapes=[pltpu.VMEM((hq, hv), F32)],
        compiler_params=_params(3, 2 * vmem),
    )(qkvg, qkvg, qkvg, qkvg, cos, sin, intra, qd, kd, cd, gn_gain.reshape(1, nh * hv))


def kernel(x, sb_norm, sb_w_qkv, sb_w_o, gm_norm, gm_w_in, gm_v_norm, gm_w_s, gm_b_s, gm_w_o,
           ret_norm, ret_w_qkvg, ret_gn, ret_w_o, ffn_norm, ffn_w_gate_up, ffn_w_down,
           final_norm):
    b, s, d = x.shape
    n = b * s
    depth = ffn_norm.shape[0]
    sb_w_qkv, sb_w_o, gm_w_in, gm_w_o, ret_w_qkvg, ret_w_o, ffn_w_gate_up, ffn_w_down = (
        w.astype(BF16) for w in (sb_w_qkv, sb_w_o, gm_w_in, gm_w_o, ret_w_qkvg, ret_w_o,
                                 ffn_w_gate_up, ffn_w_down))
    h = x.reshape(n, d)
    for i in range(depth):
        kind, j = i % N_MIXERS, i // N_MIXERS
        if kind == 0:
            q_scale = jnp.concatenate([
                jnp.full((d,), (SB_HEAD_DIM ** -0.5) * LOG2_E, F32), jnp.ones((2 * d,), F32)])
            qkv = _matmul(h, sb_w_qkv, j, gain=sb_norm[j], col_scale=q_scale, out_dtype=BF16)
            o = _sb_attention(qkv.reshape(b, s, 3 * d))
            h = _matmul(o.reshape(n, d), sb_w_o, j, residual=h, out_dtype=F32)
        elif kind == 1:
            uv = _matmul(h, gm_w_in, j, gain=gm_norm[j], act="gelu", out_dtype=BF16)
            b_full = jnp.repeat(gm_b_s[j].T, GM_GROUP_DIM, axis=1)
            gated = _gm_gate(uv, gm_v_norm[j], gm_w_s[j], b_full)
            h = _matmul(gated, gm_w_o, j, residual=h, out_dtype=F32)
        else:
            qkvg = _matmul(h, ret_w_qkvg, j, gain=ret_norm[j], out_dtype=BF16)
            o = _retention(qkvg.reshape(b, s, 6 * d), ret_gn[j])
            h = _matmul(o.reshape(n, 2 * d), ret_w_o, j, residual=h, out_dtype=F32, tn=512)
        h = _ffn(h, ffn_norm[i], ffn_w_gate_up, ffn_w_down, i,
                 final_gain=final_norm if i == depth - 1 else None)
    return h.reshape(b, s, d)
```

```python
import functools

import jax
import jax.numpy as jnp
from jax import lax
from jax.experimental import pallas as pl
from jax.experimental.pallas import tpu as pltpu

F32 = jnp.float32
BF16 = jnp.bfloat16

EPS = 1e-6
N_MIXERS = 3
SB_HEADS = 16
SB_HEAD_DIM = 128
SB_TILE = 256
SB_HEADS_PER_STEP = 4
LOG2_E = 1.4426950408889634
SB_UNDERFLOW_BITS = 160.0
GM_GROUPS = 16
GM_GROUP_DIM = 128
GM_CHUNK = 128
RET_HEADS = 8
RET_QK_DIM = 256
RET_V_DIM = 512
RET_CHUNK = 128
ROPE_BASE = 10000.0

V7X_VMEM_LIMIT_BYTES = 60000 * 1024
NORM_ROWS = 256


def _nbytes(shape, dtype):
    n = 1
    for s in shape:
        n *= s
    return n * jnp.dtype(dtype).itemsize


def _params(n_axes, vmem_bytes):
    return pltpu.CompilerParams(
        dimension_semantics=("arbitrary",) * n_axes,
        vmem_limit_bytes=int(min(vmem_bytes, V7X_VMEM_LIMIT_BYTES)),
    )


def _rms_rows(xf, gain):
    y = xf * lax.rsqrt(jnp.mean(xf * xf, axis=-1, keepdims=True) + EPS)
    return y * gain


def _norm_into(x_ref, g_ref, xn_ref):
    rows = x_ref.shape[0]
    step = min(NORM_ROWS, rows)

    def body(r, carry):
        sl = pl.ds(pl.multiple_of(r * step, step), step)
        xn_ref[sl, :] = _rms_rows(x_ref[sl, :], g_ref[...]).astype(xn_ref.dtype)
        return carry

    lax.fori_loop(0, rows // step, body, 0)


def _mm_body(*refs, norm, act, has_res, has_scale):
    it = iter(refs)
    x_ref = next(it)
    g_ref = next(it) if norm else None
    w_ref = next(it)
    s_ref = next(it) if has_scale else None
    r_ref = next(it) if has_res else None
    o_ref = next(it)
    xn_ref = next(it) if norm else None

    if norm:
        @pl.when(pl.program_id(1) == 0)
        def _():
            _norm_into(x_ref, g_ref, xn_ref)
        lhs = xn_ref[...]
    else:
        lhs = x_ref[...]
    acc = jnp.dot(lhs, w_ref[...], preferred_element_type=F32)
    if act == "gelu":
        acc = jax.nn.gelu(acc)
    if has_scale:
        acc = acc * s_ref[...]
    if has_res:
        acc = acc + r_ref[...]
    o_ref[...] = acc.astype(o_ref.dtype)


def _column_blocks(w, tn):
    k, n = w.shape
    return w.astype(BF16).reshape(k, n // tn, tn).transpose(1, 0, 2)


def _matmul(x, w, *, gain=None, residual=None, act=None, col_scale=None,
            out_dtype, tm=1024, tn=1024):
    m, k = x.shape
    n = w.shape[-1]
    norm = gain is not None
    has_res = residual is not None
    has_scale = col_scale is not None
    in_specs = [pl.BlockSpec((tm, k), lambda i, j: (i, 0))]
    args = [x]
    if norm:
        in_specs.append(pl.BlockSpec((1, k), lambda i, j: (0, 0)))
        args.append(gain.reshape(1, k))
    in_specs.append(pl.BlockSpec((None, k, tn), lambda i, j: (j, 0, 0)))
    args.append(_column_blocks(w, tn))
    if has_scale:
        in_specs.append(pl.BlockSpec((1, tn), lambda i, j: (0, j)))
        args.append(col_scale.reshape(1, n))
    if has_res:
        in_specs.append(pl.BlockSpec((tm, tn), lambda i, j: (i, j)))
        args.append(residual)
    scratch = [pltpu.VMEM((tm, k), BF16)] if norm else []
    vmem = (2 * _nbytes((tm, k), x.dtype) + 2 * _nbytes((k, tn), BF16)
            + 2 * _nbytes((tm, tn), out_dtype) + 2 * _nbytes((tm, tn), F32)
            + (2 * _nbytes((tm, tn), F32) if has_res else 0)
            + (_nbytes((tm, k), BF16) + 4 * _nbytes((NORM_ROWS, k), F32) if norm else 0))
    return pl.pallas_call(
        functools.partial(_mm_body, norm=norm, act=act, has_res=has_res,
                          has_scale=has_scale),
        grid=(m // tm, n // tn),
        in_specs=in_specs,
        out_specs=pl.BlockSpec((tm, tn), lambda i, j: (i, j)),
        out_shape=jax.ShapeDtypeStruct((m, n), out_dtype),
        scratch_shapes=scratch,
        compiler_params=_params(2, vmem),
    )(*args)


def _ffn_body(*refs, final):
    if final:
        h_ref, g_ref, wg_ref, wu_ref, wd_ref, fg_ref, o_ref, xn_ref = refs
    else:
        h_ref, g_ref, wg_ref, wu_ref, wd_ref, o_ref, xn_ref = refs
        fg_ref = None
    f = pl.program_id(1)

    @pl.when(f == 0)
    def _():
        _norm_into(h_ref, g_ref, xn_ref)
        o_ref[...] = h_ref[...]

    xn = xn_ref[...]
    gate = jnp.dot(xn, wg_ref[...], preferred_element_type=F32)
    up = jnp.dot(xn, wu_ref[...], preferred_element_type=F32)
    act = (jax.nn.silu(gate) * up).astype(BF16)
    o_ref[...] += jnp.dot(act, wd_ref[...], preferred_element_type=F32)

    if final:
        @pl.when(f == pl.num_programs(1) - 1)
        def _():
            rows = o_ref.shape[0]
            step = min(NORM_ROWS, rows)

            def body(r, carry):
                sl = pl.ds(pl.multiple_of(r * step, step), step)
                o_ref[sl, :] = _rms_rows(o_ref[sl, :], fg_ref[...])
                return carry

            lax.fori_loop(0, rows // step, body, 0)


def _ffn(h, gain, w_gate_up, w_down, *, final_gain=None, tm=512, tf=512):
    m, d = h.shape
    ff = w_down.shape[0]
    nf = ff // tf
    final = final_gain is not None
    in_specs = [
        pl.BlockSpec((tm, d), lambda i, f: (i, 0)),
        pl.BlockSpec((1, d), lambda i, f: (0, 0)),
        pl.BlockSpec((None, d, tf), lambda i, f: (f, 0, 0)),
        pl.BlockSpec((None, d, tf), lambda i, f: (nf + f, 0, 0)),
        pl.BlockSpec((tf, d), lambda i, f: (f, 0)),
    ]
    w_gu = _column_blocks(w_gate_up, tf)
    args = [h, gain.reshape(1, d), w_gu, w_gu, w_down.astype(BF16)]
    if final:
        in_specs.append(pl.BlockSpec((1, d), lambda i, f: (0, 0)))
        args.append(final_gain.reshape(1, d))
    vmem = (4 * _nbytes((tm, d), F32) + _nbytes((tm, d), BF16)
            + 6 * _nbytes((d, tf), BF16) + 3 * _nbytes((tm, tf), F32)
            + 2 * _nbytes((tm, d), F32) + 4 * _nbytes((NORM_ROWS, d), F32))
    return pl.pallas_call(
        functools.partial(_ffn_body, final=final),
        grid=(m // tm, nf),
        in_specs=in_specs,
        out_specs=pl.BlockSpec((tm, d), lambda i, f: (i, 0)),
        out_shape=jax.ShapeDtypeStruct((m, d), F32),
        scratch_shapes=[pltpu.VMEM((tm, d), BF16)],
        compiler_params=_params(2, vmem),
    )(*args)


def _sb_body(q_ref, k_ref, v_ref, o_ref):
    t = SB_TILE
    hd = SB_HEAD_DIM
    i = pl.program_id(2)
    heads = range(SB_HEADS_PER_STEP)
    hs = [slice(hh * hd, (hh + 1) * hd) for hh in heads]
    sign_bit = jnp.int32(-2 ** 31)
    row = lax.broadcasted_iota(jnp.int32, (t, t), 0)
    col = lax.broadcasted_iota(jnp.int32, (t, t), 1)
    after = (row > col).astype(BF16)
    causal = col < row

    def keys(kb):
        return pl.ds(pl.multiple_of(kb * t, t), t)

    def score(kb):
        sl = keys(kb)
        return [lax.dot_general(q_ref[:, hs[hh]], k_ref[sl, hs[hh]], (((1,), (1,)), ((), ())),
                                preferred_element_type=F32) for hh in heads]

    def keep_stage(w, masked):
        neg_abs = lax.bitcast_convert_type(
            lax.bitcast_convert_type(w, jnp.int32) | sign_bit, F32)
        keep = jnp.maximum(w, 0.0) + jnp.log2(1.0 + jnp.exp2(neg_abs))
        if masked:
            keep = jnp.where(causal, keep, 0.0)
        tail = jnp.dot(keep.astype(BF16), after, preferred_element_type=F32)
        return w - keep, tail, tail[:, :1] + keep[:, :1]

    def weigh(kb, hh, own, tail, run, acc, masked):
        a = jnp.exp2(own - tail - run)
        if masked:
            a = jnp.where(causal, a, 0.0)
        return acc + jnp.dot(a.astype(BF16), v_ref[keys(kb), hs[hh]],
                             preferred_element_type=F32)

    def tile(kb, runs, accs, masked):
        ws = score(kb)
        stages = [keep_stage(ws[hh], masked) for hh in heads]
        accs = tuple(weigh(kb, hh, stages[hh][0], stages[hh][1], runs[hh], accs[hh], masked)
                     for hh in heads)
        runs = tuple(runs[hh] + stages[hh][2] for hh in heads)
        return runs, accs

    def least(runs):
        m = runs[0]
        for r in runs[1:]:
            m = jnp.minimum(m, r)
        return jnp.min(m)

    runs, accs = tile(i, (jnp.zeros((t, 1), F32),) * len(heads),
                      (jnp.zeros((t, hd), F32),) * len(heads), True)

    def live(carry):
        kb, low, _, _ = carry
        return jnp.logical_and(kb >= 0, low < SB_UNDERFLOW_BITS)

    def step(carry):
        kb, _, runs, accs = carry
        runs, accs = tile(kb, runs, accs, False)
        return kb - 1, least(runs), runs, accs

    _, _, _, accs = lax.while_loop(live, step, (i - 1, least(runs), runs, accs))
    for hh in heads:
        o_ref[:, hs[hh]] = accs[hh].astype(o_ref.dtype)


def _sb_attention(qkv):
    b, s, _ = qkv.shape
    t = SB_TILE
    wd = SB_HEAD_DIM * SB_HEADS_PER_STEP
    groups = SB_HEADS // SB_HEADS_PER_STEP
    vmem = (4 * _nbytes((s, wd), BF16) + 4 * _nbytes((t, wd), BF16)
            + 12 * SB_HEADS_PER_STEP * _nbytes((t, t), F32))
    return pl.pallas_call(
        _sb_body,
        grid=(b, groups, s // t),
        in_specs=[
            pl.BlockSpec((None, t, wd), lambda bi, h, i: (bi, i, h)),
            pl.BlockSpec((None, s, wd), lambda bi, h, i: (bi, 0, groups + h)),
            pl.BlockSpec((None, s, wd), lambda bi, h, i: (bi, 0, 2 * groups + h)),
        ],
        out_specs=pl.BlockSpec((None, t, wd), lambda bi, h, i: (bi, i, h)),
        out_shape=jax.ShapeDtypeStruct((b, s, SB_HEADS * SB_HEAD_DIM), BF16),
        compiler_params=_params(3, 2 * vmem),
    )(qkv, qkv, qkv)


def _gm_body(u_ref, v_ref, vg_ref, ws_ref, bs_ref, o_ref, vn_ref):
    rows = u_ref.shape[0]
    c = GM_CHUNK
    row = lax.broadcasted_iota(jnp.int32, (c, c), 0)
    col = lax.broadcasted_iota(jnp.int32, (c, c), 1)
    lower = row >= col
    for ci in range(rows // c):
        sl = slice(ci * c, (ci + 1) * c)
        vf = v_ref[sl, :].astype(F32)
        mu = jnp.mean(vf, axis=-1, keepdims=True)
        xc = vf - mu
        y = xc * lax.rsqrt(jnp.mean(xc * xc, axis=-1, keepdims=True) + EPS)
        vn_ref[...] = (y * vg_ref[...]).astype(BF16)
        for g in range(GM_GROUPS):
            gs = slice(g * GM_GROUP_DIM, (g + 1) * GM_GROUP_DIM)
            ws = jnp.where(lower, ws_ref[g], 0.0).astype(BF16)
            mixed = jnp.dot(ws, vn_ref[:, gs], preferred_element_type=F32) + bs_ref[:, gs]
            o_ref[sl, gs] = (u_ref[sl, gs].astype(F32) * mixed).astype(o_ref.dtype)


def _gm_gate(uv, v_gain, w_s, b_full, *, tm=512):
    m, two_w = uv.shape
    w = two_w // 2
    c = GM_CHUNK
    vmem = (6 * _nbytes((tm, w), BF16) + 2 * _nbytes((GM_GROUPS, c, c), F32)
            + 2 * _nbytes((c, w), F32) + 8 * _nbytes((c, w), F32))
    return pl.pallas_call(
        _gm_body,
        grid=(m // tm,),
        in_specs=[
            pl.BlockSpec((tm, w), lambda i: (i, 0)),
            pl.BlockSpec((tm, w), lambda i: (i, 1)),
            pl.BlockSpec((1, w), lambda i: (0, 0)),
            pl.BlockSpec((GM_GROUPS, c, c), lambda i: (0, 0, 0)),
            pl.BlockSpec((c, w), lambda i: (0, 0)),
        ],
        out_specs=pl.BlockSpec((tm, w), lambda i: (i, 0)),
        out_shape=jax.ShapeDtypeStruct((m, w), BF16),
        scratch_shapes=[pltpu.VMEM((c, w), BF16)],
        compiler_params=_params(1, 2 * vmem),
    )(uv, uv, v_gain.reshape(1, w), w_s, b_full)


def _ret_body(q_ref, k_ref, v_ref, g_ref, cos_ref, sin_ref, intra_ref, qd_ref, kd_ref,
              cd_ref, gn_ref, o_ref, state_ref):
    c = RET_CHUNK
    half = RET_QK_DIM // 2

    @pl.when(pl.program_id(2) == 0)
    def _():
        state_ref[...] = jnp.zeros_like(state_ref)

    def rotate(x, cos, sin):
        x1, x2 = x[:, :half], x[:, half:]
        return jnp.concatenate([x1 * cos - x2 * sin, x1 * sin + x2 * cos], axis=1)

    for ci in range(q_ref.shape[0] // c):
        sl = slice(ci * c, (ci + 1) * c)
        cos = cos_ref[sl, :]
        sin = sin_ref[sl, :]
        qr = rotate(q_ref[sl, :].astype(F32), cos, sin)
        kr = rotate(k_ref[sl, :].astype(F32), cos, sin) * (RET_QK_DIM ** -0.5)
        qb = qr.astype(BF16)
        kb = kr.astype(BF16)
        vb = v_ref[sl, :]
        scores = lax.dot_general(qb, kb, (((1,), (1,)), ((), ())),
                                 preferred_element_type=F32) * intra_ref[...]
        inner = jnp.dot(scores.astype(BF16), vb, preferred_element_type=F32)
        state = state_ref[...]
        cross = jnp.dot(qb, state.astype(BF16), preferred_element_type=F32) * qd_ref[...]
        kdec = (kr * kd_ref[...]).astype(BF16)
        state_ref[...] = state * cd_ref[...] + lax.dot_general(
            kdec, vb, (((0,), (0,)), ((), ())), preferred_element_type=F32)
        o = inner + cross
        y = _rms_rows(o, gn_ref[...])
        o_ref[sl, :] = (jax.nn.silu(g_ref[sl, :].astype(F32)) * y).astype(o_ref.dtype)


def _retention(qkvg, gn_gain, *, tc=512):
    b, s, _ = qkvg.shape
    hq, hv, c = RET_QK_DIM, RET_V_DIM, RET_CHUNK
    nh = RET_HEADS
    half = hq // 2
    inv = ROPE_BASE ** (-jnp.linspace(0.0, 1.0, half, dtype=F32))
    ang = jnp.arange(s).astype(F32)[:, None] * inv[None, :]
    cos, sin = jnp.cos(ang), jnp.sin(ang)
    log_gamma = jnp.log(1.0 - 2.0 ** (-5.0 - jnp.arange(nh, dtype=F32)))
    idx = jnp.arange(c, dtype=F32)
    diff = idx[:, None] - idx[None, :]
    intra = jnp.where(diff[None] >= 0,
                      jnp.exp(jnp.maximum(diff, 0.0)[None] * log_gamma[:, None, None]), 0.0)
    qd = jnp.exp((idx + 1.0)[None, :] * log_gamma[:, None])[:, :, None]
    kd = jnp.exp((c - 1.0 - idx)[None, :] * log_gamma[:, None])[:, :, None]
    cd = jnp.broadcast_to(jnp.exp(c * log_gamma)[:, None, None], (nh, 1, hv))

    vmem = (4 * _nbytes((tc, hq), BF16) + 6 * _nbytes((tc, hv), BF16)
            + 4 * _nbytes((tc, half), F32) + 2 * _nbytes((c, c), F32)
            + 4 * _nbytes((c, 128), F32) + 3 * _nbytes((hq, hv), F32)
            + 10 * _nbytes((c, hv), F32))
    return pl.pallas_call(
        _ret_body,
        grid=(b, nh, s // tc),
        in_specs=[
            pl.BlockSpec((None, tc, hq), lambda bi, h, ci: (bi, ci, h)),
            pl.BlockSpec((None, tc, hq), lambda bi, h, ci: (bi, ci, nh + h)),
            pl.BlockSpec((None, tc, hv), lambda bi, h, ci: (bi, ci, nh + h)),
            pl.BlockSpec((None, tc, hv), lambda bi, h, ci: (bi, ci, 2 * nh + h)),
            pl.BlockSpec((tc, half), lambda bi, h, ci: (ci, 0)),
            pl.BlockSpec((tc, half), lambda bi, h, ci: (ci, 0)),
            pl.BlockSpec((None, c, c), lambda bi, h, ci: (h, 0, 0)),
            pl.BlockSpec((None, c, 1), lambda bi, h, ci: (h, 0, 0)),
            pl.BlockSpec((None, c, 1), lambda bi, h, ci: (h, 0, 0)),
            pl.BlockSpec((None, 1, hv), lambda bi, h, ci: (h, 0, 0)),
            pl.BlockSpec((1, hv), lambda bi, h, ci: (0, h)),
        ],
        out_specs=pl.BlockSpec((None, tc, hv), lambda bi, h, ci: (bi, ci, h)),
        out_shape=jax.ShapeDtypeStruct((b, s, nh * hv), BF16),
        scratch_shapes=[pltpu.VMEM((hq, hv), F32)],
        compiler_params=_params(3, 2 * vmem),
    )(qkvg, qkvg, qkvg, qkvg, cos, sin, intra, qd, kd, cd, gn_gain.reshape(1, nh * hv))


def kernel(x, sb_norm, sb_w_qkv, sb_w_o, gm_norm, gm_w_in, gm_v_norm, gm_w_s, gm_b_s, gm_w_o,
           ret_norm, ret_w_qkvg, ret_gn, ret_w_o, ffn_norm, ffn_w_gate_up, ffn_w_down,
           final_norm):
    b, s, d = x.shape
    n = b * s
    depth = ffn_norm.shape[0]
    h = x.reshape(n, d)
    for i in range(depth):
        kind, j = i % N_MIXERS, i // N_MIXERS
        if kind == 0:
            q_scale = jnp.concatenate([
                jnp.full((d,), (SB_HEAD_DIM ** -0.5) * LOG2_E, F32), jnp.ones((2 * d,), F32)])
            qkv = _matmul(h, sb_w_qkv[j], gain=sb_norm[j], col_scale=q_scale, out_dtype=BF16)
            o = _sb_attention(qkv.reshape(b, s, 3 * d))
            h = _matmul(o.reshape(n, d), sb_w_o[j], residual=h, out_dtype=F32)
        elif kind == 1:
            uv = _matmul(h, gm_w_in[j], gain=gm_norm[j], act="gelu", out_dtype=BF16)
            b_full = jnp.repeat(gm_b_s[j].T, GM_GROUP_DIM, axis=1)
            gated = _gm_gate(uv, gm_v_norm[j], gm_w_s[j], b_full)
            h = _matmul(gated, gm_w_o[j], residual=h, out_dtype=F32)
        else:
            qkvg = _matmul(h, ret_w_qkvg[j], gain=ret_norm[j], out_dtype=BF16)
            o = _retention(qkvg.reshape(b, s, 6 * d), ret_gn[j])
            h = _matmul(o.reshape(n, 2 * d), ret_w_o[j], residual=h, out_dtype=F32, tn=512)
        h = _ffn(h, ffn_norm[i], ffn_w_gate_up[i], ffn_w_down[i],
                 final_gain=final_norm if i == depth - 1 else None)
    return h.reshape(b, s, d)
```

```python
import functools

import jax
import jax.numpy as jnp
from jax import lax
from jax.experimental import pallas as pl
from jax.experimental.pallas import tpu as pltpu

F32 = jnp.float32
BF16 = jnp.bfloat16

EPS = 1e-6
N_MIXERS = 3
SB_HEADS = 16
SB_HEAD_DIM = 128
SB_TILE = 256
SB_HEADS_PER_STEP = 4
LOG2_E = 1.4426950408889634
SB_UNDERFLOW_BITS = 160.0
GM_GROUPS = 16
GM_GROUP_DIM = 128
GM_CHUNK = 128
RET_HEADS = 8
RET_QK_DIM = 256
RET_V_DIM = 512
RET_CHUNK = 128
ROPE_BASE = 10000.0

V7X_VMEM_LIMIT_BYTES = 60000 * 1024
NORM_ROWS = 256


def _nbytes(shape, dtype):
    n = 1
    for s in shape:
        n *= s
    return n * jnp.dtype(dtype).itemsize


def _params(n_axes, vmem_bytes):
    return pltpu.CompilerParams(
        dimension_semantics=("arbitrary",) * n_axes,
        vmem_limit_bytes=int(min(vmem_bytes, V7X_VMEM_LIMIT_BYTES)),
    )


def _rms_rows(xf, gain):
    y = xf * lax.rsqrt(jnp.mean(xf * xf, axis=-1, keepdims=True) + EPS)
    return y * gain


def _norm_into(x_ref, g_ref, xn_ref):
    rows = x_ref.shape[0]
    step = min(NORM_ROWS, rows)

    def body(r, carry):
        sl = pl.ds(pl.multiple_of(r * step, step), step)
        xn_ref[sl, :] = _rms_rows(x_ref[sl, :], g_ref[...]).astype(xn_ref.dtype)
        return carry

    lax.fori_loop(0, rows // step, body, 0)


def _mm_body(*refs, norm, act, has_res, has_scale):
    it = iter(refs)
    x_ref = next(it)
    g_ref = next(it) if norm else None
    w_ref = next(it)
    wu_ref = next(it) if act == "swiglu" else None
    s_ref = next(it) if has_scale else None
    r_ref = next(it) if has_res else None
    o_ref = next(it)
    xn_ref = next(it) if norm else None

    if norm:
        @pl.when(pl.program_id(1) == 0)
        def _():
            _norm_into(x_ref, g_ref, xn_ref)
        lhs = xn_ref[...]
    else:
        lhs = x_ref[...]
    acc = jnp.dot(lhs, w_ref[...], preferred_element_type=F32)
    if act == "gelu":
        acc = jax.nn.gelu(acc)
    if act == "swiglu":
        acc = jax.nn.silu(acc) * jnp.dot(lhs, wu_ref[...], preferred_element_type=F32)
    if has_scale:
        acc = acc * s_ref[...]
    if has_res:
        acc = acc + r_ref[...]
    o_ref[...] = acc.astype(o_ref.dtype)


def _matmul(x, w_stack, layer, *, gain=None, residual=None, act=None, col_scale=None,
            out_dtype, tm=1024, tn=1024):
    m, k = x.shape
    n = w_stack.shape[-1]
    if act == "swiglu":
        n //= 2
    norm = gain is not None
    has_res = residual is not None
    has_scale = col_scale is not None
    in_specs = [pl.BlockSpec((tm, k), lambda i, j: (i, 0))]
    args = [x]
    if norm:
        in_specs.append(pl.BlockSpec((1, k), lambda i, j: (0, 0)))
        args.append(gain.reshape(1, k))
    in_specs.append(pl.BlockSpec((None, k, tn), lambda i, j: (layer, 0, j)))
    args.append(w_stack)
    if act == "swiglu":
        in_specs.append(pl.BlockSpec((None, k, tn), lambda i, j: (layer, 0, n // tn + j)))
        args.append(w_stack)
    if has_scale:
        in_specs.append(pl.BlockSpec((1, tn), lambda i, j: (0, j)))
        args.append(col_scale.reshape(1, n))
    if has_res:
        in_specs.append(pl.BlockSpec((tm, tn), lambda i, j: (i, j)))
        args.append(residual)
    scratch = [pltpu.VMEM((tm, k), BF16)] if norm else []
    n_w = 2 if act == "swiglu" else 1
    vmem = (2 * _nbytes((tm, k), x.dtype) + 2 * n_w * _nbytes((k, tn), BF16)
            + 2 * _nbytes((tm, tn), out_dtype) + 2 * n_w * _nbytes((tm, tn), F32)
            + (2 * _nbytes((tm, tn), F32) if has_res else 0)
            + (_nbytes((tm, k), BF16) + 4 * _nbytes((NORM_ROWS, k), F32) if norm else 0))
    return pl.pallas_call(
        functools.partial(_mm_body, norm=norm, act=act, has_res=has_res,
                          has_scale=has_scale),
        grid=(m // tm, n // tn),
        in_specs=in_specs,
        out_specs=pl.BlockSpec((tm, tn), lambda i, j: (i, j)),
        out_shape=jax.ShapeDtypeStruct((m, n), out_dtype),
        scratch_shapes=scratch,
        compiler_params=_params(2, vmem),
    )(*args)


def _ffn_body(*refs, final):
    if final:
        h_ref, g_ref, wg_ref, wu_ref, wd_ref, fg_ref, o_ref, xn_ref = refs
    else:
        h_ref, g_ref, wg_ref, wu_ref, wd_ref, o_ref, xn_ref = refs
        fg_ref = None
    f = pl.program_id(1)

    @pl.when(f == 0)
    def _():
        _norm_into(h_ref, g_ref, xn_ref)
        o_ref[...] = h_ref[...]

    xn = xn_ref[...]
    gate = jnp.dot(xn, wg_ref[...], preferred_element_type=F32)
    up = jnp.dot(xn, wu_ref[...], preferred_element_type=F32)
    act = (jax.nn.silu(gate) * up).astype(BF16)
    o_ref[...] += jnp.dot(act, wd_ref[...], preferred_element_type=F32)

    if final:
        @pl.when(f == pl.num_programs(1) - 1)
        def _():
            rows = o_ref.shape[0]
            step = min(NORM_ROWS, rows)

            def body(r, carry):
                sl = pl.ds(pl.multiple_of(r * step, step), step)
                o_ref[sl, :] = _rms_rows(o_ref[sl, :], fg_ref[...])
                return carry

            lax.fori_loop(0, rows // step, body, 0)


def _ffn(h, gain, w_gu_stack, w_d_stack, layer, *, final_gain=None, tm=512, tf=512):
    m, d = h.shape
    ff = w_d_stack.shape[1]
    nf = ff // tf
    final = final_gain is not None
    in_specs = [
        pl.BlockSpec((tm, d), lambda i, f: (i, 0)),
        pl.BlockSpec((1, d), lambda i, f: (0, 0)),
        pl.BlockSpec((None, d, tf), lambda i, f: (layer, 0, f)),
        pl.BlockSpec((None, d, tf), lambda i, f: (layer, 0, nf + f)),
        pl.BlockSpec((None, tf, d), lambda i, f: (layer, f, 0)),
    ]
    args = [h, gain.reshape(1, d), w_gu_stack, w_gu_stack, w_d_stack]
    if final:
        in_specs.append(pl.BlockSpec((1, d), lambda i, f: (0, 0)))
        args.append(final_gain.reshape(1, d))
    vmem = (4 * _nbytes((tm, d), F32) + _nbytes((tm, d), BF16)
            + 6 * _nbytes((d, tf), BF16) + 3 * _nbytes((tm, tf), F32)
            + 2 * _nbytes((tm, d), F32) + 4 * _nbytes((NORM_ROWS, d), F32))
    return pl.pallas_call(
        functools.partial(_ffn_body, final=final),
        grid=(m // tm, nf),
        in_specs=in_specs,
        out_specs=pl.BlockSpec((tm, d), lambda i, f: (i, 0)),
        out_shape=jax.ShapeDtypeStruct((m, d), F32),
        scratch_shapes=[pltpu.VMEM((tm, d), BF16)],
        compiler_params=_params(2, vmem),
    )(*args)


def _sb_body(q_ref, k_ref, v_ref, o_ref):
    t = SB_TILE
    hd = SB_HEAD_DIM
    i = pl.program_id(2)
    heads = range(SB_HEADS_PER_STEP)
    hs = [slice(hh * hd, (hh + 1) * hd) for hh in heads]
    sign_bit = jnp.int32(-2 ** 31)
    row = lax.broadcasted_iota(jnp.int32, (t, t), 0)
    col = lax.broadcasted_iota(jnp.int32, (t, t), 1)
    after = (row > col).astype(BF16)
    causal = col < row

    def keys(kb):
        return pl.ds(pl.multiple_of(kb * t, t), t)

    def score(kb):
        sl = keys(kb)
        return [lax.dot_general(q_ref[:, hs[hh]], k_ref[sl, hs[hh]], (((1,), (1,)), ((), ())),
                                preferred_element_type=F32) for hh in heads]

    def keep_stage(w, masked):
        neg_abs = lax.bitcast_convert_type(
            lax.bitcast_convert_type(w, jnp.int32) | sign_bit, F32)
        keep = jnp.maximum(w, 0.0) + jnp.log2(1.0 + jnp.exp2(neg_abs))
        if masked:
            keep = jnp.where(causal, keep, 0.0)
        tail = jnp.dot(keep.astype(BF16), after, preferred_element_type=F32)
        return w - keep, tail, tail[:, :1] + keep[:, :1]

    def weigh(kb, hh, own, tail, run, acc, masked):
        a = jnp.exp2(own - tail - run)
        if masked:
            a = jnp.where(causal, a, 0.0)
        return acc + jnp.dot(a.astype(BF16), v_ref[keys(kb), hs[hh]],
                             preferred_element_type=F32)

    def tile(kb, runs, accs, masked):
        ws = score(kb)
        stages = [keep_stage(ws[hh], masked) for hh in heads]
        accs = tuple(weigh(kb, hh, stages[hh][0], stages[hh][1], runs[hh], accs[hh], masked)
                     for hh in heads)
        runs = tuple(runs[hh] + stages[hh][2] for hh in heads)
        return runs, accs

    def least(runs):
        m = runs[0]
        for r in runs[1:]:
            m = jnp.minimum(m, r)
        return jnp.min(m)

    runs, accs = tile(i, (jnp.zeros((t, 1), F32),) * len(heads),
                      (jnp.zeros((t, hd), F32),) * len(heads), True)

    def live(carry):
        kb, low, _, _ = carry
        return jnp.logical_and(kb >= 0, low < SB_UNDERFLOW_BITS)

    def step(carry):
        kb, _, runs, accs = carry
        runs, accs = tile(kb, runs, accs, False)
        return kb - 1, least(runs), runs, accs

    _, _, _, accs = lax.while_loop(live, step, (i - 1, least(runs), runs, accs))
    for hh in heads:
        o_ref[:, hs[hh]] = accs[hh].astype(o_ref.dtype)


def _sb_attention(qkv):
    b, s, _ = qkv.shape
    t = SB_TILE
    wd = SB_HEAD_DIM * SB_HEADS_PER_STEP
    groups = SB_HEADS // SB_HEADS_PER_STEP
    vmem = (4 * _nbytes((s, wd), BF16) + 4 * _nbytes((t, wd), BF16)
            + 12 * SB_HEADS_PER_STEP * _nbytes((t, t), F32))
    return pl.pallas_call(
        _sb_body,
        grid=(b, groups, s // t),
        in_specs=[
            pl.BlockSpec((None, t, wd), lambda bi, h, i: (bi, i, h)),
            pl.BlockSpec((None, s, wd), lambda bi, h, i: (bi, 0, groups + h)),
            pl.BlockSpec((None, s, wd), lambda bi, h, i: (bi, 0, 2 * groups + h)),
        ],
        out_specs=pl.BlockSpec((None, t, wd), lambda bi, h, i: (bi, i, h)),
        out_shape=jax.ShapeDtypeStruct((b, s, SB_HEADS * SB_HEAD_DIM), BF16),
        compiler_params=_params(3, 2 * vmem),
    )(qkv, qkv, qkv)


def _gm_body(u_ref, v_ref, vg_ref, ws_ref, bs_ref, o_ref, vn_ref):
    rows = u_ref.shape[0]
    c = GM_CHUNK
    row = lax.broadcasted_iota(jnp.int32, (c, c), 0)
    col = lax.broadcasted_iota(jnp.int32, (c, c), 1)
    lower = row >= col
    for ci in range(rows // c):
        sl = slice(ci * c, (ci + 1) * c)
        vf = v_ref[sl, :].astype(F32)
        mu = jnp.mean(vf, axis=-1, keepdims=True)
        xc = vf - mu
        y = xc * lax.rsqrt(jnp.mean(xc * xc, axis=-1, keepdims=True) + EPS)
        vn_ref[...] = (y * vg_ref[...]).astype(BF16)
        for g in range(GM_GROUPS):
            gs = slice(g * GM_GROUP_DIM, (g + 1) * GM_GROUP_DIM)
            ws = jnp.where(lower, ws_ref[g], 0.0).astype(BF16)
            mixed = jnp.dot(ws, vn_ref[:, gs], preferred_element_type=F32) + bs_ref[:, gs]
            o_ref[sl, gs] = (u_ref[sl, gs].astype(F32) * mixed).astype(o_ref.dtype)


def _gm_gate(uv, v_gain, w_s, b_full, *, tm=512):
    m, two_w = uv.shape
    w = two_w // 2
    c = GM_CHUNK
    vmem = (6 * _nbytes((tm, w), BF16) + 2 * _nbytes((GM_GROUPS, c, c), F32)
            + 2 * _nbytes((c, w), F32) + 8 * _nbytes((c, w), F32))
    return pl.pallas_call(
        _gm_body,
        grid=(m // tm,),
        in_specs=[
            pl.BlockSpec((tm, w), lambda i: (i, 0)),
            pl.BlockSpec((tm, w), lambda i: (i, 1)),
            pl.BlockSpec((1, w), lambda i: (0, 0)),
            pl.BlockSpec((GM_GROUPS, c, c), lambda i: (0, 0, 0)),
            pl.BlockSpec((c, w), lambda i: (0, 0)),
        ],
        out_specs=pl.BlockSpec((tm, w), lambda i: (i, 0)),
        out_shape=jax.ShapeDtypeStruct((m, w), BF16),
        scratch_shapes=[pltpu.VMEM((c, w), BF16)],
        compiler_params=_params(1, 2 * vmem),
    )(uv, uv, v_gain.reshape(1, w), w_s, b_full)


def _ret_body(q_ref, k_ref, v_ref, g_ref, cos_ref, sin_ref, intra_ref, qd_ref, kd_ref,
              cd_ref, gn_ref, o_ref, state_ref):
    c = RET_CHUNK
    half = RET_QK_DIM // 2

    @pl.when(pl.program_id(2) == 0)
    def _():
        state_ref[...] = jnp.zeros_like(state_ref)

    def rotate(x, cos, sin):
        x1, x2 = x[:, :half], x[:, half:]
        return jnp.concatenate([x1 * cos - x2 * sin, x1 * sin + x2 * cos], axis=1)

    for ci in range(q_ref.shape[0] // c):
        sl = slice(ci * c, (ci + 1) * c)
        cos = cos_ref[sl, :]
        sin = sin_ref[sl, :]
        qr = rotate(q_ref[sl, :].astype(F32), cos, sin)
        kr = rotate(k_ref[sl, :].astype(F32), cos, sin) * (RET_QK_DIM ** -0.5)
        qb = qr.astype(BF16)
        kb = kr.astype(BF16)
        vb = v_ref[sl, :]
        scores = lax.dot_general(qb, kb, (((1,), (1,)), ((), ())),
                                 preferred_element_type=F32) * intra_ref[...]
        inner = jnp.dot(scores.astype(BF16), vb, preferred_element_type=F32)
        state = state_ref[...]
        cross = jnp.dot(qb, state.astype(BF16), preferred_element_type=F32) * qd_ref[...]
        kdec = (kr * kd_ref[...]).astype(BF16)
        state_ref[...] = state * cd_ref[...] + lax.dot_general(
            kdec, vb, (((0,), (0,)), ((), ())), preferred_element_type=F32)
        o = inner + cross
        y = _rms_rows(o, gn_ref[...])
        o_ref[sl, :] = (jax.nn.silu(g_ref[sl, :].astype(F32)) * y).astype(o_ref.dtype)


def _retention(qkvg, gn_gain, *, tc=512):
    b, s, _ = qkvg.shape
    hq, hv, c = RET_QK_DIM, RET_V_DIM, RET_CHUNK
    nh = RET_HEADS
    half = hq // 2
    inv = ROPE_BASE ** (-jnp.linspace(0.0, 1.0, half, dtype=F32))
    ang = jnp.arange(s).astype(F32)[:, None] * inv[None, :]
    cos, sin = jnp.cos(ang), jnp.sin(ang)
    log_gamma = jnp.log(1.0 - 2.0 ** (-5.0 - jnp.arange(nh, dtype=F32)))
    idx = jnp.arange(c, dtype=F32)
    diff = idx[:, None] - idx[None, :]
    intra = jnp.where(diff[None] >= 0,
                      jnp.exp(jnp.maximum(diff, 0.0)[None] * log_gamma[:, None, None]), 0.0)
    qd = jnp.exp((idx + 1.0)[None, :] * log_gamma[:, None])[:, :, None]
    kd = jnp.exp((c - 1.0 - idx)[None, :] * log_gamma[:, None])[:, :, None]
    cd = jnp.broadcast_to(jnp.exp(c * log_gamma)[:, None, None], (nh, 1, hv))

    vmem = (4 * _nbytes((tc, hq), BF16) + 6 * _nbytes((tc, hv), BF16)
            + 4 * _nbytes((tc, half), F32) + 2 * _nbytes((c, c), F32)
            + 4 * _nbytes((c, 128), F32) + 3 * _nbytes((hq, hv), F32)
            + 10 * _nbytes((c, hv), F32))
    return pl.pallas_call(
        _ret_body,
        grid=(b, nh, s // tc),
        in_specs=[
            pl.BlockSpec((None, tc, hq), lambda bi, h, ci: (bi, ci, h)),
            pl.BlockSpec((None, tc, hq), lambda bi, h, ci: (bi, ci, nh + h)),
            pl.BlockSpec((None, tc, hv), lambda bi, h, ci: (bi, ci, nh + h)),
            pl.BlockSpec((None, tc, hv), lambda bi, h, ci: (bi, ci, 2 * nh + h)),
            pl.BlockSpec((tc, half), lambda bi, h, ci: (ci, 0)),
            pl.BlockSpec((tc, half), lambda bi, h, ci: (ci, 0)),
            pl.BlockSpec((None, c, c), lambda bi, h, ci: (h, 0, 0)),
            pl.BlockSpec((None, c, 1), lambda bi, h, ci: (h, 0, 0)),
            pl.BlockSpec((None, c, 1), lambda bi, h, ci: (h, 0, 0)),
            pl.BlockSpec((None, 1, hv), lambda bi, h, ci: (h, 0, 0)),
            pl.BlockSpec((1, hv), lambda bi, h, ci: (0, h)),
        ],
        out_specs=pl.BlockSpec((None, tc, hv), lambda bi, h, ci: (bi, ci, h)),
        out_shape=jax.ShapeDtypeStruct((b, s, nh * hv), BF16),
        scratch_shapes=[pltpu.VMEM((hq, hv), F32)],
        compiler_params=_params(3, 2 * vmem),
    )(qkvg, qkvg, qkvg, qkvg, cos, sin, intra, qd, kd, cd, gn_gain.reshape(1, nh * hv))


def kernel(x, sb_norm, sb_w_qkv, sb_w_o, gm_norm, gm_w_in, gm_v_norm, gm_w_s, gm_b_s, gm_w_o,
           ret_norm, ret_w_qkvg, ret_gn, ret_w_o, ffn_norm, ffn_w_gate_up, ffn_w_down,
           final_norm):
    b, s, d = x.shape
    n = b * s
    depth = ffn_norm.shape[0]
    sb_w_qkv, sb_w_o, gm_w_in, gm_w_o, ret_w_qkvg, ret_w_o, ffn_w_gate_up, ffn_w_down = (
        w.astype(BF16) for w in (sb_w_qkv, sb_w_o, gm_w_in, gm_w_o, ret_w_qkvg, ret_w_o,
                                 ffn_w_gate_up, ffn_w_down))
    h = x.reshape(n, d)
    for i in range(depth):
        kind, j = i % N_MIXERS, i // N_MIXERS
        if kind == 0:
            q_scale = jnp.concatenate([
                jnp.full((d,), (SB_HEAD_DIM ** -0.5) * LOG2_E, F32), jnp.ones((2 * d,), F32)])
            qkv = _matmul(h, sb_w_qkv, j, gain=sb_norm[j], col_scale=q_scale, out_dtype=BF16)
            o = _sb_attention(qkv.reshape(b, s, 3 * d))
            h = _matmul(o.reshape(n, d), sb_w_o, j, residual=h, out_dtype=F32)
        elif kind == 1:
            uv = _matmul(h, gm_w_in, j, gain=gm_norm[j], act="gelu", out_dtype=BF16)
            b_full = jnp.repeat(gm_b_s[j].T, GM_GROUP_DIM, axis=1)
            gated = _gm_gate(uv, gm_v_norm[j], gm_w_s[j], b_full)
            h = _matmul(gated, gm_w_o, j, residual=h, out_dtype=F32)
        else:
            qkvg = _matmul(h, ret_w_qkvg, j, gain=ret_norm[j], out_dtype=BF16)
            o = _retention(qkvg.reshape(b, s, 6 * d), ret_gn[j])
            h = _matmul(o.reshape(n, 2 * d), ret_w_o, j, residual=h, out_dtype=F32, tn=512)
        if i == depth - 1:
            h = _ffn(h, ffn_norm[i], ffn_w_gate_up, ffn_w_down, i, final_gain=final_norm)
        else:
            hidden = _matmul(h, ffn_w_gate_up, i, gain=ffn_norm[i], act="swiglu",
                             out_dtype=BF16, tn=512)
            h = _matmul(hidden, ffn_w_down, i, residual=h, out_dtype=F32, tn=512)
    return h.reshape(b, s, d)
```

```python
import functools

import jax
import jax.numpy as jnp
from jax import lax
from jax.experimental import pallas as pl
from jax.experimental.pallas import tpu as pltpu

F32 = jnp.float32
BF16 = jnp.bfloat16

EPS = 1e-6
N_MIXERS = 3
SB_HEADS = 16
SB_HEAD_DIM = 128
SB_TILE = 256
SB_HEADS_PER_STEP = 4
LOG2_E = 1.4426950408889634
SB_UNDERFLOW_BITS = 160.0
GM_GROUPS = 16
GM_GROUP_DIM = 128
GM_CHUNK = 128
RET_HEADS = 8
RET_QK_DIM = 256
RET_V_DIM = 512
RET_CHUNK = 128
ROPE_BASE = 10000.0

V7X_VMEM_LIMIT_BYTES = 60000 * 1024
NORM_ROWS = 256


def _nbytes(shape, dtype):
    n = 1
    for s in shape:
        n *= s
    return n * jnp.dtype(dtype).itemsize


def _params(n_axes, vmem_bytes):
    return pltpu.CompilerParams(
        dimension_semantics=("arbitrary",) * n_axes,
        vmem_limit_bytes=int(min(vmem_bytes, V7X_VMEM_LIMIT_BYTES)),
    )


def _rms_rows(xf, gain):
    y = xf * lax.rsqrt(jnp.mean(xf * xf, axis=-1, keepdims=True) + EPS)
    return y * gain


def _norm_into(x_ref, g_ref, xn_ref):
    rows = x_ref.shape[0]
    step = min(NORM_ROWS, rows)

    def body(r, carry):
        sl = pl.ds(pl.multiple_of(r * step, step), step)
        xn_ref[sl, :] = _rms_rows(x_ref[sl, :], g_ref[...]).astype(xn_ref.dtype)
        return carry

    lax.fori_loop(0, rows // step, body, 0)


def _mm_body(*refs, norm, act, has_res, has_scale):
    it = iter(refs)
    x_ref = next(it)
    g_ref = next(it) if norm else None
    w_ref = next(it)
    wu_ref = next(it) if act == "swiglu" else None
    s_ref = next(it) if has_scale else None
    r_ref = next(it) if has_res else None
    o_ref = next(it)
    xn_ref = next(it) if norm else None

    if norm:
        @pl.when(pl.program_id(1) == 0)
        def _():
            _norm_into(x_ref, g_ref, xn_ref)
        lhs = xn_ref[...]
    else:
        lhs = x_ref[...]
    acc = jnp.dot(lhs, w_ref[...].astype(BF16), preferred_element_type=F32)
    if act == "gelu":
        acc = jax.nn.gelu(acc)
    if act == "swiglu":
        acc = jax.nn.silu(acc) * jnp.dot(lhs, wu_ref[...].astype(BF16),
                                         preferred_element_type=F32)
    if has_scale:
        acc = acc * s_ref[...]
    if has_res:
        acc = acc + r_ref[...]
    o_ref[...] = acc.astype(o_ref.dtype)


def _matmul(x, w_stack, layer, *, gain=None, residual=None, act=None, col_scale=None,
            out_dtype, tm=1024, tn=1024):
    m, k = x.shape
    n = w_stack.shape[-1]
    if act == "swiglu":
        n //= 2
    norm = gain is not None
    has_res = residual is not None
    has_scale = col_scale is not None
    in_specs = [pl.BlockSpec((tm, k), lambda i, j: (i, 0))]
    args = [x]
    if norm:
        in_specs.append(pl.BlockSpec((1, k), lambda i, j: (0, 0)))
        args.append(gain.reshape(1, k))
    in_specs.append(pl.BlockSpec((None, k, tn), lambda i, j: (layer, 0, j)))
    args.append(w_stack)
    if act == "swiglu":
        in_specs.append(pl.BlockSpec((None, k, tn), lambda i, j: (layer, 0, n // tn + j)))
        args.append(w_stack)
    if has_scale:
        in_specs.append(pl.BlockSpec((1, tn), lambda i, j: (0, j)))
        args.append(col_scale.reshape(1, n))
    if has_res:
        in_specs.append(pl.BlockSpec((tm, tn), lambda i, j: (i, j)))
        args.append(residual)
    scratch = [pltpu.VMEM((tm, k), BF16)] if norm else []
    n_w = 2 if act == "swiglu" else 1
    vmem = (2 * _nbytes((tm, k), x.dtype) + 2 * n_w * _nbytes((k, tn), w_stack.dtype)
            + (n_w * _nbytes((k, tn), BF16) if w_stack.dtype != BF16 else 0)
            + 2 * _nbytes((tm, tn), out_dtype) + 2 * n_w * _nbytes((tm, tn), F32)
            + (2 * _nbytes((tm, tn), F32) if has_res else 0)
            + (_nbytes((tm, k), BF16) + 4 * _nbytes((NORM_ROWS, k), F32) if norm else 0))
    return pl.pallas_call(
        functools.partial(_mm_body, norm=norm, act=act, has_res=has_res,
                          has_scale=has_scale),
        grid=(m // tm, n // tn),
        in_specs=in_specs,
        out_specs=pl.BlockSpec((tm, tn), lambda i, j: (i, j)),
        out_shape=jax.ShapeDtypeStruct((m, n), out_dtype),
        scratch_shapes=scratch,
        compiler_params=_params(2, vmem),
    )(*args)


def _ffn_body(*refs, final):
    if final:
        h_ref, g_ref, wg_ref, wu_ref, wd_ref, fg_ref, o_ref, xn_ref = refs
    else:
        h_ref, g_ref, wg_ref, wu_ref, wd_ref, o_ref, xn_ref = refs
        fg_ref = None
    f = pl.program_id(1)

    @pl.when(f == 0)
    def _():
        _norm_into(h_ref, g_ref, xn_ref)
        o_ref[...] = h_ref[...]

    xn = xn_ref[...]
    gate = jnp.dot(xn, wg_ref[...], preferred_element_type=F32)
    up = jnp.dot(xn, wu_ref[...], preferred_element_type=F32)
    act = (jax.nn.silu(gate) * up).astype(BF16)
    o_ref[...] += jnp.dot(act, wd_ref[...], preferred_element_type=F32)

    if final:
        @pl.when(f == pl.num_programs(1) - 1)
        def _():
            rows = o_ref.shape[0]
            step = min(NORM_ROWS, rows)

            def body(r, carry):
                sl = pl.ds(pl.multiple_of(r * step, step), step)
                o_ref[sl, :] = _rms_rows(o_ref[sl, :], fg_ref[...])
                return carry

            lax.fori_loop(0, rows // step, body, 0)


def _ffn(h, gain, w_gu_stack, w_d_stack, layer, *, final_gain=None, tm=512, tf=512):
    m, d = h.shape
    ff = w_d_stack.shape[1]
    nf = ff // tf
    final = final_gain is not None
    in_specs = [
        pl.BlockSpec((tm, d), lambda i, f: (i, 0)),
        pl.BlockSpec((1, d), lambda i, f: (0, 0)),
        pl.BlockSpec((None, d, tf), lambda i, f: (layer, 0, f)),
        pl.BlockSpec((None, d, tf), lambda i, f: (layer, 0, nf + f)),
        pl.BlockSpec((None, tf, d), lambda i, f: (layer, f, 0)),
    ]
    args = [h, gain.reshape(1, d), w_gu_stack, w_gu_stack, w_d_stack]
    if final:
        in_specs.append(pl.BlockSpec((1, d), lambda i, f: (0, 0)))
        args.append(final_gain.reshape(1, d))
    vmem = (4 * _nbytes((tm, d), F32) + _nbytes((tm, d), BF16)
            + 6 * _nbytes((d, tf), BF16) + 3 * _nbytes((tm, tf), F32)
            + 2 * _nbytes((tm, d), F32) + 4 * _nbytes((NORM_ROWS, d), F32))
    return pl.pallas_call(
        functools.partial(_ffn_body, final=final),
        grid=(m // tm, nf),
        in_specs=in_specs,
        out_specs=pl.BlockSpec((tm, d), lambda i, f: (i, 0)),
        out_shape=jax.ShapeDtypeStruct((m, d), F32),
        scratch_shapes=[pltpu.VMEM((tm, d), BF16)],
        compiler_params=_params(2, vmem),
    )(*args)


def _sb_body(q_ref, k_ref, v_ref, o_ref):
    t = SB_TILE
    hd = SB_HEAD_DIM
    i = pl.program_id(2)
    heads = range(SB_HEADS_PER_STEP)
    hs = [slice(hh * hd, (hh + 1) * hd) for hh in heads]
    sign_bit = jnp.int32(-2 ** 31)
    row = lax.broadcasted_iota(jnp.int32, (t, t), 0)
    col = lax.broadcasted_iota(jnp.int32, (t, t), 1)
    after = (row > col).astype(BF16)
    causal = col < row

    def keys(kb):
        return pl.ds(pl.multiple_of(kb * t, t), t)

    def score(kb):
        sl = keys(kb)
        return [lax.dot_general(q_ref[:, hs[hh]], k_ref[sl, hs[hh]], (((1,), (1,)), ((), ())),
                                preferred_element_type=F32) for hh in heads]

    def keep_stage(w, masked):
        neg_abs = lax.bitcast_convert_type(
            lax.bitcast_convert_type(w, jnp.int32) | sign_bit, F32)
        keep = jnp.maximum(w, 0.0) + jnp.log2(1.0 + jnp.exp2(neg_abs))
        if masked:
            keep = jnp.where(causal, keep, 0.0)
        tail = jnp.dot(keep.astype(BF16), after, preferred_element_type=F32)
        return w - keep, tail, tail[:, :1] + keep[:, :1]

    def weigh(kb, hh, own, tail, run, acc, masked):
        a = jnp.exp2(own - tail - run)
        if masked:
            a = jnp.where(causal, a, 0.0)
        return acc + jnp.dot(a.astype(BF16), v_ref[keys(kb), hs[hh]],
                             preferred_element_type=F32)

    def tile(kb, runs, accs, masked):
        ws = score(kb)
        stages = [keep_stage(ws[hh], masked) for hh in heads]
        accs = tuple(weigh(kb, hh, stages[hh][0], stages[hh][1], runs[hh], accs[hh], masked)
                     for hh in heads)
        runs = tuple(runs[hh] + stages[hh][2] for hh in heads)
        return runs, accs

    def least(runs):
        m = runs[0]
        for r in runs[1:]:
            m = jnp.minimum(m, r)
        return jnp.min(m)

    runs, accs = tile(i, (jnp.zeros((t, 1), F32),) * len(heads),
                      (jnp.zeros((t, hd), F32),) * len(heads), True)

    def live(carry):
        kb, low, _, _ = carry
        return jnp.logical_and(kb >= 0, low < SB_UNDERFLOW_BITS)

    def step(carry):
        kb, _, runs, accs = carry
        runs, accs = tile(kb, runs, accs, False)
        return kb - 1, least(runs), runs, accs

    _, _, _, accs = lax.while_loop(live, step, (i - 1, least(runs), runs, accs))
    for hh in heads:
        o_ref[:, hs[hh]] = accs[hh].astype(o_ref.dtype)


def _sb_attention(qkv):
    b, s, _ = qkv.shape
    t = SB_TILE
    wd = SB_HEAD_DIM * SB_HEADS_PER_STEP
    groups = SB_HEADS // SB_HEADS_PER_STEP
    vmem = (4 * _nbytes((s, wd), BF16) + 4 * _nbytes((t, wd), BF16)
            + 12 * SB_HEADS_PER_STEP * _nbytes((t, t), F32))
    return pl.pallas_call(
        _sb_body,
        grid=(b, groups, s // t),
        in_specs=[
            pl.BlockSpec((None, t, wd), lambda bi, h, i: (bi, i, h)),
            pl.BlockSpec((None, s, wd), lambda bi, h, i: (bi, 0, groups + h)),
            pl.BlockSpec((None, s, wd), lambda bi, h, i: (bi, 0, 2 * groups + h)),
        ],
        out_specs=pl.BlockSpec((None, t, wd), lambda bi, h, i: (bi, i, h)),
        out_shape=jax.ShapeDtypeStruct((b, s, SB_HEADS * SB_HEAD_DIM), BF16),
        compiler_params=_params(3, 2 * vmem),
    )(qkv, qkv, qkv)


def _gm_body(u_ref, v_ref, vg_ref, ws_ref, bs_ref, o_ref, vn_ref):
    rows = u_ref.shape[0]
    c = GM_CHUNK
    row = lax.broadcasted_iota(jnp.int32, (c, c), 0)
    col = lax.broadcasted_iota(jnp.int32, (c, c), 1)
    lower = row >= col
    for ci in range(rows // c):
        sl = slice(ci * c, (ci + 1) * c)
        vf = v_ref[sl, :].astype(F32)
        mu = jnp.mean(vf, axis=-1, keepdims=True)
        xc = vf - mu
        y = xc * lax.rsqrt(jnp.mean(xc * xc, axis=-1, keepdims=True) + EPS)
        vn_ref[...] = (y * vg_ref[...]).astype(BF16)
        for g in range(GM_GROUPS):
            gs = slice(g * GM_GROUP_DIM, (g + 1) * GM_GROUP_DIM)
            ws = jnp.where(lower, ws_ref[g], 0.0).astype(BF16)
            mixed = jnp.dot(ws, vn_ref[:, gs], preferred_element_type=F32) + bs_ref[:, gs]
            o_ref[sl, gs] = (u_ref[sl, gs].astype(F32) * mixed).astype(o_ref.dtype)


def _gm_gate(uv, v_gain, w_s, b_full, *, tm=512):
    m, two_w = uv.shape
    w = two_w // 2
    c = GM_CHUNK
    vmem = (6 * _nbytes((tm, w), BF16) + 2 * _nbytes((GM_GROUPS, c, c), F32)
            + 2 * _nbytes((c, w), F32) + 8 * _nbytes((c, w), F32))
    return pl.pallas_call(
        _gm_body,
        grid=(m // tm,),
        in_specs=[
            pl.BlockSpec((tm, w), lambda i: (i, 0)),
            pl.BlockSpec((tm, w), lambda i: (i, 1)),
            pl.BlockSpec((1, w), lambda i: (0, 0)),
            pl.BlockSpec((GM_GROUPS, c, c), lambda i: (0, 0, 0)),
            pl.BlockSpec((c, w), lambda i: (0, 0)),
        ],
        out_specs=pl.BlockSpec((tm, w), lambda i: (i, 0)),
        out_shape=jax.ShapeDtypeStruct((m, w), BF16),
        scratch_shapes=[pltpu.VMEM((c, w), BF16)],
        compiler_params=_params(1, 2 * vmem),
    )(uv, uv, v_gain.reshape(1, w), w_s, b_full)


def _ret_body(q_ref, k_ref, v_ref, g_ref, cos_ref, sin_ref, intra_ref, qd_ref, kd_ref,
              cd_ref, gn_ref, o_ref, state_ref):
    c = RET_CHUNK
    half = RET_QK_DIM // 2

    @pl.when(pl.program_id(2) == 0)
    def _():
        state_ref[...] = jnp.zeros_like(state_ref)

    def rotate(x, cos, sin):
        x1, x2 = x[:, :half], x[:, half:]
        return jnp.concatenate([x1 * cos - x2 * sin, x1 * sin + x2 * cos], axis=1)

    for ci in range(q_ref.shape[0] // c):
        sl = slice(ci * c, (ci + 1) * c)
        cos = cos_ref[sl, :]
        sin = sin_ref[sl, :]
        qr = rotate(q_ref[sl, :].astype(F32), cos, sin)
        kr = rotate(k_ref[sl, :].astype(F32), cos, sin) * (RET_QK_DIM ** -0.5)
        qb = qr.astype(BF16)
        kb = kr.astype(BF16)
        vb = v_ref[sl, :]
        scores = lax.dot_general(qb, kb, (((1,), (1,)), ((), ())),
                                 preferred_element_type=F32) * intra_ref[...]
        inner = jnp.dot(scores.astype(BF16), vb, preferred_element_type=F32)
        state = state_ref[...]
        cross = jnp.dot(qb, state.astype(BF16), preferred_element_type=F32) * qd_ref[...]
        kdec = (kr * kd_ref[...]).astype(BF16)
        state_ref[...] = state * cd_ref[...] + lax.dot_general(
            kdec, vb, (((0,), (0,)), ((), ())), preferred_element_type=F32)
        o = inner + cross
        y = _rms_rows(o, gn_ref[...])
        o_ref[sl, :] = (jax.nn.silu(g_ref[sl, :].astype(F32)) * y).astype(o_ref.dtype)


def _retention(qkvg, gn_gain, *, tc=512):
    b, s, _ = qkvg.shape
    hq, hv, c = RET_QK_DIM, RET_V_DIM, RET_CHUNK
    nh = RET_HEADS
    half = hq // 2
    inv = ROPE_BASE ** (-jnp.linspace(0.0, 1.0, half, dtype=F32))
    ang = jnp.arange(s).astype(F32)[:, None] * inv[None, :]
    cos, sin = jnp.cos(ang), jnp.sin(ang)
    log_gamma = jnp.log(1.0 - 2.0 ** (-5.0 - jnp.arange(nh, dtype=F32)))
    idx = jnp.arange(c, dtype=F32)
    diff = idx[:, None] - idx[None, :]
    intra = jnp.where(diff[None] >= 0,
                      jnp.exp(jnp.maximum(diff, 0.0)[None] * log_gamma[:, None, None]), 0.0)
    qd = jnp.exp((idx + 1.0)[None, :] * log_gamma[:, None])[:, :, None]
    kd = jnp.exp((c - 1.0 - idx)[None, :] * log_gamma[:, None])[:, :, None]
    cd = jnp.broadcast_to(jnp.exp(c * log_gamma)[:, None, None], (nh, 1, hv))

    vmem = (4 * _nbytes((tc, hq), BF16) + 6 * _nbytes((tc, hv), BF16)
            + 4 * _nbytes((tc, half), F32) + 2 * _nbytes((c, c), F32)
            + 4 * _nbytes((c, 128), F32) + 3 * _nbytes((hq, hv), F32)
            + 10 * _nbytes((c, hv), F32))
    return pl.pallas_call(
        _ret_body,
        grid=(b, nh, s // tc),
        in_specs=[
            pl.BlockSpec((None, tc, hq), lambda bi, h, ci: (bi, ci, h)),
            pl.BlockSpec((None, tc, hq), lambda bi, h, ci: (bi, ci, nh + h)),
            pl.BlockSpec((None, tc, hv), lambda bi, h, ci: (bi, ci, nh + h)),
            pl.BlockSpec((None, tc, hv), lambda bi, h, ci: (bi, ci, 2 * nh + h)),
            pl.BlockSpec((tc, half), lambda bi, h, ci: (ci, 0)),
            pl.BlockSpec((tc, half), lambda bi, h, ci: (ci, 0)),
            pl.BlockSpec((None, c, c), lambda bi, h, ci: (h, 0, 0)),
            pl.BlockSpec((None, c, 1), lambda bi, h, ci: (h, 0, 0)),
            pl.BlockSpec((None, c, 1), lambda bi, h, ci: (h, 0, 0)),
            pl.BlockSpec((None, 1, hv), lambda bi, h, ci: (h, 0, 0)),
            pl.BlockSpec((1, hv), lambda bi, h, ci: (0, h)),
        ],
        out_specs=pl.BlockSpec((None, tc, hv), lambda bi, h, ci: (bi, ci, h)),
        out_shape=jax.ShapeDtypeStruct((b, s, nh * hv), BF16),
        scratch_shapes=[pltpu.VMEM((hq, hv), F32)],
        compiler_params=_params(3, 2 * vmem),
    )(qkvg, qkvg, qkvg, qkvg, cos, sin, intra, qd, kd, cd, gn_gain.reshape(1, nh * hv))


def kernel(x, sb_norm, sb_w_qkv, sb_w_o, gm_norm, gm_w_in, gm_v_norm, gm_w_s, gm_b_s, gm_w_o,
           ret_norm, ret_w_qkvg, ret_gn, ret_w_o, ffn_norm, ffn_w_gate_up, ffn_w_down,
           final_norm):
    b, s, d = x.shape
    n = b * s
    depth = ffn_norm.shape[0]
    sb_w_o, gm_w_o, ret_w_o = (w.astype(BF16) for w in (sb_w_o, gm_w_o, ret_w_o))
    last_w_gate_up = ffn_w_gate_up[depth - 1:].astype(BF16)
    last_w_down = ffn_w_down[depth - 1:].astype(BF16)
    h = x.reshape(n, d)
    for i in range(depth):
        kind, j = i % N_MIXERS, i // N_MIXERS
        if kind == 0:
            q_scale = jnp.concatenate([
                jnp.full((d,), (SB_HEAD_DIM ** -0.5) * LOG2_E, F32), jnp.ones((2 * d,), F32)])
            qkv = _matmul(h, sb_w_qkv, j, gain=sb_norm[j], col_scale=q_scale, out_dtype=BF16)
            o = _sb_attention(qkv.reshape(b, s, 3 * d))
            h = _matmul(o.reshape(n, d), sb_w_o, j, residual=h, out_dtype=F32)
        elif kind == 1:
            uv = _matmul(h, gm_w_in, j, gain=gm_norm[j], act="gelu", out_dtype=BF16)
            b_full = jnp.repeat(gm_b_s[j].T, GM_GROUP_DIM, axis=1)
            gated = _gm_gate(uv, gm_v_norm[j], gm_w_s[j], b_full)
            h = _matmul(gated, gm_w_o, j, residual=h, out_dtype=F32)
        else:
            qkvg = _matmul(h, ret_w_qkvg, j, gain=ret_norm[j], out_dtype=BF16)
            o = _retention(qkvg.reshape(b, s, 6 * d), ret_gn[j])
            h = _matmul(o.reshape(n, 2 * d), ret_w_o, j, residual=h, out_dtype=F32, tn=512)
        if i == depth - 1:
            h = _ffn(h, ffn_norm[i], last_w_gate_up, last_w_down, 0, final_gain=final_norm)
        else:
            hidden = _matmul(h, ffn_w_gate_up, i, gain=ffn_norm[i], act="swiglu",
                             out_dtype=BF16, tn=512)
            h = _matmul(hidden, ffn_w_down, i, residual=h, out_dtype=F32, tn=256)
    return h.reshape(b, s, d)
```

```python
import functools

import jax
import jax.numpy as jnp
from jax import lax
from jax.experimental import pallas as pl
from jax.experimental.pallas import tpu as pltpu

F32 = jnp.float32
BF16 = jnp.bfloat16

EPS = 1e-6
N_MIXERS = 3
SB_HEADS = 16
SB_HEAD_DIM = 128
SB_TILE = 256
SB_HEADS_PER_STEP = 4
LOG2_E = 1.4426950408889634
SB_UNDERFLOW_BITS = 160.0
GM_GROUPS = 16
GM_GROUP_DIM = 128
GM_CHUNK = 128
RET_HEADS = 8
RET_QK_DIM = 256
RET_V_DIM = 512
RET_CHUNK = 128
ROPE_BASE = 10000.0

V7X_VMEM_LIMIT_BYTES = 60000 * 1024
NORM_ROWS = 256


def _nbytes(shape, dtype):
    n = 1
    for s in shape:
        n *= s
    return n * jnp.dtype(dtype).itemsize


def _params(n_axes, vmem_bytes):
    return pltpu.CompilerParams(
        dimension_semantics=("arbitrary",) * n_axes,
        vmem_limit_bytes=int(min(vmem_bytes, V7X_VMEM_LIMIT_BYTES)),
    )


def _rms_rows(xf, gain):
    y = xf * lax.rsqrt(jnp.mean(xf * xf, axis=-1, keepdims=True) + EPS)
    return y * gain


def _norm_into(x_ref, g_ref, xn_ref):
    rows = x_ref.shape[0]
    step = min(NORM_ROWS, rows)

    def body(r, carry):
        sl = pl.ds(pl.multiple_of(r * step, step), step)
        xn_ref[sl, :] = _rms_rows(x_ref[sl, :], g_ref[...]).astype(xn_ref.dtype)
        return carry

    lax.fori_loop(0, rows // step, body, 0)


def _mm_body(*refs, norm, act, has_res, has_scale):
    it = iter(refs)
    x_ref = next(it)
    g_ref = next(it) if norm else None
    w_ref = next(it)
    wu_ref = next(it) if act == "swiglu" else None
    s_ref = next(it) if has_scale else None
    r_ref = next(it) if has_res else None
    o_ref = next(it)
    xn_ref = next(it) if norm else None

    if norm:
        @pl.when(pl.program_id(1) == 0)
        def _():
            _norm_into(x_ref, g_ref, xn_ref)
        lhs = xn_ref[...]
    else:
        lhs = x_ref[...]
    acc = jnp.dot(lhs, w_ref[...], preferred_element_type=F32)
    if act == "gelu":
        acc = jax.nn.gelu(acc)
    if act == "swiglu":
        acc = jax.nn.silu(acc) * jnp.dot(lhs, wu_ref[...], preferred_element_type=F32)
    if has_scale:
        acc = acc * s_ref[...]
    if has_res:
        acc = acc + r_ref[...]
    o_ref[...] = acc.astype(o_ref.dtype)


def _matmul(x, w_stack, layer, *, gain=None, residual=None, act=None, col_scale=None,
            out_dtype, tm=1024, tn=1024):
    m, k = x.shape
    n = w_stack.shape[-1]
    if act == "swiglu":
        n //= 2
    norm = gain is not None
    has_res = residual is not None
    has_scale = col_scale is not None
    in_specs = [pl.BlockSpec((tm, k), lambda i, j: (i, 0))]
    args = [x]
    if norm:
        in_specs.append(pl.BlockSpec((1, k), lambda i, j: (0, 0)))
        args.append(gain.reshape(1, k))
    in_specs.append(pl.BlockSpec((None, k, tn), lambda i, j: (layer, 0, j)))
    args.append(w_stack)
    if act == "swiglu":
        in_specs.append(pl.BlockSpec((None, k, tn), lambda i, j: (layer, 0, n // tn + j)))
        args.append(w_stack)
    if has_scale:
        in_specs.append(pl.BlockSpec((1, tn), lambda i, j: (0, j)))
        args.append(col_scale.reshape(1, n))
    if has_res:
        in_specs.append(pl.BlockSpec((tm, tn), lambda i, j: (i, j)))
        args.append(residual)
    scratch = [pltpu.VMEM((tm, k), BF16)] if norm else []
    n_w = 2 if act == "swiglu" else 1
    vmem = (2 * _nbytes((tm, k), x.dtype) + 2 * n_w * _nbytes((k, tn), BF16)
            + 2 * _nbytes((tm, tn), out_dtype) + 2 * n_w * _nbytes((tm, tn), F32)
            + (2 * _nbytes((tm, tn), F32) if has_res else 0)
            + (_nbytes((tm, k), BF16) + 4 * _nbytes((NORM_ROWS, k), F32) if norm else 0))
    return pl.pallas_call(
        functools.partial(_mm_body, norm=norm, act=act, has_res=has_res,
                          has_scale=has_scale),
        grid=(m // tm, n // tn),
        in_specs=in_specs,
        out_specs=pl.BlockSpec((tm, tn), lambda i, j: (i, j)),
        out_shape=jax.ShapeDtypeStruct((m, n), out_dtype),
        scratch_shapes=scratch,
        compiler_params=_params(2, vmem),
    )(*args)


def _ffn_body(*refs, final):
    if final:
        h_ref, g_ref, wg_ref, wu_ref, wd_ref, fg_ref, o_ref, xn_ref = refs
    else:
        h_ref, g_ref, wg_ref, wu_ref, wd_ref, o_ref, xn_ref = refs
        fg_ref = None
    f = pl.program_id(1)

    @pl.when(f == 0)
    def _():
        _norm_into(h_ref, g_ref, xn_ref)
        o_ref[...] = h_ref[...]

    xn = xn_ref[...]
    gate = jnp.dot(xn, wg_ref[...], preferred_element_type=F32)
    up = jnp.dot(xn, wu_ref[...], preferred_element_type=F32)
    act = (jax.nn.silu(gate) * up).astype(BF16)
    o_ref[...] += jnp.dot(act, wd_ref[...], preferred_element_type=F32)

    if final:
        @pl.when(f == pl.num_programs(1) - 1)
        def _():
            rows = o_ref.shape[0]
            step = min(NORM_ROWS, rows)

            def body(r, carry):
                sl = pl.ds(pl.multiple_of(r * step, step), step)
                o_ref[sl, :] = _rms_rows(o_ref[sl, :], fg_ref[...])
                return carry

            lax.fori_loop(0, rows // step, body, 0)


def _ffn(h, gain, w_gu_stack, w_d_stack, layer, *, final_gain=None, tm=512, tf=512):
    m, d = h.shape
    ff = w_d_stack.shape[1]
    nf = ff // tf
    final = final_gain is not None
    in_specs = [
        pl.BlockSpec((tm, d), lambda i, f: (i, 0)),
        pl.BlockSpec((1, d), lambda i, f: (0, 0)),
        pl.BlockSpec((None, d, tf), lambda i, f: (layer, 0, f)),
        pl.BlockSpec((None, d, tf), lambda i, f: (layer, 0, nf + f)),
        pl.BlockSpec((None, tf, d), lambda i, f: (layer, f, 0)),
    ]
    args = [h, gain.reshape(1, d), w_gu_stack, w_gu_stack, w_d_stack]
    if final:
        in_specs.append(pl.BlockSpec((1, d), lambda i, f: (0, 0)))
        args.append(final_gain.reshape(1, d))
    vmem = (4 * _nbytes((tm, d), F32) + _nbytes((tm, d), BF16)
            + 6 * _nbytes((d, tf), BF16) + 3 * _nbytes((tm, tf), F32)
            + 2 * _nbytes((tm, d), F32) + 4 * _nbytes((NORM_ROWS, d), F32))
    return pl.pallas_call(
        functools.partial(_ffn_body, final=final),
        grid=(m // tm, nf),
        in_specs=in_specs,
        out_specs=pl.BlockSpec((tm, d), lambda i, f: (i, 0)),
        out_shape=jax.ShapeDtypeStruct((m, d), F32),
        scratch_shapes=[pltpu.VMEM((tm, d), BF16)],
        compiler_params=_params(2, vmem),
    )(*args)


def _sb_body(q_ref, k_ref, v_ref, o_ref):
    t = SB_TILE
    hd = SB_HEAD_DIM
    heads = range(SB_HEADS_PER_STEP)
    hs = [slice(hh * hd, (hh + 1) * hd) for hh in heads]
    sign_bit = jnp.int32(-2 ** 31)
    row = lax.broadcasted_iota(jnp.int32, (t, t), 0)
    col = lax.broadcasted_iota(jnp.int32, (t, t), 1)
    after = (row > col).astype(BF16)
    causal = col < row

    def keys(kb):
        return pl.ds(pl.multiple_of(kb * t, t), t)

    def score(i, kb):
        return [lax.dot_general(q_ref[keys(i), hs[hh]], k_ref[keys(kb), hs[hh]],
                                (((1,), (1,)), ((), ())), preferred_element_type=F32)
                for hh in heads]

    def keep_stage(w, masked):
        neg_abs = lax.bitcast_convert_type(
            lax.bitcast_convert_type(w, jnp.int32) | sign_bit, F32)
        keep = jnp.maximum(w, 0.0) + jnp.log2(1.0 + jnp.exp2(neg_abs))
        if masked:
            keep = jnp.where(causal, keep, 0.0)
        tail = jnp.dot(keep.astype(BF16), after, preferred_element_type=F32)
        return w - keep, tail, tail[:, :1] + keep[:, :1]

    def weigh(kb, hh, own, tail, run, acc, masked):
        a = jnp.exp2(own - tail - run)
        if masked:
            a = jnp.where(causal, a, 0.0)
        return acc + jnp.dot(a.astype(BF16), v_ref[keys(kb), hs[hh]],
                             preferred_element_type=F32)

    def least(runs):
        m = runs[0]
        for r in runs[1:]:
            m = jnp.minimum(m, r)
        return jnp.min(m)

    def live(carry):
        kb, low, _, _ = carry
        return jnp.logical_and(kb >= 0, low < SB_UNDERFLOW_BITS)

    def query_tile(i, carry):
        has_prev = i > 0
        prev = jnp.maximum(i - 1, 0)
        ws_d = score(i, i)
        ws_p = score(i, prev)
        st_d = [keep_stage(ws_d[hh], True) for hh in heads]
        st_p = [keep_stage(ws_p[hh], False) for hh in heads]
        acc_d = [weigh(i, hh, st_d[hh][0], st_d[hh][1], jnp.zeros((t, 1), F32),
                       jnp.zeros((t, hd), F32), True) for hh in heads]
        acc_p = [weigh(prev, hh, st_p[hh][0], st_p[hh][1], st_d[hh][2], acc_d[hh], False)
                 for hh in heads]
        accs = tuple(jnp.where(has_prev, acc_p[hh], acc_d[hh]) for hh in heads)
        runs = tuple(st_d[hh][2] + jnp.where(has_prev, st_p[hh][2], 0.0) for hh in heads)

        def step(state):
            kb, _, runs, accs = state
            ws = score(i, kb)
            stages = [keep_stage(ws[hh], False) for hh in heads]
            accs = tuple(weigh(kb, hh, stages[hh][0], stages[hh][1], runs[hh], accs[hh],
                               False) for hh in heads)
            runs = tuple(runs[hh] + stages[hh][2] for hh in heads)
            return kb - 1, least(runs), runs, accs

        _, _, _, accs = lax.while_loop(live, step, (i - 2, least(runs), runs, accs))
        for hh in heads:
            o_ref[keys(i), hs[hh]] = accs[hh].astype(o_ref.dtype)
        return carry

    lax.fori_loop(0, q_ref.shape[0] // t, query_tile, 0)


def _sb_attention(qkv):
    b, s, _ = qkv.shape
    t = SB_TILE
    wd = SB_HEAD_DIM * SB_HEADS_PER_STEP
    groups = SB_HEADS // SB_HEADS_PER_STEP
    vmem = (8 * _nbytes((s, wd), BF16) + 24 * SB_HEADS_PER_STEP * _nbytes((t, t), F32))
    return pl.pallas_call(
        _sb_body,
        grid=(b, groups),
        in_specs=[
            pl.BlockSpec((None, s, wd), lambda bi, h: (bi, 0, h)),
            pl.BlockSpec((None, s, wd), lambda bi, h: (bi, 0, groups + h)),
            pl.BlockSpec((None, s, wd), lambda bi, h: (bi, 0, 2 * groups + h)),
        ],
        out_specs=pl.BlockSpec((None, s, wd), lambda bi, h: (bi, 0, h)),
        out_shape=jax.ShapeDtypeStruct((b, s, SB_HEADS * SB_HEAD_DIM), BF16),
        compiler_params=_params(2, 2 * vmem),
    )(qkv, qkv, qkv)


def _gm_body(u_ref, v_ref, vg_ref, ws_ref, bs_ref, o_ref, vn_ref):
    rows = u_ref.shape[0]
    c = GM_CHUNK
    row = lax.broadcasted_iota(jnp.int32, (c, c), 0)
    col = lax.broadcasted_iota(jnp.int32, (c, c), 1)
    lower = row >= col
    for ci in range(rows // c):
        sl = slice(ci * c, (ci + 1) * c)
        vf = v_ref[sl, :].astype(F32)
        mu = jnp.mean(vf, axis=-1, keepdims=True)
        xc = vf - mu
        y = xc * lax.rsqrt(jnp.mean(xc * xc, axis=-1, keepdims=True) + EPS)
        vn_ref[...] = (y * vg_ref[...]).astype(BF16)
        for g in range(GM_GROUPS):
            gs = slice(g * GM_GROUP_DIM, (g + 1) * GM_GROUP_DIM)
            ws = jnp.where(lower, ws_ref[g], 0.0).astype(BF16)
            mixed = jnp.dot(ws, vn_ref[:, gs], preferred_element_type=F32) + bs_ref[:, gs]
            o_ref[sl, gs] = (u_ref[sl, gs].astype(F32) * mixed).astype(o_ref.dtype)


def _gm_gate(uv, v_gain, w_s, b_full, *, tm=512):
    m, two_w = uv.shape
    w = two_w // 2
    c = GM_CHUNK
    vmem = (6 * _nbytes((tm, w), BF16) + 2 * _nbytes((GM_GROUPS, c, c), F32)
            + 2 * _nbytes((c, w), F32) + 8 * _nbytes((c, w), F32))
    return pl.pallas_call(
        _gm_body,
        grid=(m // tm,),
        in_specs=[
            pl.BlockSpec((tm, w), lambda i: (i, 0)),
            pl.BlockSpec((tm, w), lambda i: (i, 1)),
            pl.BlockSpec((1, w), lambda i: (0, 0)),
            pl.BlockSpec((GM_GROUPS, c, c), lambda i: (0, 0, 0)),
            pl.BlockSpec((c, w), lambda i: (0, 0)),
        ],
        out_specs=pl.BlockSpec((tm, w), lambda i: (i, 0)),
        out_shape=jax.ShapeDtypeStruct((m, w), BF16),
        scratch_shapes=[pltpu.VMEM((c, w), BF16)],
        compiler_params=_params(1, 2 * vmem),
    )(uv, uv, v_gain.reshape(1, w), w_s, b_full)


def _ret_body(q_ref, k_ref, v_ref, g_ref, cos_ref, sin_ref, intra_ref, qd_ref, kd_ref,
              cd_ref, gn_ref, o_ref, state_ref):
    c = RET_CHUNK
    half = RET_QK_DIM // 2

    @pl.when(pl.program_id(2) == 0)
    def _():
        state_ref[...] = jnp.zeros_like(state_ref)

    def rotate(x, cos, sin):
        x1, x2 = x[:, :half], x[:, half:]
        return jnp.concatenate([x1 * cos - x2 * sin, x1 * sin + x2 * cos], axis=1)

    for ci in range(q_ref.shape[0] // c):
        sl = slice(ci * c, (ci + 1) * c)
        cos = cos_ref[sl, :]
        sin = sin_ref[sl, :]
        qr = rotate(q_ref[sl, :].astype(F32), cos, sin)
        kr = rotate(k_ref[sl, :].astype(F32), cos, sin) * (RET_QK_DIM ** -0.5)
        qb = qr.astype(BF16)
        kb = kr.astype(BF16)
        vb = v_ref[sl, :]
        scores = lax.dot_general(qb, kb, (((1,), (1,)), ((), ())),
                                 preferred_element_type=F32) * intra_ref[...]
        inner = jnp.dot(scores.astype(BF16), vb, preferred_element_type=F32)
        state = state_ref[...]
        cross = jnp.dot(qb, state.astype(BF16), preferred_element_type=F32) * qd_ref[...]
        kdec = (kr * kd_ref[...]).astype(BF16)
        state_ref[...] = state * cd_ref[...] + lax.dot_general(
            kdec, vb, (((0,), (0,)), ((), ())), preferred_element_type=F32)
        o = inner + cross
        y = _rms_rows(o, gn_ref[...])
        o_ref[sl, :] = (jax.nn.silu(g_ref[sl, :].astype(F32)) * y).astype(o_ref.dtype)


def _retention(qkvg, gn_gain, *, tc=1024):
    b, s, _ = qkvg.shape
    hq, hv, c = RET_QK_DIM, RET_V_DIM, RET_CHUNK
    nh = RET_HEADS
    half = hq // 2
    inv = ROPE_BASE ** (-jnp.linspace(0.0, 1.0, half, dtype=F32))
    ang = jnp.arange(s).astype(F32)[:, None] * inv[None, :]
    cos, sin = jnp.cos(ang), jnp.sin(ang)
    log_gamma = jnp.log(1.0 - 2.0 ** (-5.0 - jnp.arange(nh, dtype=F32)))
    idx = jnp.arange(c, dtype=F32)
    diff = idx[:, None] - idx[None, :]
    intra = jnp.where(diff[None] >= 0,
                      jnp.exp(jnp.maximum(diff, 0.0)[None] * log_gamma[:, None, None]), 0.0)
    qd = jnp.exp((idx + 1.0)[None, :] * log_gamma[:, None])[:, :, None]
    kd = jnp.exp((c - 1.0 - idx)[None, :] * log_gamma[:, None])[:, :, None]
    cd = jnp.broadcast_to(jnp.exp(c * log_gamma)[:, None, None], (nh, 1, hv))

    vmem = (4 * _nbytes((tc, hq), BF16) + 6 * _nbytes((tc, hv), BF16)
            + 4 * _nbytes((tc, half), F32) + 2 * _nbytes((c, c), F32)
            + 4 * _nbytes((c, 128), F32) + 3 * _nbytes((hq, hv), F32)
            + 10 * _nbytes((c, hv), F32))
    return pl.pallas_call(
        _ret_body,
        grid=(b, nh, s // tc),
        in_specs=[
            pl.BlockSpec((None, tc, hq), lambda bi, h, ci: (bi, ci, h)),
            pl.BlockSpec((None, tc, hq), lambda bi, h, ci: (bi, ci, nh + h)),
            pl.BlockSpec((None, tc, hv), lambda bi, h, ci: (bi, ci, nh + h)),
            pl.BlockSpec((None, tc, hv), lambda bi, h, ci: (bi, ci, 2 * nh + h)),
            pl.BlockSpec((tc, half), lambda bi, h, ci: (ci, 0)),
            pl.BlockSpec((tc, half), lambda bi, h, ci: (ci, 0)),
            pl.BlockSpec((None, c, c), lambda bi, h, ci: (h, 0, 0)),
            pl.BlockSpec((None, c, 1), lambda bi, h, ci: (h, 0, 0)),
            pl.BlockSpec((None, c, 1), lambda bi, h, ci: (h, 0, 0)),
            pl.BlockSpec((None, 1, hv), lambda bi, h, ci: (h, 0, 0)),
            pl.BlockSpec((1, hv), lambda bi, h, ci: (0, h)),
        ],
        out_specs=pl.BlockSpec((None, tc, hv), lambda bi, h, ci: (bi, ci, h)),
        out_shape=jax.ShapeDtypeStruct((b, s, nh * hv), BF16),
        scratch_shapes=[pltpu.VMEM((hq, hv), F32)],
        compiler_params=_params(3, 2 * vmem),
    )(qkvg, qkvg, qkvg, qkvg, cos, sin, intra, qd, kd, cd, gn_gain.reshape(1, nh * hv))


def kernel(x, sb_norm, sb_w_qkv, sb_w_o, gm_norm, gm_w_in, gm_v_norm, gm_w_s, gm_b_s, gm_w_o,
           ret_norm, ret_w_qkvg, ret_gn, ret_w_o, ffn_norm, ffn_w_gate_up, ffn_w_down,
           final_norm):
    b, s, d = x.shape
    n = b * s
    depth = ffn_norm.shape[0]
    sb_w_qkv, sb_w_o, gm_w_in, gm_w_o, ret_w_qkvg, ret_w_o, ffn_w_gate_up, ffn_w_down = (
        w.astype(BF16) for w in (sb_w_qkv, sb_w_o, gm_w_in, gm_w_o, ret_w_qkvg, ret_w_o,
                                 ffn_w_gate_up, ffn_w_down))
    h = x.reshape(n, d)
    for i in range(depth):
        kind, j = i % N_MIXERS, i // N_MIXERS
        if kind == 0:
            q_scale = jnp.concatenate([
                jnp.full((d,), (SB_HEAD_DIM ** -0.5) * LOG2_E, F32), jnp.ones((2 * d,), F32)])
            qkv = _matmul(h, sb_w_qkv, j, gain=sb_norm[j], col_scale=q_scale, out_dtype=BF16)
            o = _sb_attention(qkv.reshape(b, s, 3 * d))
            h = _matmul(o.reshape(n, d), sb_w_o, j, residual=h, out_dtype=F32)
        elif kind == 1:
            uv = _matmul(h, gm_w_in, j, gain=gm_norm[j], act="gelu", out_dtype=BF16)
            b_full = jnp.repeat(gm_b_s[j].T, GM_GROUP_DIM, axis=1)
            gated = _gm_gate(uv, gm_v_norm[j], gm_w_s[j], b_full)
            h = _matmul(gated, gm_w_o, j, residual=h, out_dtype=F32)
        else:
            qkvg = _matmul(h, ret_w_qkvg, j, gain=ret_norm[j], out_dtype=BF16)
            o = _retention(qkvg.reshape(b, s, 6 * d), ret_gn[j])
            h = _matmul(o.reshape(n, 2 * d), ret_w_o, j, residual=h, out_dtype=F32, tn=512)
        if i == depth - 1:
            h = _ffn(h, ffn_norm[i], ffn_w_gate_up, ffn_w_down, i, final_gain=final_norm)
        else:
            hidden = _matmul(h, ffn_w_gate_up, i, gain=ffn_norm[i], act="swiglu",
                             out_dtype=BF16, tn=512)
            h = _matmul(hidden, ffn_w_down, i, residual=h, out_dtype=F32, tn=512)
    return h.reshape(b, s, d)
```

```python
import functools

import jax
import jax.numpy as jnp
from jax import lax
from jax.experimental import pallas as pl
from jax.experimental.pallas import tpu as pltpu

F32 = jnp.float32
BF16 = jnp.bfloat16

EPS = 1e-6
N_MIXERS = 3
SB_HEADS = 16
SB_HEAD_DIM = 128
SB_TILE = 256
SB_HEADS_PER_STEP = 4
LOG2_E = 1.4426950408889634
SB_UNDERFLOW_BITS = 160.0
GM_GROUPS = 16
GM_GROUP_DIM = 128
GM_CHUNK = 128
RET_HEADS = 8
RET_QK_DIM = 256
RET_V_DIM = 512
RET_CHUNK = 128
ROPE_BASE = 10000.0

V7X_VMEM_LIMIT_BYTES = 60000 * 1024
NORM_ROWS = 256
BF16_SUBLANES = 16


def _nbytes(shape, dtype):
    n = 1
    for s in shape:
        n *= s
    return n * jnp.dtype(dtype).itemsize


def _params(n_axes, vmem_bytes):
    return pltpu.CompilerParams(
        dimension_semantics=("arbitrary",) * n_axes,
        vmem_limit_bytes=int(min(vmem_bytes, V7X_VMEM_LIMIT_BYTES)),
    )


def _rms_rows(xf, gain):
    y = xf * lax.rsqrt(jnp.mean(xf * xf, axis=-1, keepdims=True) + EPS)
    return y * gain


def _norm_into(x_ref, g_ref, xn_ref):
    rows = x_ref.shape[0]
    step = min(NORM_ROWS, rows)

    def body(r, carry):
        sl = pl.ds(pl.multiple_of(r * step, step), step)
        xn_ref[sl, :] = _rms_rows(x_ref[sl, :], g_ref[...]).astype(xn_ref.dtype)
        return carry

    lax.fori_loop(0, rows // step, body, 0)


def _mm_body(*refs, norm, act, has_res, has_scale, n_jobs):
    it = iter(refs)
    x_ref = next(it)
    g_ref = next(it) if norm else None
    w_ref = next(it)
    wu_ref = next(it) if act == "swiglu" else None
    s_ref = next(it) if has_scale else None
    r_ref = next(it) if has_res else None
    raw_refs = [next(it) for _ in range(n_jobs)]
    o_ref = next(it)
    rounded_refs = [next(it) for _ in range(n_jobs)]
    xn_ref = next(it) if norm else None

    for raw_ref, rounded_ref in zip(raw_refs, rounded_refs):
        rounded_ref[...] = raw_ref[...].astype(rounded_ref.dtype)

    if norm:
        @pl.when(pl.program_id(1) == 0)
        def _():
            _norm_into(x_ref, g_ref, xn_ref)
        lhs = xn_ref[...]
    else:
        lhs = x_ref[...]
    acc = jnp.dot(lhs, w_ref[...], preferred_element_type=F32)
    if act == "gelu":
        acc = jax.nn.gelu(acc)
    if act == "swiglu":
        acc = jax.nn.silu(acc) * jnp.dot(lhs, wu_ref[...], preferred_element_type=F32)
    if has_scale:
        acc = acc * s_ref[...]
    if has_res:
        acc = acc + r_ref[...]
    o_ref[...] = acc.astype(o_ref.dtype)


def _rows_per_step(rows, steps):
    for rb in range(BF16_SUBLANES, rows + 1, BF16_SUBLANES):
        if rows % rb == 0 and rows // rb <= steps:
            return rb
    raise ValueError((rows, steps))


def _matmul(x, w, *, gain=None, residual=None, act=None, col_scale=None, round_jobs=(),
            out_dtype, tm=1024, tn=1024):
    m, k = x.shape
    n = w.shape[-1]
    if act == "swiglu":
        n //= 2
    nj = n // tn
    steps = (m // tm) * nj
    norm = gain is not None
    has_res = residual is not None
    has_scale = col_scale is not None
    in_specs = [pl.BlockSpec((tm, k), lambda i, j: (i, 0))]
    args = [x]
    if norm:
        in_specs.append(pl.BlockSpec((1, k), lambda i, j: (0, 0)))
        args.append(gain.reshape(1, k))
    in_specs.append(pl.BlockSpec((k, tn), lambda i, j: (0, j)))
    args.append(w)
    if act == "swiglu":
        in_specs.append(pl.BlockSpec((k, tn), lambda i, j: (0, nj + j)))
        args.append(w)
    if has_scale:
        in_specs.append(pl.BlockSpec((1, tn), lambda i, j: (0, j)))
        args.append(col_scale.reshape(1, n))
    if has_res:
        in_specs.append(pl.BlockSpec((tm, tn), lambda i, j: (i, j)))
        args.append(residual)
    out_specs = [pl.BlockSpec((tm, tn), lambda i, j: (i, j))]
    out_shape = [jax.ShapeDtypeStruct((m, n), out_dtype)]
    scratch = [pltpu.VMEM((tm, k), BF16)] if norm else []
    n_w = 2 if act == "swiglu" else 1
    vmem = (2 * _nbytes((tm, k), x.dtype) + 2 * n_w * _nbytes((k, tn), BF16)
            + 2 * _nbytes((tm, tn), out_dtype) + 2 * n_w * _nbytes((tm, tn), F32)
            + (2 * _nbytes((tm, tn), F32) if has_res else 0)
            + (_nbytes((tm, k), BF16) + 4 * _nbytes((NORM_ROWS, k), F32) if norm else 0))
    for stack, layer in round_jobs:
        _, rows, cols = stack.shape
        rb = _rows_per_step(rows, steps)
        last = rows // rb - 1
        in_specs.append(pl.BlockSpec(
            (None, rb, cols),
            lambda i, j, layer=layer, last=last: (layer, jnp.minimum(i * nj + j, last), 0)))
        args.append(stack)
        out_specs.append(pl.BlockSpec(
            (rb, cols), lambda i, j, last=last: (jnp.minimum(i * nj + j, last), 0)))
        out_shape.append(jax.ShapeDtypeStruct((rows, cols), BF16))
        vmem += 2 * _nbytes((rb, cols), F32) + 3 * _nbytes((rb, cols), BF16)
    outs = pl.pallas_call(
        functools.partial(_mm_body, norm=norm, act=act, has_res=has_res,
                          has_scale=has_scale, n_jobs=len(round_jobs)),
        grid=(m // tm, nj),
        in_specs=in_specs,
        out_specs=out_specs,
        out_shape=out_shape,
        scratch_shapes=scratch,
        compiler_params=_params(2, vmem),
    )(*args)
    return (outs[0], list(outs[1:])) if round_jobs else outs[0]


def _ffn_body(*refs, final):
    if final:
        h_ref, g_ref, wg_ref, wu_ref, wd_ref, fg_ref, o_ref, xn_ref = refs
    else:
        h_ref, g_ref, wg_ref, wu_ref, wd_ref, o_ref, xn_ref = refs
        fg_ref = None
    f = pl.program_id(1)

    @pl.when(f == 0)
    def _():
        _norm_into(h_ref, g_ref, xn_ref)
        o_ref[...] = h_ref[...]

    xn = xn_ref[...]
    gate = jnp.dot(xn, wg_ref[...], preferred_element_type=F32)
    up = jnp.dot(xn, wu_ref[...], preferred_element_type=F32)
    act = (jax.nn.silu(gate) * up).astype(BF16)
    o_ref[...] += jnp.dot(act, wd_ref[...], preferred_element_type=F32)

    if final:
        @pl.when(f == pl.num_programs(1) - 1)
        def _():
            rows = o_ref.shape[0]
            step = min(NORM_ROWS, rows)

            def body(r, carry):
                sl = pl.ds(pl.multiple_of(r * step, step), step)
                o_ref[sl, :] = _rms_rows(o_ref[sl, :], fg_ref[...])
                return carry

            lax.fori_loop(0, rows // step, body, 0)


def _ffn(h, gain, w_gate_up, w_down, *, final_gain=None, tm=512, tf=512):
    m, d = h.shape
    ff = w_down.shape[0]
    nf = ff // tf
    final = final_gain is not None
    in_specs = [
        pl.BlockSpec((tm, d), lambda i, f: (i, 0)),
        pl.BlockSpec((1, d), lambda i, f: (0, 0)),
        pl.BlockSpec((d, tf), lambda i, f: (0, f)),
        pl.BlockSpec((d, tf), lambda i, f: (0, nf + f)),
        pl.BlockSpec((tf, d), lambda i, f: (f, 0)),
    ]
    args = [h, gain.reshape(1, d), w_gate_up, w_gate_up, w_down]
    if final:
        in_specs.append(pl.BlockSpec((1, d), lambda i, f: (0, 0)))
        args.append(final_gain.reshape(1, d))
    vmem = (4 * _nbytes((tm, d), F32) + _nbytes((tm, d), BF16)
            + 6 * _nbytes((d, tf), BF16) + 3 * _nbytes((tm, tf), F32)
            + 2 * _nbytes((tm, d), F32) + 4 * _nbytes((NORM_ROWS, d), F32))
    return pl.pallas_call(
        functools.partial(_ffn_body, final=final),
        grid=(m // tm, nf),
        in_specs=in_specs,
        out_specs=pl.BlockSpec((tm, d), lambda i, f: (i, 0)),
        out_shape=jax.ShapeDtypeStruct((m, d), F32),
        scratch_shapes=[pltpu.VMEM((tm, d), BF16)],
        compiler_params=_params(2, vmem),
    )(*args)


def _sb_body(q_ref, k_ref, v_ref, o_ref):
    t = SB_TILE
    hd = SB_HEAD_DIM
    heads = range(SB_HEADS_PER_STEP)
    hs = [slice(hh * hd, (hh + 1) * hd) for hh in heads]
    sign_bit = jnp.int32(-2 ** 31)
    row = lax.broadcasted_iota(jnp.int32, (t, t), 0)
    col = lax.broadcasted_iota(jnp.int32, (t, t), 1)
    after = (row > col).astype(BF16)
    causal = col < row

    def keys(kb):
        return pl.ds(pl.multiple_of(kb * t, t), t)

    def score(i, kb):
        return [lax.dot_general(q_ref[keys(i), hs[hh]], k_ref[keys(kb), hs[hh]],
                                (((1,), (1,)), ((), ())), preferred_element_type=F32)
                for hh in heads]

    def keep_stage(w, masked):
        neg_abs = lax.bitcast_convert_type(
            lax.bitcast_convert_type(w, jnp.int32) | sign_bit, F32)
        keep = jnp.maximum(w, 0.0) + jnp.log2(1.0 + jnp.exp2(neg_abs))
        if masked:
            keep = jnp.where(causal, keep, 0.0)
        tail = jnp.dot(keep.astype(BF16), after, preferred_element_type=F32)
        return w - keep, tail, tail[:, :1] + keep[:, :1]

    def weigh(kb, hh, own, tail, run, acc, masked):
        a = jnp.exp2(own - tail - run)
        if masked:
            a = jnp.where(causal, a, 0.0)
        return acc + jnp.dot(a.astype(BF16), v_ref[keys(kb), hs[hh]],
                             preferred_element_type=F32)

    def least(runs):
        m = runs[0]
        for r in runs[1:]:
            m = jnp.minimum(m, r)
        return jnp.min(m)

    def live(carry):
        kb, low, _, _ = carry
        return jnp.logical_and(kb >= 0, low < SB_UNDERFLOW_BITS)

    def query_tile(i, carry):
        has_prev = i > 0
        prev = jnp.maximum(i - 1, 0)
        ws_d = score(i, i)
        ws_p = score(i, prev)
        st_d = [keep_stage(ws_d[hh], True) for hh in heads]
        st_p = [keep_stage(ws_p[hh], False) for hh in heads]
        acc_d = [weigh(i, hh, st_d[hh][0], st_d[hh][1], jnp.zeros((t, 1), F32),
                       jnp.zeros((t, hd), F32), True) for hh in heads]
        acc_p = [weigh(prev, hh, st_p[hh][0], st_p[hh][1], st_d[hh][2], acc_d[hh], False)
                 for hh in heads]
        accs = tuple(jnp.where(has_prev, acc_p[hh], acc_d[hh]) for hh in heads)
        runs = tuple(st_d[hh][2] + jnp.where(has_prev, st_p[hh][2], 0.0) for hh in heads)

        def step(state):
            kb, _, runs, accs = state
            ws = score(i, kb)
            stages = [keep_stage(ws[hh], False) for hh in heads]
            accs = tuple(weigh(kb, hh, stages[hh][0], stages[hh][1], runs[hh], accs[hh],
                               False) for hh in heads)
            runs = tuple(runs[hh] + stages[hh][2] for hh in heads)
            return kb - 1, least(runs), runs, accs

        _, _, _, accs = lax.while_loop(live, step, (i - 2, least(runs), runs, accs))
        for hh in heads:
            o_ref[keys(i), hs[hh]] = accs[hh].astype(o_ref.dtype)
        return carry

    lax.fori_loop(0, q_ref.shape[0] // t, query_tile, 0)


def _sb_attention(qkv):
    b, s, _ = qkv.shape
    t = SB_TILE
    wd = SB_HEAD_DIM * SB_HEADS_PER_STEP
    groups = SB_HEADS // SB_HEADS_PER_STEP
    vmem = (8 * _nbytes((s, wd), BF16) + 24 * SB_HEADS_PER_STEP * _nbytes((t, t), F32))
    return pl.pallas_call(
        _sb_body,
        grid=(b, groups),
        in_specs=[
            pl.BlockSpec((None, s, wd), lambda bi, h: (bi, 0, h)),
            pl.BlockSpec((None, s, wd), lambda bi, h: (bi, 0, groups + h)),
            pl.BlockSpec((None, s, wd), lambda bi, h: (bi, 0, 2 * groups + h)),
        ],
        out_specs=pl.BlockSpec((None, s, wd), lambda bi, h: (bi, 0, h)),
        out_shape=jax.ShapeDtypeStruct((b, s, SB_HEADS * SB_HEAD_DIM), BF16),
        compiler_params=_params(2, 2 * vmem),
    )(qkv, qkv, qkv)


def _gm_body(u_ref, v_ref, vg_ref, ws_ref, bs_ref, o_ref, vn_ref):
    rows = u_ref.shape[0]
    c = GM_CHUNK
    row = lax.broadcasted_iota(jnp.int32, (c, c), 0)
    col = lax.broadcasted_iota(jnp.int32, (c, c), 1)
    lower = row >= col
    for ci in range(rows // c):
        sl = slice(ci * c, (ci + 1) * c)
        vf = v_ref[sl, :].astype(F32)
        mu = jnp.mean(vf, axis=-1, keepdims=True)
        xc = vf - mu
        y = xc * lax.rsqrt(jnp.mean(xc * xc, axis=-1, keepdims=True) + EPS)
        vn_ref[...] = (y * vg_ref[...]).astype(BF16)
        for g in range(GM_GROUPS):
            gs = slice(g * GM_GROUP_DIM, (g + 1) * GM_GROUP_DIM)
            ws = jnp.where(lower, ws_ref[g], 0.0).astype(BF16)
            mixed = jnp.dot(ws, vn_ref[:, gs], preferred_element_type=F32) + bs_ref[:, gs]
            o_ref[sl, gs] = (u_ref[sl, gs].astype(F32) * mixed).astype(o_ref.dtype)


def _gm_gate(uv, v_gain, w_s, b_full, *, tm=512):
    m, two_w = uv.shape
    w = two_w // 2
    c = GM_CHUNK
    vmem = (6 * _nbytes((tm, w), BF16) + 2 * _nbytes((GM_GROUPS, c, c), F32)
            + 2 * _nbytes((c, w), F32) + 8 * _nbytes((c, w), F32))
    return pl.pallas_call(
        _gm_body,
        grid=(m // tm,),
        in_specs=[
            pl.BlockSpec((tm, w), lambda i: (i, 0)),
            pl.BlockSpec((tm, w), lambda i: (i, 1)),
            pl.BlockSpec((1, w), lambda i: (0, 0)),
            pl.BlockSpec((GM_GROUPS, c, c), lambda i: (0, 0, 0)),
            pl.BlockSpec((c, w), lambda i: (0, 0)),
        ],
        out_specs=pl.BlockSpec((tm, w), lambda i: (i, 0)),
        out_shape=jax.ShapeDtypeStruct((m, w), BF16),
        scratch_shapes=[pltpu.VMEM((c, w), BF16)],
        compiler_params=_params(1, 2 * vmem),
    )(uv, uv, v_gain.reshape(1, w), w_s, b_full)


def _ret_body(q_ref, k_ref, v_ref, g_ref, cos_ref, sin_ref, intra_ref, qd_ref, kd_ref,
              cd_ref, gn_ref, o_ref, state_ref):
    c = RET_CHUNK
    half = RET_QK_DIM // 2

    @pl.when(pl.program_id(2) == 0)
    def _():
        state_ref[...] = jnp.zeros_like(state_ref)

    def rotate(x, cos, sin):
        x1, x2 = x[:, :half], x[:, half:]
        return jnp.concatenate([x1 * cos - x2 * sin, x1 * sin + x2 * cos], axis=1)

    for ci in range(q_ref.shape[0] // c):
        sl = slice(ci * c, (ci + 1) * c)
        cos = cos_ref[sl, :]
        sin = sin_ref[sl, :]
        qr = rotate(q_ref[sl, :].astype(F32), cos, sin)
        kr = rotate(k_ref[sl, :].astype(F32), cos, sin) * (RET_QK_DIM ** -0.5)
        qb = qr.astype(BF16)
        kb = kr.astype(BF16)
        vb = v_ref[sl, :]
        scores = lax.dot_general(qb, kb, (((1,), (1,)), ((), ())),
                                 preferred_element_type=F32) * intra_ref[...]
        inner = jnp.dot(scores.astype(BF16), vb, preferred_element_type=F32)
        state = state_ref[...]
        cross = jnp.dot(qb, state.astype(BF16), preferred_element_type=F32) * qd_ref[...]
        kdec = (kr * kd_ref[...]).astype(BF16)
        state_ref[...] = state * cd_ref[...] + lax.dot_general(
            kdec, vb, (((0,), (0,)), ((), ())), preferred_element_type=F32)
        o = inner + cross
        y = _rms_rows(o, gn_ref[...])
        o_ref[sl, :] = (jax.nn.silu(g_ref[sl, :].astype(F32)) * y).astype(o_ref.dtype)


def _retention(qkvg, gn_gain, *, tc=1024):
    b, s, _ = qkvg.shape
    hq, hv, c = RET_QK_DIM, RET_V_DIM, RET_CHUNK
    nh = RET_HEADS
    half = hq // 2
    inv = ROPE_BASE ** (-jnp.linspace(0.0, 1.0, half, dtype=F32))
    ang = jnp.arange(s).astype(F32)[:, None] * inv[None, :]
    cos, sin = jnp.cos(ang), jnp.sin(ang)
    log_gamma = jnp.log(1.0 - 2.0 ** (-5.0 - jnp.arange(nh, dtype=F32)))
    idx = jnp.arange(c, dtype=F32)
    diff = idx[:, None] - idx[None, :]
    intra = jnp.where(diff[None] >= 0,
                      jnp.exp(jnp.maximum(diff, 0.0)[None] * log_gamma[:, None, None]), 0.0)
    qd = jnp.exp((idx + 1.0)[None, :] * log_gamma[:, None])[:, :, None]
    kd = jnp.exp((c - 1.0 - idx)[None, :] * log_gamma[:, None])[:, :, None]
    cd = jnp.broadcast_to(jnp.exp(c * log_gamma)[:, None, None], (nh, 1, hv))

    vmem = (4 * _nbytes((tc, hq), BF16) + 6 * _nbytes((tc, hv), BF16)
            + 4 * _nbytes((tc, half), F32) + 2 * _nbytes((c, c), F32)
            + 4 * _nbytes((c, 128), F32) + 3 * _nbytes((hq, hv), F32)
            + 10 * _nbytes((c, hv), F32))
    return pl.pallas_call(
        _ret_body,
        grid=(b, nh, s // tc),
        in_specs=[
            pl.BlockSpec((None, tc, hq), lambda bi, h, ci: (bi, ci, h)),
            pl.BlockSpec((None, tc, hq), lambda bi, h, ci: (bi, ci, nh + h)),
            pl.BlockSpec((None, tc, hv), lambda bi, h, ci: (bi, ci, nh + h)),
            pl.BlockSpec((None, tc, hv), lambda bi, h, ci: (bi, ci, 2 * nh + h)),
            pl.BlockSpec((tc, half), lambda bi, h, ci: (ci, 0)),
            pl.BlockSpec((tc, half), lambda bi, h, ci: (ci, 0)),
            pl.BlockSpec((None, c, c), lambda bi, h, ci: (h, 0, 0)),
            pl.BlockSpec((None, c, 1), lambda bi, h, ci: (h, 0, 0)),
            pl.BlockSpec((None, c, 1), lambda bi, h, ci: (h, 0, 0)),
            pl.BlockSpec((None, 1, hv), lambda bi, h, ci: (h, 0, 0)),
            pl.BlockSpec((1, hv), lambda bi, h, ci: (0, h)),
        ],
        out_specs=pl.BlockSpec((None, tc, hv), lambda bi, h, ci: (bi, ci, h)),
        out_shape=jax.ShapeDtypeStruct((b, s, nh * hv), BF16),
        scratch_shapes=[pltpu.VMEM((hq, hv), F32)],
        compiler_params=_params(3, 2 * vmem),
    )(qkvg, qkvg, qkvg, qkvg, cos, sin, intra, qd, kd, cd, gn_gain.reshape(1, nh * hv))


def kernel(x, sb_norm, sb_w_qkv, sb_w_o, gm_norm, gm_w_in, gm_v_norm, gm_w_s, gm_b_s, gm_w_o,
           ret_norm, ret_w_qkvg, ret_gn, ret_w_o, ffn_norm, ffn_w_gate_up, ffn_w_down,
           final_norm):
    b, s, d = x.shape
    n = b * s
    depth = ffn_norm.shape[0]
    mixer_stacks = ((sb_w_qkv, sb_w_o), (gm_w_in, gm_w_o), (ret_w_qkvg, ret_w_o))
    q_scale = jnp.concatenate([
        jnp.full((d,), (SB_HEAD_DIM ** -0.5) * LOG2_E, F32), jnp.ones((2 * d,), F32)])

    def layer_weights(i):
        kind, j = i % N_MIXERS, i // N_MIXERS
        return ((mixer_stacks[kind][0], j), (mixer_stacks[kind][1], j),
                (ffn_w_gate_up, i), (ffn_w_down, i))

    def project_in(i, h, w_in, round_jobs):
        kind, j = i % N_MIXERS, i // N_MIXERS
        if kind == 0:
            return _matmul(h, w_in, gain=sb_norm[j], col_scale=q_scale,
                           round_jobs=round_jobs, out_dtype=BF16)
        if kind == 1:
            return _matmul(h, w_in, gain=gm_norm[j], act="gelu", round_jobs=round_jobs,
                           out_dtype=BF16)
        return _matmul(h, w_in, gain=ret_norm[j], round_jobs=round_jobs, out_dtype=BF16)

    def mix(i, proj, w_out, h):
        kind, j = i % N_MIXERS, i // N_MIXERS
        if kind == 0:
            o = _sb_attention(proj.reshape(b, s, 3 * d))
            return _matmul(o.reshape(n, d), w_out, residual=h, out_dtype=F32)
        if kind == 1:
            b_full = jnp.repeat(gm_b_s[j].T, GM_GROUP_DIM, axis=1)
            gated = _gm_gate(proj, gm_v_norm[j], gm_w_s[j], b_full)
            return _matmul(gated, w_out, residual=h, out_dtype=F32)
        o = _retention(proj.reshape(b, s, 6 * d), ret_gn[j])
        return _matmul(o.reshape(n, 2 * d), w_out, residual=h, out_dtype=F32, tn=512)

    stack, layer = layer_weights(0)[0]
    w_in = stack[layer].astype(BF16)
    h = x.reshape(n, d)
    proj, (w_out, w_gate_up, w_down) = project_in(0, h, w_in, layer_weights(0)[1:])
    for i in range(depth):
        if i > 0:
            proj = project_in(i, h, w_in, ())
        h = mix(i, proj, w_out, h)
        if i == depth - 1:
            h = _ffn(h, ffn_norm[i], w_gate_up, w_down, final_gain=final_norm)
        else:
            hidden, (w_in, w_out, next_gate_up, next_down) = _matmul(
                h, w_gate_up, gain=ffn_norm[i], act="swiglu",
                round_jobs=layer_weights(i + 1), out_dtype=BF16, tn=512)
            h = _matmul(hidden, w_down, residual=h, out_dtype=F32, tn=512)
            w_gate_up, w_down = next_gate_up, next_down
    return h.reshape(b, s, d)
```

```python
import functools

import jax
import jax.numpy as jnp
from jax import lax
from jax.experimental import pallas as pl
from jax.experimental.pallas import tpu as pltpu

F32 = jnp.float32
BF16 = jnp.bfloat16

EPS = 1e-6
N_MIXERS = 3
SB_HEADS = 16
SB_HEAD_DIM = 128
SB_TILE = 256
SB_HEADS_PER_STEP = 4
LOG2_E = 1.4426950408889634
SB_UNDERFLOW_BITS = 160.0
GM_GROUPS = 16
GM_GROUP_DIM = 128
GM_CHUNK = 128
RET_HEADS = 8
RET_QK_DIM = 256
RET_V_DIM = 512
RET_CHUNK = 128
ROPE_BASE = 10000.0

V7X_VMEM_LIMIT_BYTES = 60000 * 1024
NORM_ROWS = 256
BF16_SUBLANES = 16


def _nbytes(shape, dtype):
    n = 1
    for s in shape:
        n *= s
    return n * jnp.dtype(dtype).itemsize


def _params(n_axes, vmem_bytes):
    return pltpu.CompilerParams(
        dimension_semantics=("arbitrary",) * n_axes,
        vmem_limit_bytes=int(min(vmem_bytes, V7X_VMEM_LIMIT_BYTES)),
    )


def _rms_rows(xf, gain):
    y = xf * lax.rsqrt(jnp.mean(xf * xf, axis=-1, keepdims=True) + EPS)
    return y * gain


def _norm_into(x_ref, g_ref, xn_ref):
    rows = x_ref.shape[0]
    step = min(NORM_ROWS, rows)

    def body(r, carry):
        sl = pl.ds(pl.multiple_of(r * step, step), step)
        xn_ref[sl, :] = _rms_rows(x_ref[sl, :], g_ref[...]).astype(xn_ref.dtype)
        return carry

    lax.fori_loop(0, rows // step, body, 0)


def _mm_body(*refs, norm, act, has_res, has_scale, n_jobs):
    it = iter(refs)
    x_ref = next(it)
    g_ref = next(it) if norm else None
    w_ref = next(it)
    wu_ref = next(it) if act == "swiglu" else None
    s_ref = next(it) if has_scale else None
    r_ref = next(it) if has_res else None
    raw_refs = [next(it) for _ in range(n_jobs)]
    o_ref = next(it)
    rounded_refs = [next(it) for _ in range(n_jobs)]
    xn_ref = next(it) if norm else None

    for raw_ref, rounded_ref in zip(raw_refs, rounded_refs):
        rounded_ref[...] = raw_ref[...].astype(rounded_ref.dtype)

    if norm:
        @pl.when(pl.program_id(1) == 0)
        def _():
            _norm_into(x_ref, g_ref, xn_ref)
        lhs = xn_ref[...]
    else:
        lhs = x_ref[...]
    acc = jnp.dot(lhs, w_ref[...], preferred_element_type=F32)
    if act == "gelu":
        acc = jax.nn.gelu(acc)
    if act == "swiglu":
        acc = jax.nn.silu(acc) * jnp.dot(lhs, wu_ref[...], preferred_element_type=F32)
    if has_scale:
        acc = acc * s_ref[...]
    if has_res:
        acc = acc + r_ref[...]
    o_ref[...] = acc.astype(o_ref.dtype)


def _rows_per_step(rows, steps):
    for rb in range(BF16_SUBLANES, rows + 1, BF16_SUBLANES):
        if rows % rb == 0 and rows // rb <= steps:
            return rb
    raise ValueError((rows, steps))


def _matmul(x, w, *, gain=None, residual=None, act=None, col_scale=None, round_jobs=(),
            out_dtype, tm=1024, tn=1024):
    m, k = x.shape
    n = w.shape[-1]
    if act == "swiglu":
        n //= 2
    nj = n // tn
    steps = (m // tm) * nj
    norm = gain is not None
    has_res = residual is not None
    has_scale = col_scale is not None
    in_specs = [pl.BlockSpec((tm, k), lambda i, j: (i, 0))]
    args = [x]
    if norm:
        in_specs.append(pl.BlockSpec((1, k), lambda i, j: (0, 0)))
        args.append(gain.reshape(1, k))
    in_specs.append(pl.BlockSpec((k, tn), lambda i, j: (0, j)))
    args.append(w)
    if act == "swiglu":
        in_specs.append(pl.BlockSpec((k, tn), lambda i, j: (0, nj + j)))
        args.append(w)
    if has_scale:
        in_specs.append(pl.BlockSpec((1, tn), lambda i, j: (0, j)))
        args.append(col_scale.reshape(1, n))
    if has_res:
        in_specs.append(pl.BlockSpec((tm, tn), lambda i, j: (i, j)))
        args.append(residual)
    out_specs = [pl.BlockSpec((tm, tn), lambda i, j: (i, j))]
    out_shape = [jax.ShapeDtypeStruct((m, n), out_dtype)]
    scratch = [pltpu.VMEM((tm, k), BF16)] if norm else []
    n_w = 2 if act == "swiglu" else 1
    vmem = (2 * _nbytes((tm, k), x.dtype) + 2 * n_w * _nbytes((k, tn), BF16)
            + 2 * _nbytes((tm, tn), out_dtype) + 2 * n_w * _nbytes((tm, tn), F32)
            + (2 * _nbytes((tm, tn), F32) if has_res else 0)
            + (_nbytes((tm, k), BF16) + 4 * _nbytes((NORM_ROWS, k), F32) if norm else 0))
    for stack, layer in round_jobs:
        _, rows, cols = stack.shape
        rb = _rows_per_step(rows, steps)
        last = rows // rb - 1
        in_specs.append(pl.BlockSpec(
            (None, rb, cols),
            lambda i, j, layer=layer, last=last: (layer, jnp.minimum(i * nj + j, last), 0)))
        args.append(stack)
        out_specs.append(pl.BlockSpec(
            (rb, cols), lambda i, j, last=last: (jnp.minimum(i * nj + j, last), 0)))
        out_shape.append(jax.ShapeDtypeStruct((rows, cols), BF16))
        vmem += 2 * _nbytes((rb, cols), F32) + 3 * _nbytes((rb, cols), BF16)
    outs = pl.pallas_call(
        functools.partial(_mm_body, norm=norm, act=act, has_res=has_res,
                          has_scale=has_scale, n_jobs=len(round_jobs)),
        grid=(m // tm, nj),
        in_specs=in_specs,
        out_specs=out_specs,
        out_shape=out_shape,
        scratch_shapes=scratch,
        compiler_params=_params(2, vmem),
    )(*args)
    return (outs[0], list(outs[1:])) if round_jobs else outs[0]


def _ffn_body(*refs, final):
    if final:
        h_ref, g_ref, wg_ref, wu_ref, wd_ref, fg_ref, o_ref, xn_ref = refs
    else:
        h_ref, g_ref, wg_ref, wu_ref, wd_ref, o_ref, xn_ref = refs
        fg_ref = None
    f = pl.program_id(1)

    @pl.when(f == 0)
    def _():
        _norm_into(h_ref, g_ref, xn_ref)
        o_ref[...] = h_ref[...]

    xn = xn_ref[...]
    gate = jnp.dot(xn, wg_ref[...], preferred_element_type=F32)
    up = jnp.dot(xn, wu_ref[...], preferred_element_type=F32)
    act = (jax.nn.silu(gate) * up).astype(BF16)
    o_ref[...] += jnp.dot(act, wd_ref[...], preferred_element_type=F32)

    if final:
        @pl.when(f == pl.num_programs(1) - 1)
        def _():
            rows = o_ref.shape[0]
            step = min(NORM_ROWS, rows)

            def body(r, carry):
                sl = pl.ds(pl.multiple_of(r * step, step), step)
                o_ref[sl, :] = _rms_rows(o_ref[sl, :], fg_ref[...])
                return carry

            lax.fori_loop(0, rows // step, body, 0)


def _ffn(h, gain, w_gate_up, w_down, *, final_gain=None, tm=512, tf=512):
    m, d = h.shape
    ff = w_down.shape[0]
    nf = ff // tf
    final = final_gain is not None
    in_specs = [
        pl.BlockSpec((tm, d), lambda i, f: (i, 0)),
        pl.BlockSpec((1, d), lambda i, f: (0, 0)),
        pl.BlockSpec((d, tf), lambda i, f: (0, f)),
        pl.BlockSpec((d, tf), lambda i, f: (0, nf + f)),
        pl.BlockSpec((tf, d), lambda i, f: (f, 0)),
    ]
    args = [h, gain.reshape(1, d), w_gate_up, w_gate_up, w_down]
    if final:
        in_specs.append(pl.BlockSpec((1, d), lambda i, f: (0, 0)))
        args.append(final_gain.reshape(1, d))
    vmem = (4 * _nbytes((tm, d), F32) + _nbytes((tm, d), BF16)
            + 6 * _nbytes((d, tf), BF16) + 3 * _nbytes((tm, tf), F32)
            + 2 * _nbytes((tm, d), F32) + 4 * _nbytes((NORM_ROWS, d), F32))
    return pl.pallas_call(
        functools.partial(_ffn_body, final=final),
        grid=(m // tm, nf),
        in_specs=in_specs,
        out_specs=pl.BlockSpec((tm, d), lambda i, f: (i, 0)),
        out_shape=jax.ShapeDtypeStruct((m, d), F32),
        scratch_shapes=[pltpu.VMEM((tm, d), BF16)],
        compiler_params=_params(2, vmem),
    )(*args)


def _sb_body(q_ref, k_ref, v_ref, o_ref):
    t = SB_TILE
    hd = SB_HEAD_DIM
    heads = range(SB_HEADS_PER_STEP)
    hs = [slice(hh * hd, (hh + 1) * hd) for hh in heads]
    sign_bit = jnp.int32(-2 ** 31)
    row = lax.broadcasted_iota(jnp.int32, (t, t), 0)
    col = lax.broadcasted_iota(jnp.int32, (t, t), 1)
    after = (row > col).astype(BF16)
    causal = col < row

    def keys(kb):
        return pl.ds(pl.multiple_of(kb * t, t), t)

    def score(i, kb):
        return [lax.dot_general(q_ref[keys(i), hs[hh]], k_ref[keys(kb), hs[hh]],
                                (((1,), (1,)), ((), ())), preferred_element_type=F32)
                for hh in heads]

    def keep_stage(w, masked):
        neg_abs = lax.bitcast_convert_type(
            lax.bitcast_convert_type(w, jnp.int32) | sign_bit, F32)
        keep = jnp.maximum(w, 0.0) + jnp.log2(1.0 + jnp.exp2(neg_abs))
        if masked:
            keep = jnp.where(causal, keep, 0.0)
        tail = jnp.dot(keep.astype(BF16), after, preferred_element_type=F32)
        return w - keep, tail, tail[:, :1] + keep[:, :1]

    def weigh(kb, hh, own, tail, run, acc, masked):
        a = jnp.exp2(own - tail - run)
        if masked:
            a = jnp.where(causal, a, 0.0)
        return acc + jnp.dot(a.astype(BF16), v_ref[keys(kb), hs[hh]],
                             preferred_element_type=F32)

    def least(runs):
        m = runs[0]
        for r in runs[1:]:
            m = jnp.minimum(m, r)
        return jnp.min(m)

    def live(carry):
        kb, low, _, _ = carry
        return jnp.logical_and(kb >= 0, low < SB_UNDERFLOW_BITS)

    def query_tile(i, carry):
        has_prev = i > 0
        prev = jnp.maximum(i - 1, 0)
        ws_d = score(i, i)
        ws_p = score(i, prev)
        st_d = [keep_stage(ws_d[hh], True) for hh in heads]
        st_p = [keep_stage(ws_p[hh], False) for hh in heads]
        acc_d = [weigh(i, hh, st_d[hh][0], st_d[hh][1], jnp.zeros((t, 1), F32),
                       jnp.zeros((t, hd), F32), True) for hh in heads]
        acc_p = [weigh(prev, hh, st_p[hh][0], st_p[hh][1], st_d[hh][2], acc_d[hh], False)
                 for hh in heads]
        accs = tuple(jnp.where(has_prev, acc_p[hh], acc_d[hh]) for hh in heads)
        runs = tuple(st_d[hh][2] + jnp.where(has_prev, st_p[hh][2], 0.0) for hh in heads)

        def step(state):
            kb, _, runs, accs = state
            ws = score(i, kb)
            stages = [keep_stage(ws[hh], False) for hh in heads]
            accs = tuple(weigh(kb, hh, stages[hh][0], stages[hh][1], runs[hh], accs[hh],
                               False) for hh in heads)
            runs = tuple(runs[hh] + stages[hh][2] for hh in heads)
            return kb - 1, least(runs), runs, accs

        _, _, _, accs = lax.while_loop(live, step, (i - 2, least(runs), runs, accs))
        for hh in heads:
            o_ref[keys(i), hs[hh]] = accs[hh].astype(o_ref.dtype)
        return carry

    lax.fori_loop(0, q_ref.shape[0] // t, query_tile, 0)


def _sb_attention(qkv):
    b, s, _ = qkv.shape
    t = SB_TILE
    wd = SB_HEAD_DIM * SB_HEADS_PER_STEP
    groups = SB_HEADS // SB_HEADS_PER_STEP
    vmem = (8 * _nbytes((s, wd), BF16) + 2 * 4 * SB_HEADS_PER_STEP * _nbytes((t, t), F32))
    return pl.pallas_call(
        _sb_body,
        grid=(b, groups),
        in_specs=[
            pl.BlockSpec((None, s, wd), lambda bi, h: (bi, 0, h)),
            pl.BlockSpec((None, s, wd), lambda bi, h: (bi, 0, groups + h)),
            pl.BlockSpec((None, s, wd), lambda bi, h: (bi, 0, 2 * groups + h)),
        ],
        out_specs=pl.BlockSpec((None, s, wd), lambda bi, h: (bi, 0, h)),
        out_shape=jax.ShapeDtypeStruct((b, s, SB_HEADS * SB_HEAD_DIM), BF16),
        compiler_params=_params(2, vmem),
    )(qkv, qkv, qkv)


def _gm_body(u_ref, v_ref, vg_ref, ws_ref, bs_ref, o_ref, vn_ref):
    rows = u_ref.shape[0]
    c = GM_CHUNK
    row = lax.broadcasted_iota(jnp.int32, (c, c), 0)
    col = lax.broadcasted_iota(jnp.int32, (c, c), 1)
    lower = row >= col
    for ci in range(rows // c):
        sl = slice(ci * c, (ci + 1) * c)
        vf = v_ref[sl, :].astype(F32)
        mu = jnp.mean(vf, axis=-1, keepdims=True)
        xc = vf - mu
        y = xc * lax.rsqrt(jnp.mean(xc * xc, axis=-1, keepdims=True) + EPS)
        vn_ref[...] = (y * vg_ref[...]).astype(BF16)
        for g in range(GM_GROUPS):
            gs = slice(g * GM_GROUP_DIM, (g + 1) * GM_GROUP_DIM)
            ws = jnp.where(lower, ws_ref[g], 0.0).astype(BF16)
            mixed = jnp.dot(ws, vn_ref[:, gs], preferred_element_type=F32) + bs_ref[:, gs]
            o_ref[sl, gs] = (u_ref[sl, gs].astype(F32) * mixed).astype(o_ref.dtype)


def _gm_gate(uv, v_gain, w_s, b_full, *, tm=512):
    m, two_w = uv.shape
    w = two_w // 2
    c = GM_CHUNK
    vmem = (6 * _nbytes((tm, w), BF16) + 2 * _nbytes((GM_GROUPS, c, c), F32)
            + 2 * _nbytes((c, w), F32) + 8 * _nbytes((c, w), F32))
    return pl.pallas_call(
        _gm_body,
        grid=(m // tm,),
        in_specs=[
            pl.BlockSpec((tm, w), lambda i: (i, 0)),
            pl.BlockSpec((tm, w), lambda i: (i, 1)),
            pl.BlockSpec((1, w), lambda i: (0, 0)),
            pl.BlockSpec((GM_GROUPS, c, c), lambda i: (0, 0, 0)),
            pl.BlockSpec((c, w), lambda i: (0, 0)),
        ],
        out_specs=pl.BlockSpec((tm, w), lambda i: (i, 0)),
        out_shape=jax.ShapeDtypeStruct((m, w), BF16),
        scratch_shapes=[pltpu.VMEM((c, w), BF16)],
        compiler_params=_params(1, vmem),
    )(uv, uv, v_gain.reshape(1, w), w_s, b_full)


def _ret_body(q_ref, k_ref, v_ref, g_ref, cos_ref, sin_ref, intra_ref, qd_ref, kd_ref,
              cd_ref, gn_ref, o_ref, state_ref):
    c = RET_CHUNK
    half = RET_QK_DIM // 2

    @pl.when(pl.program_id(2) == 0)
    def _():
        state_ref[...] = jnp.zeros_like(state_ref)

    def rotate(x, cos, sin):
        x1, x2 = x[:, :half], x[:, half:]
        return jnp.concatenate([x1 * cos - x2 * sin, x1 * sin + x2 * cos], axis=1)

    for ci in range(q_ref.shape[0] // c):
        sl = slice(ci * c, (ci + 1) * c)
        cos = cos_ref[sl, :]
        sin = sin_ref[sl, :]
        qr = rotate(q_ref[sl, :].astype(F32), cos, sin)
        kr = rotate(k_ref[sl, :].astype(F32), cos, sin) * (RET_QK_DIM ** -0.5)
        qb = qr.astype(BF16)
        kb = kr.astype(BF16)
        vb = v_ref[sl, :]
        scores = lax.dot_general(qb, kb, (((1,), (1,)), ((), ())),
                                 preferred_element_type=F32) * intra_ref[...]
        inner = jnp.dot(scores.astype(BF16), vb, preferred_element_type=F32)
        state = state_ref[...]
        cross = jnp.dot(qb, state.astype(BF16), preferred_element_type=F32) * qd_ref[...]
        kdec = (kr * kd_ref[...]).astype(BF16)
        state_ref[...] = state * cd_ref[...] + lax.dot_general(
            kdec, vb, (((0,), (0,)), ((), ())), preferred_element_type=F32)
        o = inner + cross
        y = _rms_rows(o, gn_ref[...])
        o_ref[sl, :] = (jax.nn.silu(g_ref[sl, :].astype(F32)) * y).astype(o_ref.dtype)


def _retention(qkvg, gn_gain, *, tc=1024):
    b, s, _ = qkvg.shape
    hq, hv, c = RET_QK_DIM, RET_V_DIM, RET_CHUNK
    nh = RET_HEADS
    half = hq // 2
    inv = ROPE_BASE ** (-jnp.linspace(0.0, 1.0, half, dtype=F32))
    ang = jnp.arange(s).astype(F32)[:, None] * inv[None, :]
    cos, sin = jnp.cos(ang), jnp.sin(ang)
    log_gamma = jnp.log(1.0 - 2.0 ** (-5.0 - jnp.arange(nh, dtype=F32)))
    idx = jnp.arange(c, dtype=F32)
    diff = idx[:, None] - idx[None, :]
    intra = jnp.where(diff[None] >= 0,
                      jnp.exp(jnp.maximum(diff, 0.0)[None] * log_gamma[:, None, None]), 0.0)
    qd = jnp.exp((idx + 1.0)[None, :] * log_gamma[:, None])[:, :, None]
    kd = jnp.exp((c - 1.0 - idx)[None, :] * log_gamma[:, None])[:, :, None]
    cd = jnp.broadcast_to(jnp.exp(c * log_gamma)[:, None, None], (nh, 1, hv))

    vmem = (4 * _nbytes((tc, hq), BF16) + 6 * _nbytes((tc, hv), BF16)
            + 4 * _nbytes((tc, half), F32) + 2 * _nbytes((c, c), F32)
            + 4 * _nbytes((c, 128), F32) + 3 * _nbytes((hq, hv), F32)
            + 10 * _nbytes((c, hv), F32))
    return pl.pallas_call(
        _ret_body,
        grid=(b, nh, s // tc),
        in_specs=[
            pl.BlockSpec((None, tc, hq), lambda bi, h, ci: (bi, ci, h)),
            pl.BlockSpec((None, tc, hq), lambda bi, h, ci: (bi, ci, nh + h)),
            pl.BlockSpec((None, tc, hv), lambda bi, h, ci: (bi, ci, nh + h)),
            pl.BlockSpec((None, tc, hv), lambda bi, h, ci: (bi, ci, 2 * nh + h)),
            pl.BlockSpec((tc, half), lambda bi, h, ci: (ci, 0)),
            pl.BlockSpec((tc, half), lambda bi, h, ci: (ci, 0)),
            pl.BlockSpec((None, c, c), lambda bi, h, ci: (h, 0, 0)),
            pl.BlockSpec((None, c, 1), lambda bi, h, ci: (h, 0, 0)),
            pl.BlockSpec((None, c, 1), lambda bi, h, ci: (h, 0, 0)),
            pl.BlockSpec((None, 1, hv), lambda bi, h, ci: (h, 0, 0)),
            pl.BlockSpec((1, hv), lambda bi, h, ci: (0, h)),
        ],
        out_specs=pl.BlockSpec((None, tc, hv), lambda bi, h, ci: (bi, ci, h)),
        out_shape=jax.ShapeDtypeStruct((b, s, nh * hv), BF16),
        scratch_shapes=[pltpu.VMEM((hq, hv), F32)],
        compiler_params=_params(3, vmem),
    )(qkvg, qkvg, qkvg, qkvg, cos, sin, intra, qd, kd, cd, gn_gain.reshape(1, nh * hv))


def kernel(x, sb_norm, sb_w_qkv, sb_w_o, gm_norm, gm_w_in, gm_v_norm, gm_w_s, gm_b_s, gm_w_o,
           ret_norm, ret_w_qkvg, ret_gn, ret_w_o, ffn_norm, ffn_w_gate_up, ffn_w_down,
           final_norm):
    b, s, d = x.shape
    n = b * s
    depth = ffn_norm.shape[0]
    mixer_stacks = ((sb_w_qkv, sb_w_o), (gm_w_in, gm_w_o), (ret_w_qkvg, ret_w_o))
    q_scale = jnp.concatenate([
        jnp.full((d,), (SB_HEAD_DIM ** -0.5) * LOG2_E, F32), jnp.ones((2 * d,), F32)])

    def layer_weights(i):
        kind, j = i % N_MIXERS, i // N_MIXERS
        return ((mixer_stacks[kind][0], j), (mixer_stacks[kind][1], j),
                (ffn_w_gate_up, i), (ffn_w_down, i))

    def project_in(i, h, w_in, round_jobs):
        kind, j = i % N_MIXERS, i // N_MIXERS
        if kind == 0:
            return _matmul(h, w_in, gain=sb_norm[j], col_scale=q_scale,
                           round_jobs=round_jobs, out_dtype=BF16, tn=1536)
        if kind == 1:
            return _matmul(h, w_in, gain=gm_norm[j], act="gelu", round_jobs=round_jobs,
                           out_dtype=BF16)
        return _matmul(h, w_in, gain=ret_norm[j], round_jobs=round_jobs, out_dtype=BF16,
                       tn=1536)

    def mix(i, proj, w_out, h):
        kind, j = i % N_MIXERS, i // N_MIXERS
        if kind == 0:
            o = _sb_attention(proj.reshape(b, s, 3 * d))
            return _matmul(o.reshape(n, d), w_out, residual=h, out_dtype=F32)
        if kind == 1:
            b_full = jnp.repeat(gm_b_s[j].T, GM_GROUP_DIM, axis=1)
            gated = _gm_gate(proj, gm_v_norm[j], gm_w_s[j], b_full)
            return _matmul(gated, w_out, residual=h, out_dtype=F32)
        o = _retention(proj.reshape(b, s, 6 * d), ret_gn[j])
        return _matmul(o.reshape(n, 2 * d), w_out, residual=h, out_dtype=F32, tn=512)

    stack, layer = layer_weights(0)[0]
    w_in = stack[layer].astype(BF16)
    h = x.reshape(n, d)
    proj, (w_out, w_gate_up, w_down) = project_in(0, h, w_in, layer_weights(0)[1:])
    for i in range(depth):
        if i > 0:
            proj = project_in(i, h, w_in, ())
        h = mix(i, proj, w_out, h)
        if i == depth - 1:
            h = _ffn(h, ffn_norm[i], w_gate_up, w_down, final_gain=final_norm)
        else:
            hidden, (w_in, w_out, next_gate_up, next_down) = _matmul(
                h, w_gate_up, gain=ffn_norm[i], act="swiglu",
                round_jobs=layer_weights(i + 1), out_dtype=BF16, tn=512)
            h = _matmul(hidden, w_down, residual=h, out_dtype=F32, tn=512)
            w_gate_up, w_down = next_gate_up, next_down
    return h.reshape(b, s, d)
```

```python
import functools

import jax
import jax.numpy as jnp
from jax import lax
from jax.experimental import pallas as pl
from jax.experimental.pallas import tpu as pltpu

F32 = jnp.float32
BF16 = jnp.bfloat16

EPS = 1e-6
N_MIXERS = 3
SB_HEADS = 16
SB_HEAD_DIM = 128
SB_TILE = 256
SB_HEADS_PER_STEP = 4
LOG2_E = 1.4426950408889634
SB_UNDERFLOW_BITS = 160.0
GM_GROUPS = 16
GM_GROUP_DIM = 128
GM_CHUNK = 128
RET_HEADS = 8
RET_QK_DIM = 256
RET_V_DIM = 512
RET_CHUNK = 128
ROPE_BASE = 10000.0

V7X_VMEM_LIMIT_BYTES = 60000 * 1024
NORM_ROWS = 256
BF16_SUBLANES = 16


def _nbytes(shape, dtype):
    n = 1
    for s in shape:
        n *= s
    return n * jnp.dtype(dtype).itemsize


def _params(n_axes, vmem_bytes):
    return pltpu.CompilerParams(
        dimension_semantics=("arbitrary",) * n_axes,
        vmem_limit_bytes=int(min(vmem_bytes, V7X_VMEM_LIMIT_BYTES)),
    )


def _rms_rows(xf, gain):
    y = xf * lax.rsqrt(jnp.mean(xf * xf, axis=-1, keepdims=True) + EPS)
    return y * gain


def _norm_into(x_ref, g_ref, xn_ref):
    rows = x_ref.shape[0]
    step = min(NORM_ROWS, rows)
    for r in range(0, rows, step):
        xn_ref[r:r + step, :] = _rms_rows(x_ref[r:r + step, :], g_ref[...]).astype(xn_ref.dtype)


def _mm_body(*refs, norm, act, has_res, has_scale, n_jobs):
    it = iter(refs)
    x_ref = next(it)
    g_ref = next(it) if norm else None
    w_ref = next(it)
    wu_ref = next(it) if act == "swiglu" else None
    s_ref = next(it) if has_scale else None
    r_ref = next(it) if has_res else None
    raw_refs = [next(it) for _ in range(n_jobs)]
    o_ref = next(it)
    rounded_refs = [next(it) for _ in range(n_jobs)]
    xn_ref = next(it) if norm else None

    for raw_ref, rounded_ref in zip(raw_refs, rounded_refs):
        rounded_ref[...] = raw_ref[...].astype(rounded_ref.dtype)

    def project():
        lhs = xn_ref[...] if norm else x_ref[...]
        acc = jnp.dot(lhs, w_ref[...], preferred_element_type=F32)
        if act == "gelu":
            acc = jax.nn.gelu(acc)
        if act == "swiglu":
            acc = jax.nn.silu(acc) * jnp.dot(lhs, wu_ref[...], preferred_element_type=F32)
        if has_scale:
            acc = acc * s_ref[...]
        if has_res:
            acc = acc + r_ref[...]
        o_ref[...] = acc.astype(o_ref.dtype)

    if not norm:
        project()
        return

    first = pl.program_id(1) == 0

    @pl.when(first)
    def _():
        _norm_into(x_ref, g_ref, xn_ref)
        project()

    @pl.when(jnp.logical_not(first))
    def _():
        project()


def _rows_per_step(rows, steps):
    for rb in range(BF16_SUBLANES, rows + 1, BF16_SUBLANES):
        if rows % rb == 0 and rows // rb <= steps:
            return rb
    raise ValueError((rows, steps))


def _matmul(x, w, *, gain=None, residual=None, act=None, col_scale=None, round_jobs=(),
            out_dtype, tm=1024, tn=1024):
    m, k = x.shape
    n = w.shape[-1]
    if act == "swiglu":
        n //= 2
    nj = n // tn
    steps = (m // tm) * nj
    norm = gain is not None
    has_res = residual is not None
    has_scale = col_scale is not None
    in_specs = [pl.BlockSpec((tm, k), lambda i, j: (i, 0))]
    args = [x]
    if norm:
        in_specs.append(pl.BlockSpec((1, k), lambda i, j: (0, 0)))
        args.append(gain.reshape(1, k))
    in_specs.append(pl.BlockSpec((k, tn), lambda i, j: (0, j)))
    args.append(w)
    if act == "swiglu":
        in_specs.append(pl.BlockSpec((k, tn), lambda i, j: (0, nj + j)))
        args.append(w)
    if has_scale:
        in_specs.append(pl.BlockSpec((1, tn), lambda i, j: (0, j)))
        args.append(col_scale.reshape(1, n))
    if has_res:
        in_specs.append(pl.BlockSpec((tm, tn), lambda i, j: (i, j)))
        args.append(residual)
    out_specs = [pl.BlockSpec((tm, tn), lambda i, j: (i, j))]
    out_shape = [jax.ShapeDtypeStruct((m, n), out_dtype)]
    scratch = [pltpu.VMEM((tm, k), BF16)] if norm else []
    n_w = 2 if act == "swiglu" else 1
    vmem = (2 * _nbytes((tm, k), x.dtype) + 2 * n_w * _nbytes((k, tn), BF16)
            + 2 * _nbytes((tm, tn), out_dtype) + 2 * n_w * _nbytes((tm, tn), F32)
            + (2 * _nbytes((tm, tn), F32) if has_res else 0)
            + (_nbytes((tm, k), BF16) + 4 * _nbytes((NORM_ROWS, k), F32) if norm else 0))
    for stack, layer in round_jobs:
        _, rows, cols = stack.shape
        rb = _rows_per_step(rows, steps)
        last = rows // rb - 1
        in_specs.append(pl.BlockSpec(
            (None, rb, cols),
            lambda i, j, layer=layer, last=last: (layer, jnp.minimum(i * nj + j, last), 0)))
        args.append(stack)
        out_specs.append(pl.BlockSpec(
            (rb, cols), lambda i, j, last=last: (jnp.minimum(i * nj + j, last), 0)))
        out_shape.append(jax.ShapeDtypeStruct((rows, cols), BF16))
        vmem += 2 * _nbytes((rb, cols), F32) + 3 * _nbytes((rb, cols), BF16)
    outs = pl.pallas_call(
        functools.partial(_mm_body, norm=norm, act=act, has_res=has_res,
                          has_scale=has_scale, n_jobs=len(round_jobs)),
        grid=(m // tm, nj),
        in_specs=in_specs,
        out_specs=out_specs,
        out_shape=out_shape,
        scratch_shapes=scratch,
        compiler_params=_params(2, vmem),
    )(*args)
    return (outs[0], list(outs[1:])) if round_jobs else outs[0]


def _ffn_body(*refs, final):
    if final:
        h_ref, g_ref, wg_ref, wu_ref, wd_ref, fg_ref, o_ref, xn_ref = refs
    else:
        h_ref, g_ref, wg_ref, wu_ref, wd_ref, o_ref, xn_ref = refs
        fg_ref = None
    f = pl.program_id(1)

    def hidden():
        xn = xn_ref[...]
        gate = jnp.dot(xn, wg_ref[...], preferred_element_type=F32)
        up = jnp.dot(xn, wu_ref[...], preferred_element_type=F32)
        return (jax.nn.silu(gate) * up).astype(BF16)

    @pl.when(f == 0)
    def _():
        _norm_into(h_ref, g_ref, xn_ref)
        o_ref[...] = h_ref[...] + jnp.dot(hidden(), wd_ref[...], preferred_element_type=F32)

    @pl.when(f != 0)
    def _():
        o_ref[...] += jnp.dot(hidden(), wd_ref[...], preferred_element_type=F32)

    if final:
        @pl.when(f == pl.num_programs(1) - 1)
        def _():
            rows = o_ref.shape[0]
            step = min(NORM_ROWS, rows)

            def body(r, carry):
                sl = pl.ds(pl.multiple_of(r * step, step), step)
                o_ref[sl, :] = _rms_rows(o_ref[sl, :], fg_ref[...])
                return carry

            lax.fori_loop(0, rows // step, body, 0)


def _ffn(h, gain, w_gate_up, w_down, *, final_gain=None, tm=512, tf=512):
    m, d = h.shape
    ff = w_down.shape[0]
    nf = ff // tf
    final = final_gain is not None
    in_specs = [
        pl.BlockSpec((tm, d), lambda i, f: (i, 0)),
        pl.BlockSpec((1, d), lambda i, f: (0, 0)),
        pl.BlockSpec((d, tf), lambda i, f: (0, f)),
        pl.BlockSpec((d, tf), lambda i, f: (0, nf + f)),
        pl.BlockSpec((tf, d), lambda i, f: (f, 0)),
    ]
    args = [h, gain.reshape(1, d), w_gate_up, w_gate_up, w_down]
    if final:
        in_specs.append(pl.BlockSpec((1, d), lambda i, f: (0, 0)))
        args.append(final_gain.reshape(1, d))
    vmem = (4 * _nbytes((tm, d), F32) + _nbytes((tm, d), BF16)
            + 6 * _nbytes((d, tf), BF16) + 3 * _nbytes((tm, tf), F32)
            + 2 * _nbytes((tm, d), F32) + 4 * _nbytes((NORM_ROWS, d), F32))
    return pl.pallas_call(
        functools.partial(_ffn_body, final=final),
        grid=(m // tm, nf),
        in_specs=in_specs,
        out_specs=pl.BlockSpec((tm, d), lambda i, f: (i, 0)),
        out_shape=jax.ShapeDtypeStruct((m, d), F32),
        scratch_shapes=[pltpu.VMEM((tm, d), BF16)],
        compiler_params=_params(2, vmem),
    )(*args)


def _sb_body(q_ref, k_ref, v_ref, o_ref):
    t = SB_TILE
    hd = SB_HEAD_DIM
    heads = range(SB_HEADS_PER_STEP)
    hs = [slice(hh * hd, (hh + 1) * hd) for hh in heads]
    sign_bit = jnp.int32(-2 ** 31)
    row = lax.broadcasted_iota(jnp.int32, (t, t), 0)
    col = lax.broadcasted_iota(jnp.int32, (t, t), 1)
    after = (row > col).astype(BF16)
    causal = col < row

    def keys(kb):
        return pl.ds(pl.multiple_of(kb * t, t), t)

    def score(i, kb):
        return [lax.dot_general(q_ref[keys(i), hs[hh]], k_ref[keys(kb), hs[hh]],
                                (((1,), (1,)), ((), ())), preferred_element_type=F32)
                for hh in heads]

    def keep_stage(w, masked):
        neg_abs = lax.bitcast_convert_type(
            lax.bitcast_convert_type(w, jnp.int32) | sign_bit, F32)
        keep = jnp.maximum(w, 0.0) + jnp.log2(1.0 + jnp.exp2(neg_abs))
        if masked:
            keep = jnp.where(causal, keep, 0.0)
        tail = jnp.dot(keep.astype(BF16), after, preferred_element_type=F32)
        return w - keep, tail, tail[:, :1] + keep[:, :1]

    def weigh(kb, hh, own, tail, run, acc, masked):
        a = jnp.exp2(own - tail - run)
        if masked:
            a = jnp.where(causal, a, 0.0)
        return acc + jnp.dot(a.astype(BF16), v_ref[keys(kb), hs[hh]],
                             preferred_element_type=F32)

    def least(runs):
        m = runs[0]
        for r in runs[1:]:
            m = jnp.minimum(m, r)
        return jnp.min(m)

    def live(carry):
        kb, low, _, _ = carry
        return jnp.logical_and(kb >= 0, low < SB_UNDERFLOW_BITS)

    def query_tile(i, carry):
        has_prev = i > 0
        prev = jnp.maximum(i - 1, 0)
        ws_d = score(i, i)
        ws_p = score(i, prev)
        st_d = [keep_stage(ws_d[hh], True) for hh in heads]
        st_p = [keep_stage(ws_p[hh], False) for hh in heads]
        acc_d = [weigh(i, hh, st_d[hh][0], st_d[hh][1], jnp.zeros((t, 1), F32),
                       jnp.zeros((t, hd), F32), True) for hh in heads]
        acc_p = [weigh(prev, hh, st_p[hh][0], st_p[hh][1], st_d[hh][2], acc_d[hh], False)
                 for hh in heads]
        accs = tuple(jnp.where(has_prev, acc_p[hh], acc_d[hh]) for hh in heads)
        runs = tuple(st_d[hh][2] + jnp.where(has_prev, st_p[hh][2], 0.0) for hh in heads)

        def step(state):
            kb, _, runs, accs = state
            ws = score(i, kb)
            stages = [keep_stage(ws[hh], False) for hh in heads]
            accs = tuple(weigh(kb, hh, stages[hh][0], stages[hh][1], runs[hh], accs[hh],
                               False) for hh in heads)
            runs = tuple(runs[hh] + stages[hh][2] for hh in heads)
            return kb - 1, least(runs), runs, accs

        _, _, _, accs = lax.while_loop(live, step, (i - 2, least(runs), runs, accs))
        for hh in heads:
            o_ref[keys(i), hs[hh]] = accs[hh].astype(o_ref.dtype)
        return carry

    lax.fori_loop(0, q_ref.shape[0] // t, query_tile, 0)


def _sb_attention(qkv):
    b, s, _ = qkv.shape
    t = SB_TILE
    wd = SB_HEAD_DIM * SB_HEADS_PER_STEP
    groups = SB_HEADS // SB_HEADS_PER_STEP
    vmem = (8 * _nbytes((s, wd), BF16) + 2 * 4 * SB_HEADS_PER_STEP * _nbytes((t, t), F32))
    return pl.pallas_call(
        _sb_body,
        grid=(b, groups),
        in_specs=[
            pl.BlockSpec((None, s, wd), lambda bi, h: (bi, 0, h)),
            pl.BlockSpec((None, s, wd), lambda bi, h: (bi, 0, groups + h)),
            pl.BlockSpec((None, s, wd), lambda bi, h: (bi, 0, 2 * groups + h)),
        ],
        out_specs=pl.BlockSpec((None, s, wd), lambda bi, h: (bi, 0, h)),
        out_shape=jax.ShapeDtypeStruct((b, s, SB_HEADS * SB_HEAD_DIM), BF16),
        compiler_params=_params(2, vmem),
    )(qkv, qkv, qkv)


def _gm_body(u_ref, v_ref, vg_ref, ws_ref, bs_ref, o_ref, vn_ref):
    rows = u_ref.shape[0]
    c = GM_CHUNK
    row = lax.broadcasted_iota(jnp.int32, (c, c), 0)
    col = lax.broadcasted_iota(jnp.int32, (c, c), 1)
    lower = row >= col
    for ci in range(rows // c):
        sl = slice(ci * c, (ci + 1) * c)
        vf = v_ref[sl, :].astype(F32)
        mu = jnp.mean(vf, axis=-1, keepdims=True)
        xc = vf - mu
        y = xc * lax.rsqrt(jnp.mean(xc * xc, axis=-1, keepdims=True) + EPS)
        vn_ref[...] = (y * vg_ref[...]).astype(BF16)
        for g in range(GM_GROUPS):
            gs = slice(g * GM_GROUP_DIM, (g + 1) * GM_GROUP_DIM)
            ws = jnp.where(lower, ws_ref[g], 0.0).astype(BF16)
            mixed = jnp.dot(ws, vn_ref[:, gs], preferred_element_type=F32) + bs_ref[:, gs]
            o_ref[sl, gs] = (u_ref[sl, gs].astype(F32) * mixed).astype(o_ref.dtype)


def _gm_gate(uv, v_gain, w_s, b_full, *, tm=512):
    m, two_w = uv.shape
    w = two_w // 2
    c = GM_CHUNK
    vmem = (6 * _nbytes((tm, w), BF16) + 2 * _nbytes((GM_GROUPS, c, c), F32)
            + 2 * _nbytes((c, w), F32) + 8 * _nbytes((c, w), F32))
    return pl.pallas_call(
        _gm_body,
        grid=(m // tm,),
        in_specs=[
            pl.BlockSpec((tm, w), lambda i: (i, 0)),
            pl.BlockSpec((tm, w), lambda i: (i, 1)),
            pl.BlockSpec((1, w), lambda i: (0, 0)),
            pl.BlockSpec((GM_GROUPS, c, c), lambda i: (0, 0, 0)),
            pl.BlockSpec((c, w), lambda i: (0, 0)),
        ],
        out_specs=pl.BlockSpec((tm, w), lambda i: (i, 0)),
        out_shape=jax.ShapeDtypeStruct((m, w), BF16),
        scratch_shapes=[pltpu.VMEM((c, w), BF16)],
        compiler_params=_params(1, vmem),
    )(uv, uv, v_gain.reshape(1, w), w_s, b_full)


def _ret_body(q_ref, k_ref, v_ref, g_ref, cos_ref, sin_ref, intra_ref, qd_ref, kd_ref,
              cd_ref, gn_ref, o_ref, state_ref):
    c = RET_CHUNK
    half = RET_QK_DIM // 2

    @pl.when(pl.program_id(2) == 0)
    def _():
        state_ref[...] = jnp.zeros_like(state_ref)

    def rotate(x, cos, sin):
        x1, x2 = x[:, :half], x[:, half:]
        return jnp.concatenate([x1 * cos - x2 * sin, x1 * sin + x2 * cos], axis=1)

    for ci in range(q_ref.shape[0] // c):
        sl = slice(ci * c, (ci + 1) * c)
        cos = cos_ref[sl, :]
        sin = sin_ref[sl, :]
        qr = rotate(q_ref[sl, :].astype(F32), cos, sin)
        kr = rotate(k_ref[sl, :].astype(F32), cos, sin) * (RET_QK_DIM ** -0.5)
        qb = qr.astype(BF16)
        kb = kr.astype(BF16)
        vb = v_ref[sl, :]
        scores = lax.dot_general(qb, kb, (((1,), (1,)), ((), ())),
                                 preferred_element_type=F32) * intra_ref[...]
        inner = jnp.dot(scores.astype(BF16), vb, preferred_element_type=F32)
        state = state_ref[...]
        cross = jnp.dot(qb, state.astype(BF16), preferred_element_type=F32) * qd_ref[...]
        kdec = (kr * kd_ref[...]).astype(BF16)
        state_ref[...] = state * cd_ref[...] + lax.dot_general(
            kdec, vb, (((0,), (0,)), ((), ())), preferred_element_type=F32)
        o = inner + cross
        y = _rms_rows(o, gn_ref[...])
        o_ref[sl, :] = (jax.nn.silu(g_ref[sl, :].astype(F32)) * y).astype(o_ref.dtype)


def _retention(qkvg, gn_gain, *, tc=1024):
    b, s, _ = qkvg.shape
    hq, hv, c = RET_QK_DIM, RET_V_DIM, RET_CHUNK
    nh = RET_HEADS
    half = hq // 2
    inv = ROPE_BASE ** (-jnp.linspace(0.0, 1.0, half, dtype=F32))
    ang = jnp.arange(s).astype(F32)[:, None] * inv[None, :]
    cos, sin = jnp.cos(ang), jnp.sin(ang)
    log_gamma = jnp.log(1.0 - 2.0 ** (-5.0 - jnp.arange(nh, dtype=F32)))
    idx = jnp.arange(c, dtype=F32)
    diff = idx[:, None] - idx[None, :]
    intra = jnp.where(diff[None] >= 0,
                      jnp.exp(jnp.maximum(diff, 0.0)[None] * log_gamma[:, None, None]), 0.0)
    qd = jnp.exp((idx + 1.0)[None, :] * log_gamma[:, None])[:, :, None]
    kd = jnp.exp((c - 1.0 - idx)[None, :] * log_gamma[:, None])[:, :, None]
    cd = jnp.broadcast_to(jnp.exp(c * log_gamma)[:, None, None], (nh, 1, hv))

    vmem = (4 * _nbytes((tc, hq), BF16) + 6 * _nbytes((tc, hv), BF16)
            + 4 * _nbytes((tc, half), F32) + 2 * _nbytes((c, c), F32)
            + 4 * _nbytes((c, 128), F32) + 3 * _nbytes((hq, hv), F32)
            + 10 * _nbytes((c, hv), F32))
    return pl.pallas_call(
        _ret_body,
        grid=(b, nh, s // tc),
        in_specs=[
            pl.BlockSpec((None, tc, hq), lambda bi, h, ci: (bi, ci, h)),
            pl.BlockSpec((None, tc, hq), lambda bi, h, ci: (bi, ci, nh + h)),
            pl.BlockSpec((None, tc, hv), lambda bi, h, ci: (bi, ci, nh + h)),
            pl.BlockSpec((None, tc, hv), lambda bi, h, ci: (bi, ci, 2 * nh + h)),
            pl.BlockSpec((tc, half), lambda bi, h, ci: (ci, 0)),
            pl.BlockSpec((tc, half), lambda bi, h, ci: (ci, 0)),
            pl.BlockSpec((None, c, c), lambda bi, h, ci: (h, 0, 0)),
            pl.BlockSpec((None, c, 1), lambda bi, h, ci: (h, 0, 0)),
            pl.BlockSpec((None, c, 1), lambda bi, h, ci: (h, 0, 0)),
            pl.BlockSpec((None, 1, hv), lambda bi, h, ci: (h, 0, 0)),
            pl.BlockSpec((1, hv), lambda bi, h, ci: (0, h)),
        ],
        out_specs=pl.BlockSpec((None, tc, hv), lambda bi, h, ci: (bi, ci, h)),
        out_shape=jax.ShapeDtypeStruct((b, s, nh * hv), BF16),
        scratch_shapes=[pltpu.VMEM((hq, hv), F32)],
        compiler_params=_params(3, vmem),
    )(qkvg, qkvg, qkvg, qkvg, cos, sin, intra, qd, kd, cd, gn_gain.reshape(1, nh * hv))


def kernel(x, sb_norm, sb_w_qkv, sb_w_o, gm_norm, gm_w_in, gm_v_norm, gm_w_s, gm_b_s, gm_w_o,
           ret_norm, ret_w_qkvg, ret_gn, ret_w_o, ffn_norm, ffn_w_gate_up, ffn_w_down,
           final_norm):
    b, s, d = x.shape
    n = b * s
    depth = ffn_norm.shape[0]
    mixer_stacks = ((sb_w_qkv, sb_w_o), (gm_w_in, gm_w_o), (ret_w_qkvg, ret_w_o))
    q_scale = jnp.concatenate([
        jnp.full((d,), (SB_HEAD_DIM ** -0.5) * LOG2_E, F32), jnp.ones((2 * d,), F32)])

    def layer_weights(i):
        kind, j = i % N_MIXERS, i // N_MIXERS
        return ((mixer_stacks[kind][0], j), (mixer_stacks[kind][1], j),
                (ffn_w_gate_up, i), (ffn_w_down, i))

    def project_in(i, h, w_in, round_jobs):
        kind, j = i % N_MIXERS, i // N_MIXERS
        if kind == 0:
            return _matmul(h, w_in, gain=sb_norm[j], col_scale=q_scale,
                           round_jobs=round_jobs, out_dtype=BF16, tn=1536)
        if kind == 1:
            return _matmul(h, w_in, gain=gm_norm[j], act="gelu", round_jobs=round_jobs,
                           out_dtype=BF16)
        return _matmul(h, w_in, gain=ret_norm[j], round_jobs=round_jobs, out_dtype=BF16,
                       tn=1536)

    def mix(i, proj, w_out, h):
        kind, j = i % N_MIXERS, i // N_MIXERS
        if kind == 0:
            o = _sb_attention(proj.reshape(b, s, 3 * d))
            return _matmul(o.reshape(n, d), w_out, residual=h, out_dtype=F32)
        if kind == 1:
            b_full = jnp.repeat(gm_b_s[j].T, GM_GROUP_DIM, axis=1)
            gated = _gm_gate(proj, gm_v_norm[j], gm_w_s[j], b_full)
            return _matmul(gated, w_out, residual=h, out_dtype=F32)
        o = _retention(proj.reshape(b, s, 6 * d), ret_gn[j])
        return _matmul(o.reshape(n, 2 * d), w_out, residual=h, out_dtype=F32, tn=512)

    stack, layer = layer_weights(0)[0]
    w_in = stack[layer].astype(BF16)
    h = x.reshape(n, d)
    proj, (w_out, w_gate_up, w_down) = project_in(0, h, w_in, layer_weights(0)[1:])
    for i in range(depth):
        if i > 0:
            proj = project_in(i, h, w_in, ())
        h = mix(i, proj, w_out, h)
        if i == depth - 1:
            h = _ffn(h, ffn_norm[i], w_gate_up, w_down, final_gain=final_norm)
        else:
            hidden, (w_in, w_out, next_gate_up, next_down) = _matmul(
                h, w_gate_up, gain=ffn_norm[i], act="swiglu",
                round_jobs=layer_weights(i + 1), out_dtype=BF16, tn=512)
            h = _matmul(hidden, w_down, residual=h, out_dtype=F32, tn=512)
            w_gate_up, w_down = next_gate_up, next_down
    return h.reshape(b, s, d)
```

```python
import functools

import jax
import jax.numpy as jnp
from jax import lax
from jax.experimental import pallas as pl
from jax.experimental.pallas import tpu as pltpu

F32 = jnp.float32
BF16 = jnp.bfloat16

EPS = 1e-6
N_MIXERS = 3
SB_HEADS = 16
SB_HEAD_DIM = 128
SB_TILE = 256
SB_HEADS_PER_STEP = 4
LOG2_E = 1.4426950408889634
SB_UNDERFLOW_BITS = 160.0
GM_GROUPS = 16
GM_GROUP_DIM = 128
GM_CHUNK = 128
RET_HEADS = 8
RET_QK_DIM = 256
RET_V_DIM = 512
RET_CHUNK = 128
ROPE_BASE = 10000.0

V7X_VMEM_LIMIT_BYTES = 60000 * 1024
NORM_ROWS = 256
BF16_SUBLANES = 16


def _nbytes(shape, dtype):
    n = 1
    for s in shape:
        n *= s
    return n * jnp.dtype(dtype).itemsize


def _params(n_axes, vmem_bytes):
    return pltpu.CompilerParams(
        dimension_semantics=("arbitrary",) * n_axes,
        vmem_limit_bytes=int(min(vmem_bytes, V7X_VMEM_LIMIT_BYTES)),
    )


def _rms_rows(xf, gain):
    y = xf * lax.rsqrt(jnp.mean(xf * xf, axis=-1, keepdims=True) + EPS)
    return y * gain


def _norm_into(x_ref, g_ref, xn_ref):
    rows = x_ref.shape[0]
    step = min(NORM_ROWS, rows)
    for r in range(0, rows, step):
        xn_ref[r:r + step, :] = _rms_rows(x_ref[r:r + step, :], g_ref[...]).astype(xn_ref.dtype)


def _mm_body(*refs, norm, act, has_res, has_scale, n_jobs):
    it = iter(refs)
    x_ref = next(it)
    g_ref = next(it) if norm else None
    w_ref = next(it)
    wu_ref = next(it) if act == "swiglu" else None
    s_ref = next(it) if has_scale else None
    r_ref = next(it) if has_res else None
    raw_refs = [next(it) for _ in range(n_jobs)]
    o_ref = next(it)
    rounded_refs = [next(it) for _ in range(n_jobs)]
    xn_ref = next(it) if norm else None

    def project():
        for raw_ref, rounded_ref in zip(raw_refs, rounded_refs):
            rounded_ref[...] = raw_ref[...].astype(rounded_ref.dtype)
        lhs = xn_ref[...] if norm else x_ref[...]
        acc = jnp.dot(lhs, w_ref[...], preferred_element_type=F32)
        if act == "gelu":
            acc = jax.nn.gelu(acc)
        if act == "swiglu":
            acc = jax.nn.silu(acc) * jnp.dot(lhs, wu_ref[...], preferred_element_type=F32)
        if has_scale:
            acc = acc * s_ref[...]
        if has_res:
            acc = acc + r_ref[...]
        o_ref[...] = acc.astype(o_ref.dtype)

    if not norm:
        project()
        return

    first = pl.program_id(1) == 0

    @pl.when(first)
    def _():
        _norm_into(x_ref, g_ref, xn_ref)
        project()

    @pl.when(jnp.logical_not(first))
    def _():
        project()


def _rows_per_step(rows, steps):
    for rb in range(BF16_SUBLANES, rows + 1, BF16_SUBLANES):
        if rows % rb == 0 and rows // rb <= steps:
            return rb
    raise ValueError((rows, steps))


def _matmul(x, w, *, gain=None, residual=None, act=None, col_scale=None, round_jobs=(),
            out_dtype, tm=1024, tn=1024):
    m, k = x.shape
    n = w.shape[-1]
    if act == "swiglu":
        n //= 2
    nj = n // tn
    steps = (m // tm) * nj
    norm = gain is not None
    has_res = residual is not None
    has_scale = col_scale is not None
    in_specs = [pl.BlockSpec((tm, k), lambda i, j: (i, 0))]
    args = [x]
    if norm:
        in_specs.append(pl.BlockSpec((1, k), lambda i, j: (0, 0)))
        args.append(gain.reshape(1, k))
    in_specs.append(pl.BlockSpec((k, tn), lambda i, j: (0, j)))
    args.append(w)
    if act == "swiglu":
        in_specs.append(pl.BlockSpec((k, tn), lambda i, j: (0, nj + j)))
        args.append(w)
    if has_scale:
        in_specs.append(pl.BlockSpec((1, tn), lambda i, j: (0, j)))
        args.append(col_scale.reshape(1, n))
    if has_res:
        in_specs.append(pl.BlockSpec((tm, tn), lambda i, j: (i, j)))
        args.append(residual)
    out_specs = [pl.BlockSpec((tm, tn), lambda i, j: (i, j))]
    out_shape = [jax.ShapeDtypeStruct((m, n), out_dtype)]
    scratch = [pltpu.VMEM((tm, k), BF16)] if norm else []
    n_w = 2 if act == "swiglu" else 1
    vmem = (2 * _nbytes((tm, k), x.dtype) + 2 * n_w * _nbytes((k, tn), BF16)
            + 2 * _nbytes((tm, tn), out_dtype) + 2 * n_w * _nbytes((tm, tn), F32)
            + (2 * _nbytes((tm, tn), F32) if has_res else 0)
            + (_nbytes((tm, k), BF16) + 4 * _nbytes((NORM_ROWS, k), F32) if norm else 0))
    for stack, layer in round_jobs:
        _, rows, cols = stack.shape
        rb = _rows_per_step(rows, steps)
        last = rows // rb - 1
        in_specs.append(pl.BlockSpec(
            (None, rb, cols),
            lambda i, j, layer=layer, last=last: (layer, jnp.minimum(i * nj + j, last), 0)))
        args.append(stack)
        out_specs.append(pl.BlockSpec(
            (rb, cols), lambda i, j, last=last: (jnp.minimum(i * nj + j, last), 0)))
        out_shape.append(jax.ShapeDtypeStruct((rows, cols), BF16))
        vmem += 2 * _nbytes((rb, cols), F32) + 3 * _nbytes((rb, cols), BF16)
    outs = pl.pallas_call(
        functools.partial(_mm_body, norm=norm, act=act, has_res=has_res,
                          has_scale=has_scale, n_jobs=len(round_jobs)),
        grid=(m // tm, nj),
        in_specs=in_specs,
        out_specs=out_specs,
        out_shape=out_shape,
        scratch_shapes=scratch,
        compiler_params=_params(2, vmem),
    )(*args)
    return (outs[0], list(outs[1:])) if round_jobs else outs[0]


def _ffn_body(*refs, final):
    if final:
        h_ref, g_ref, wg_ref, wu_ref, wd_ref, fg_ref, o_ref, xn_ref = refs
    else:
        h_ref, g_ref, wg_ref, wu_ref, wd_ref, o_ref, xn_ref = refs
        fg_ref = None
    f = pl.program_id(1)

    def hidden():
        xn = xn_ref[...]
        gate = jnp.dot(xn, wg_ref[...], preferred_element_type=F32)
        up = jnp.dot(xn, wu_ref[...], preferred_element_type=F32)
        return (jax.nn.silu(gate) * up).astype(BF16)

    @pl.when(f == 0)
    def _():
        _norm_into(h_ref, g_ref, xn_ref)
        o_ref[...] = h_ref[...] + jnp.dot(hidden(), wd_ref[...], preferred_element_type=F32)

    @pl.when(f != 0)
    def _():
        o_ref[...] += jnp.dot(hidden(), wd_ref[...], preferred_element_type=F32)

    if final:
        @pl.when(f == pl.num_programs(1) - 1)
        def _():
            rows = o_ref.shape[0]
            step = min(NORM_ROWS, rows)

            def body(r, carry):
                sl = pl.ds(pl.multiple_of(r * step, step), step)
                o_ref[sl, :] = _rms_rows(o_ref[sl, :], fg_ref[...])
                return carry

            lax.fori_loop(0, rows // step, body, 0)


def _ffn(h, gain, w_gate_up, w_down, *, final_gain=None, tm=512, tf=512):
    m, d = h.shape
    ff = w_down.shape[0]
    nf = ff // tf
    final = final_gain is not None
    in_specs = [
        pl.BlockSpec((tm, d), lambda i, f: (i, 0)),
        pl.BlockSpec((1, d), lambda i, f: (0, 0)),
        pl.BlockSpec((d, tf), lambda i, f: (0, f)),
        pl.BlockSpec((d, tf), lambda i, f: (0, nf + f)),
        pl.BlockSpec((tf, d), lambda i, f: (f, 0)),
    ]
    args = [h, gain.reshape(1, d), w_gate_up, w_gate_up, w_down]
    if final:
        in_specs.append(pl.BlockSpec((1, d), lambda i, f: (0, 0)))
        args.append(final_gain.reshape(1, d))
    vmem = (4 * _nbytes((tm, d), F32) + _nbytes((tm, d), BF16)
            + 6 * _nbytes((d, tf), BF16) + 3 * _nbytes((tm, tf), F32)
            + 2 * _nbytes((tm, d), F32) + 4 * _nbytes((NORM_ROWS, d), F32))
    return pl.pallas_call(
        functools.partial(_ffn_body, final=final),
        grid=(m // tm, nf),
        in_specs=in_specs,
        out_specs=pl.BlockSpec((tm, d), lambda i, f: (i, 0)),
        out_shape=jax.ShapeDtypeStruct((m, d), F32),
        scratch_shapes=[pltpu.VMEM((tm, d), BF16)],
        compiler_params=_params(2, vmem),
    )(*args)


def _sb_body(q_ref, k_ref, v_ref, o_ref):
    t = SB_TILE
    hd = SB_HEAD_DIM
    heads = range(SB_HEADS_PER_STEP)
    hs = [slice(hh * hd, (hh + 1) * hd) for hh in heads]
    sign_bit = jnp.int32(-2 ** 31)
    row = lax.broadcasted_iota(jnp.int32, (t, t), 0)
    col = lax.broadcasted_iota(jnp.int32, (t, t), 1)
    after = (row > col).astype(BF16)
    causal = col < row

    def keys(kb):
        return pl.ds(pl.multiple_of(kb * t, t), t)

    def score(i, kb):
        return [lax.dot_general(q_ref[keys(i), hs[hh]], k_ref[keys(kb), hs[hh]],
                                (((1,), (1,)), ((), ())), preferred_element_type=F32)
                for hh in heads]

    def keep_stage(w, masked):
        neg_abs = lax.bitcast_convert_type(
            lax.bitcast_convert_type(w, jnp.int32) | sign_bit, F32)
        keep = jnp.maximum(w, 0.0) + jnp.log2(1.0 + jnp.exp2(neg_abs))
        if masked:
            keep = jnp.where(causal, keep, 0.0)
        tail = jnp.dot(keep.astype(BF16), after, preferred_element_type=F32)
        return w - keep, tail, tail[:, :1] + keep[:, :1]

    def weigh(kb, hh, own, tail, run, acc, masked):
        a = jnp.exp2(own - tail - run)
        if masked:
            a = jnp.where(causal, a, 0.0)
        return acc + jnp.dot(a.astype(BF16), v_ref[keys(kb), hs[hh]],
                             preferred_element_type=F32)

    def least(runs):
        m = runs[0]
        for r in runs[1:]:
            m = jnp.minimum(m, r)
        return jnp.min(m)

    def live(carry):
        kb, low, _, _ = carry
        return jnp.logical_and(kb >= 0, low < SB_UNDERFLOW_BITS)

    def query_tile(i, carry):
        has_prev = i > 0
        prev = jnp.maximum(i - 1, 0)
        ws_d = score(i, i)
        ws_p = score(i, prev)
        st_d = [keep_stage(ws_d[hh], True) for hh in heads]
        st_p = [keep_stage(ws_p[hh], False) for hh in heads]
        acc_d = [weigh(i, hh, st_d[hh][0], st_d[hh][1], jnp.zeros((t, 1), F32),
                       jnp.zeros((t, hd), F32), True) for hh in heads]
        acc_p = [weigh(prev, hh, st_p[hh][0], st_p[hh][1], st_d[hh][2], acc_d[hh], False)
                 for hh in heads]
        accs = tuple(jnp.where(has_prev, acc_p[hh], acc_d[hh]) for hh in heads)
        runs = tuple(st_d[hh][2] + jnp.where(has_prev, st_p[hh][2], 0.0) for hh in heads)

        def step(state):
            kb, _, runs, accs = state
            ws = score(i, kb)
            stages = [keep_stage(ws[hh], False) for hh in heads]
            accs = tuple(weigh(kb, hh, stages[hh][0], stages[hh][1], runs[hh], accs[hh],
                               False) for hh in heads)
            runs = tuple(runs[hh] + stages[hh][2] for hh in heads)
            return kb - 1, least(runs), runs, accs

        _, _, _, accs = lax.while_loop(live, step, (i - 2, least(runs), runs, accs))
        for hh in heads:
            o_ref[keys(i), hs[hh]] = accs[hh].astype(o_ref.dtype)
        return carry

    lax.fori_loop(0, q_ref.shape[0] // t, query_tile, 0)


def _sb_attention(qkv):
    b, s, _ = qkv.shape
    t = SB_TILE
    wd = SB_HEAD_DIM * SB_HEADS_PER_STEP
    groups = SB_HEADS // SB_HEADS_PER_STEP
    vmem = (8 * _nbytes((s, wd), BF16) + 2 * 4 * SB_HEADS_PER_STEP * _nbytes((t, t), F32))
    return pl.pallas_call(
        _sb_body,
        grid=(b, groups),
        in_specs=[
            pl.BlockSpec((None, s, wd), lambda bi, h: (bi, 0, h)),
            pl.BlockSpec((None, s, wd), lambda bi, h: (bi, 0, groups + h)),
            pl.BlockSpec((None, s, wd), lambda bi, h: (bi, 0, 2 * groups + h)),
        ],
        out_specs=pl.BlockSpec((None, s, wd), lambda bi, h: (bi, 0, h)),
        out_shape=jax.ShapeDtypeStruct((b, s, SB_HEADS * SB_HEAD_DIM), BF16),
        compiler_params=_params(2, vmem),
    )(qkv, qkv, qkv)


def _gm_body(u_ref, v_ref, vg_ref, ws_ref, bs_ref, o_ref, vn_ref):
    rows = u_ref.shape[0]
    c = GM_CHUNK
    row = lax.broadcasted_iota(jnp.int32, (c, c), 0)
    col = lax.broadcasted_iota(jnp.int32, (c, c), 1)
    lower = row >= col
    for ci in range(rows // c):
        sl = slice(ci * c, (ci + 1) * c)
        vf = v_ref[sl, :].astype(F32)
        mu = jnp.mean(vf, axis=-1, keepdims=True)
        xc = vf - mu
        y = xc * lax.rsqrt(jnp.mean(xc * xc, axis=-1, keepdims=True) + EPS)
        vn_ref[...] = (y * vg_ref[...]).astype(BF16)
        for g in range(GM_GROUPS):
            gs = slice(g * GM_GROUP_DIM, (g + 1) * GM_GROUP_DIM)
            ws = jnp.where(lower, ws_ref[g], 0.0).astype(BF16)
            mixed = jnp.dot(ws, vn_ref[:, gs], preferred_element_type=F32) + bs_ref[:, gs]
            o_ref[sl, gs] = (u_ref[sl, gs].astype(F32) * mixed).astype(o_ref.dtype)


def _gm_gate(uv, v_gain, w_s, b_full, *, tm=512):
    m, two_w = uv.shape
    w = two_w // 2
    c = GM_CHUNK
    vmem = (6 * _nbytes((tm, w), BF16) + 2 * _nbytes((GM_GROUPS, c, c), F32)
            + 2 * _nbytes((c, w), F32) + 8 * _nbytes((c, w), F32))
    return pl.pallas_call(
        _gm_body,
        grid=(m // tm,),
        in_specs=[
            pl.BlockSpec((tm, w), lambda i: (i, 0)),
            pl.BlockSpec((tm, w), lambda i: (i, 1)),
            pl.BlockSpec((1, w), lambda i: (0, 0)),
            pl.BlockSpec((GM_GROUPS, c, c), lambda i: (0, 0, 0)),
            pl.BlockSpec((c, w), lambda i: (0, 0)),
        ],
        out_specs=pl.BlockSpec((tm, w), lambda i: (i, 0)),
        out_shape=jax.ShapeDtypeStruct((m, w), BF16),
        scratch_shapes=[pltpu.VMEM((c, w), BF16)],
        compiler_params=_params(1, vmem),
    )(uv, uv, v_gain.reshape(1, w), w_s, b_full)


def _ret_body(q_ref, k_ref, v_ref, g_ref, cos_ref, sin_ref, intra_ref, qd_ref, kd_ref,
              cd_ref, gn_ref, o_ref, state_ref):
    c = RET_CHUNK
    half = RET_QK_DIM // 2

    @pl.when(pl.program_id(2) == 0)
    def _():
        state_ref[...] = jnp.zeros_like(state_ref)

    def rotate(x, cos, sin):
        x1, x2 = x[:, :half], x[:, half:]
        return jnp.concatenate([x1 * cos - x2 * sin, x1 * sin + x2 * cos], axis=1)

    for ci in range(q_ref.shape[0] // c):
        sl = slice(ci * c, (ci + 1) * c)
        cos = cos_ref[sl, :]
        sin = sin_ref[sl, :]
        qr = rotate(q_ref[sl, :].astype(F32), cos, sin)
        kr = rotate(k_ref[sl, :].astype(F32), cos, sin) * (RET_QK_DIM ** -0.5)
        qb = qr.astype(BF16)
        kb = kr.astype(BF16)
        vb = v_ref[sl, :]
        scores = lax.dot_general(qb, kb, (((1,), (1,)), ((), ())),
                                 preferred_element_type=F32) * intra_ref[...]
        inner = jnp.dot(scores.astype(BF16), vb, preferred_element_type=F32)
        state = state_ref[...]
        cross = jnp.dot(qb, state.astype(BF16), preferred_element_type=F32) * qd_ref[...]
        kdec = (kr * kd_ref[...]).astype(BF16)
        state_ref[...] = state * cd_ref[...] + lax.dot_general(
            kdec, vb, (((0,), (0,)), ((), ())), preferred_element_type=F32)
        o = inner + cross
        y = _rms_rows(o, gn_ref[...])
        o_ref[sl, :] = (jax.nn.silu(g_ref[sl, :].astype(F32)) * y).astype(o_ref.dtype)


def _retention(qkvg, gn_gain, *, tc=1024):
    b, s, _ = qkvg.shape
    hq, hv, c = RET_QK_DIM, RET_V_DIM, RET_CHUNK
    nh = RET_HEADS
    half = hq // 2
    inv = ROPE_BASE ** (-jnp.linspace(0.0, 1.0, half, dtype=F32))
    ang = jnp.arange(s).astype(F32)[:, None] * inv[None, :]
    cos, sin = jnp.cos(ang), jnp.sin(ang)
    log_gamma = jnp.log(1.0 - 2.0 ** (-5.0 - jnp.arange(nh, dtype=F32)))
    idx = jnp.arange(c, dtype=F32)
    diff = idx[:, None] - idx[None, :]
    intra = jnp.where(diff[None] >= 0,
                      jnp.exp(jnp.maximum(diff, 0.0)[None] * log_gamma[:, None, None]), 0.0)
    qd = jnp.exp((idx + 1.0)[None, :] * log_gamma[:, None])[:, :, None]
    kd = jnp.exp((c - 1.0 - idx)[None, :] * log_gamma[:, None])[:, :, None]
    cd = jnp.broadcast_to(jnp.exp(c * log_gamma)[:, None, None], (nh, 1, hv))

    vmem = (4 * _nbytes((tc, hq), BF16) + 6 * _nbytes((tc, hv), BF16)
            + 4 * _nbytes((tc, half), F32) + 2 * _nbytes((c, c), F32)
            + 4 * _nbytes((c, 128), F32) + 3 * _nbytes((hq, hv), F32)
            + 10 * _nbytes((c, hv), F32))
    return pl.pallas_call(
        _ret_body,
        grid=(b, nh, s // tc),
        in_specs=[
            pl.BlockSpec((None, tc, hq), lambda bi, h, ci: (bi, ci, h)),
            pl.BlockSpec((None, tc, hq), lambda bi, h, ci: (bi, ci, nh + h)),
            pl.BlockSpec((None, tc, hv), lambda bi, h, ci: (bi, ci, nh + h)),
            pl.BlockSpec((None, tc, hv), lambda bi, h, ci: (bi, ci, 2 * nh + h)),
            pl.BlockSpec((tc, half), lambda bi, h, ci: (ci, 0)),
            pl.BlockSpec((tc, half), lambda bi, h, ci: (ci, 0)),
            pl.BlockSpec((None, c, c), lambda bi, h, ci: (h, 0, 0)),
            pl.BlockSpec((None, c, 1), lambda bi, h, ci: (h, 0, 0)),
            pl.BlockSpec((None, c, 1), lambda bi, h, ci: (h, 0, 0)),
            pl.BlockSpec((None, 1, hv), lambda bi, h, ci: (h, 0, 0)),
            pl.BlockSpec((1, hv), lambda bi, h, ci: (0, h)),
        ],
        out_specs=pl.BlockSpec((None, tc, hv), lambda bi, h, ci: (bi, ci, h)),
        out_shape=jax.ShapeDtypeStruct((b, s, nh * hv), BF16),
        scratch_shapes=[pltpu.VMEM((hq, hv), F32)],
        compiler_params=_params(3, vmem),
    )(qkvg, qkvg, qkvg, qkvg, cos, sin, intra, qd, kd, cd, gn_gain.reshape(1, nh * hv))


def kernel(x, sb_norm, sb_w_qkv, sb_w_o, gm_norm, gm_w_in, gm_v_norm, gm_w_s, gm_b_s, gm_w_o,
           ret_norm, ret_w_qkvg, ret_gn, ret_w_o, ffn_norm, ffn_w_gate_up, ffn_w_down,
           final_norm):
    b, s, d = x.shape
    n = b * s
    depth = ffn_norm.shape[0]
    mixer_stacks = ((sb_w_qkv, sb_w_o), (gm_w_in, gm_w_o), (ret_w_qkvg, ret_w_o))
    q_scale = jnp.concatenate([
        jnp.full((d,), (SB_HEAD_DIM ** -0.5) * LOG2_E, F32), jnp.ones((2 * d,), F32)])

    def layer_weights(i):
        kind, j = i % N_MIXERS, i // N_MIXERS
        return ((mixer_stacks[kind][0], j), (mixer_stacks[kind][1], j),
                (ffn_w_gate_up, i), (ffn_w_down, i))

    def project_in(i, h, w_in, round_jobs):
        kind, j = i % N_MIXERS, i // N_MIXERS
        if kind == 0:
            return _matmul(h, w_in, gain=sb_norm[j], col_scale=q_scale,
                           round_jobs=round_jobs, out_dtype=BF16, tn=1536)
        if kind == 1:
            return _matmul(h, w_in, gain=gm_norm[j], act="gelu", round_jobs=round_jobs,
                           out_dtype=BF16)
        return _matmul(h, w_in, gain=ret_norm[j], round_jobs=round_jobs, out_dtype=BF16,
                       tn=1536)

    def mix(i, proj, w_out, h):
        kind, j = i % N_MIXERS, i // N_MIXERS
        if kind == 0:
            o = _sb_attention(proj.reshape(b, s, 3 * d))
            return _matmul(o.reshape(n, d), w_out, residual=h, out_dtype=F32)
        if kind == 1:
            b_full = jnp.repeat(gm_b_s[j].T, GM_GROUP_DIM, axis=1)
            gated = _gm_gate(proj, gm_v_norm[j], gm_w_s[j], b_full)
            return _matmul(gated, w_out, residual=h, out_dtype=F32)
        o = _retention(proj.reshape(b, s, 6 * d), ret_gn[j])
        return _matmul(o.reshape(n, 2 * d), w_out, residual=h, out_dtype=F32, tn=512)

    stack, layer = layer_weights(0)[0]
    w_in = stack[layer].astype(BF16)
    h = x.reshape(n, d)
    proj, (w_out, w_gate_up, w_down) = project_in(0, h, w_in, layer_weights(0)[1:])
    for i in range(depth):
        if i > 0:
            proj = project_in(i, h, w_in, ())
        h = mix(i, proj, w_out, h)
        if i == depth - 1:
            h = _ffn(h, ffn_norm[i], w_gate_up, w_down, final_gain=final_norm)
        else:
            hidden, (w_in, w_out, next_gate_up, next_down) = _matmul(
                h, w_gate_up, gain=ffn_norm[i], act="swiglu",
                round_jobs=layer_weights(i + 1), out_dtype=BF16, tn=512)
            h = _matmul(hidden, w_down, residual=h, out_dtype=F32, tn=512)
            w_gate_up, w_down = next_gate_up, next_down
    return h.reshape(b, s, d)
```

```python
import functools

import jax
import jax.numpy as jnp
from jax import lax
from jax.experimental import pallas as pl
from jax.experimental.pallas import tpu as pltpu

F32 = jnp.float32
BF16 = jnp.bfloat16

EPS = 1e-6
N_MIXERS = 3
SB_HEADS = 16
SB_HEAD_DIM = 128
SB_TILE = 256
SB_HEADS_PER_STEP = 4
LOG2_E = 1.4426950408889634
SB_UNDERFLOW_BITS = 160.0
GM_GROUPS = 16
GM_GROUP_DIM = 128
GM_CHUNK = 128
RET_HEADS = 8
RET_QK_DIM = 256
RET_V_DIM = 512
RET_CHUNK = 128
ROPE_BASE = 10000.0

V7X_VMEM_LIMIT_BYTES = 60000 * 1024
NORM_ROWS = 256
BF16_SUBLANES = 16


def _nbytes(shape, dtype):
    n = 1
    for s in shape:
        n *= s
    return n * jnp.dtype(dtype).itemsize


def _params(n_axes, vmem_bytes):
    return pltpu.CompilerParams(
        dimension_semantics=("arbitrary",) * n_axes,
        vmem_limit_bytes=int(min(vmem_bytes, V7X_VMEM_LIMIT_BYTES)),
    )


def _rms_rows(xf, gain):
    y = xf * lax.rsqrt(jnp.mean(xf * xf, axis=-1, keepdims=True) + EPS)
    return y * gain


def _norm_into(x_ref, g_ref, xn_ref):
    rows = x_ref.shape[0]
    step = min(NORM_ROWS, rows)
    for r in range(0, rows, step):
        xn_ref[r:r + step, :] = _rms_rows(x_ref[r:r + step, :], g_ref[...]).astype(xn_ref.dtype)


def _mm_body(*refs, norm, act, has_res, has_scale, n_jobs):
    it = iter(refs)
    x_ref = next(it)
    g_ref = next(it) if norm else None
    w_ref = next(it)
    wu_ref = next(it) if act == "swiglu" else None
    s_ref = next(it) if has_scale else None
    r_ref = next(it) if has_res else None
    raw_refs = [next(it) for _ in range(n_jobs)]
    o_ref = next(it)
    rounded_refs = [next(it) for _ in range(n_jobs)]
    xn_ref = next(it) if norm else None

    def project():
        for raw_ref, rounded_ref in zip(raw_refs, rounded_refs):
            rounded_ref[...] = raw_ref[...].astype(rounded_ref.dtype)
        lhs = xn_ref[...] if norm else x_ref[...]
        acc = jnp.dot(lhs, w_ref[...], preferred_element_type=F32)
        if act == "gelu":
            acc = jax.nn.gelu(acc)
        if act == "swiglu":
            acc = jax.nn.silu(acc) * jnp.dot(lhs, wu_ref[...], preferred_element_type=F32)
        if has_scale:
            acc = acc * s_ref[...]
        if has_res:
            acc = acc + r_ref[...]
        o_ref[...] = acc.astype(o_ref.dtype)

    if not norm:
        project()
        return

    first = pl.program_id(1) == 0

    @pl.when(first)
    def _():
        _norm_into(x_ref, g_ref, xn_ref)
        project()

    @pl.when(jnp.logical_not(first))
    def _():
        project()


def _rows_per_step(rows, steps):
    for rb in range(BF16_SUBLANES, rows + 1, BF16_SUBLANES):
        if rows % rb == 0 and rows // rb <= steps:
            return rb
    raise ValueError((rows, steps))


def _matmul(x, w, *, gain=None, residual=None, act=None, col_scale=None, round_jobs=(),
            out_dtype, tm=1024, tn=1024):
    m, k = x.shape
    n = w.shape[-1]
    if act == "swiglu":
        n //= 2
    assert w.shape[0] == k and m % tm == 0 and n % tn == 0, (x.shape, w.shape, tm, tn)
    nj = n // tn
    steps = (m // tm) * nj
    norm = gain is not None
    has_res = residual is not None
    has_scale = col_scale is not None
    in_specs = [pl.BlockSpec((tm, k), lambda i, j: (i, 0))]
    args = [x]
    if norm:
        in_specs.append(pl.BlockSpec((1, k), lambda i, j: (0, 0)))
        args.append(gain.reshape(1, k))
    in_specs.append(pl.BlockSpec((k, tn), lambda i, j: (0, j)))
    args.append(w)
    if act == "swiglu":
        in_specs.append(pl.BlockSpec((k, tn), lambda i, j: (0, nj + j)))
        args.append(w)
    if has_scale:
        in_specs.append(pl.BlockSpec((1, tn), lambda i, j: (0, j)))
        args.append(col_scale.reshape(1, n))
    if has_res:
        in_specs.append(pl.BlockSpec((tm, tn), lambda i, j: (i, j)))
        args.append(residual)
    out_specs = [pl.BlockSpec((tm, tn), lambda i, j: (i, j))]
    out_shape = [jax.ShapeDtypeStruct((m, n), out_dtype)]
    scratch = [pltpu.VMEM((tm, k), BF16)] if norm else []
    n_w = 2 if act == "swiglu" else 1
    vmem = (2 * _nbytes((tm, k), x.dtype) + 2 * n_w * _nbytes((k, tn), BF16)
            + 2 * _nbytes((tm, tn), out_dtype) + 2 * n_w * _nbytes((tm, tn), F32)
            + (2 * _nbytes((tm, tn), F32) if has_res else 0)
            + (_nbytes((tm, k), BF16) + 4 * _nbytes((NORM_ROWS, k), F32) if norm else 0))
    for stack, layer in round_jobs:
        _, rows, cols = stack.shape
        rb = _rows_per_step(rows, steps)
        last = rows // rb - 1
        in_specs.append(pl.BlockSpec(
            (None, rb, cols),
            lambda i, j, layer=layer, last=last: (layer, jnp.minimum(i * nj + j, last), 0)))
        args.append(stack)
        out_specs.append(pl.BlockSpec(
            (rb, cols), lambda i, j, last=last: (jnp.minimum(i * nj + j, last), 0)))
        out_shape.append(jax.ShapeDtypeStruct((rows, cols), BF16))
        vmem += 2 * _nbytes((rb, cols), F32) + 3 * _nbytes((rb, cols), BF16)
    outs = pl.pallas_call(
        functools.partial(_mm_body, norm=norm, act=act, has_res=has_res,
                          has_scale=has_scale, n_jobs=len(round_jobs)),
        grid=(m // tm, nj),
        in_specs=in_specs,
        out_specs=out_specs,
        out_shape=out_shape,
        scratch_shapes=scratch,
        compiler_params=_params(2, vmem),
    )(*args)
    return (outs[0], list(outs[1:])) if round_jobs else outs[0]


def _ffn_body(*refs, final):
    if final:
        h_ref, g_ref, wg_ref, wu_ref, wd_ref, fg_ref, o_ref, xn_ref = refs
    else:
        h_ref, g_ref, wg_ref, wu_ref, wd_ref, o_ref, xn_ref = refs
        fg_ref = None
    f = pl.program_id(1)

    def hidden():
        xn = xn_ref[...]
        gate = jnp.dot(xn, wg_ref[...], preferred_element_type=F32)
        up = jnp.dot(xn, wu_ref[...], preferred_element_type=F32)
        return (jax.nn.silu(gate) * up).astype(BF16)

    @pl.when(f == 0)
    def _():
        _norm_into(h_ref, g_ref, xn_ref)
        o_ref[...] = h_ref[...] + jnp.dot(hidden(), wd_ref[...], preferred_element_type=F32)

    @pl.when(f != 0)
    def _():
        o_ref[...] += jnp.dot(hidden(), wd_ref[...], preferred_element_type=F32)

    if final:
        @pl.when(f == pl.num_programs(1) - 1)
        def _():
            rows = o_ref.shape[0]
            step = min(NORM_ROWS, rows)

            def body(r, carry):
                sl = pl.ds(pl.multiple_of(r * step, step), step)
                o_ref[sl, :] = _rms_rows(o_ref[sl, :], fg_ref[...])
                return carry

            lax.fori_loop(0, rows // step, body, 0)


def _ffn(h, gain, w_gate_up, w_down, *, final_gain=None, tm=512, tf=512):
    m, d = h.shape
    ff = w_down.shape[0]
    assert w_gate_up.shape == (d, 2 * ff) and m % tm == 0 and ff % tf == 0
    nf = ff // tf
    final = final_gain is not None
    in_specs = [
        pl.BlockSpec((tm, d), lambda i, f: (i, 0)),
        pl.BlockSpec((1, d), lambda i, f: (0, 0)),
        pl.BlockSpec((d, tf), lambda i, f: (0, f)),
        pl.BlockSpec((d, tf), lambda i, f: (0, nf + f)),
        pl.BlockSpec((tf, d), lambda i, f: (f, 0)),
    ]
    args = [h, gain.reshape(1, d), w_gate_up, w_gate_up, w_down]
    if final:
        in_specs.append(pl.BlockSpec((1, d), lambda i, f: (0, 0)))
        args.append(final_gain.reshape(1, d))
    vmem = (4 * _nbytes((tm, d), F32) + _nbytes((tm, d), BF16)
            + 6 * _nbytes((d, tf), BF16) + 3 * _nbytes((tm, tf), F32)
            + 2 * _nbytes((tm, d), F32) + 4 * _nbytes((NORM_ROWS, d), F32))
    return pl.pallas_call(
        functools.partial(_ffn_body, final=final),
        grid=(m // tm, nf),
        in_specs=in_specs,
        out_specs=pl.BlockSpec((tm, d), lambda i, f: (i, 0)),
        out_shape=jax.ShapeDtypeStruct((m, d), F32),
        scratch_shapes=[pltpu.VMEM((tm, d), BF16)],
        compiler_params=_params(2, vmem),
    )(*args)


def _sb_body(q_ref, k_ref, v_ref, o_ref):
    t = SB_TILE
    hd = SB_HEAD_DIM
    heads = range(SB_HEADS_PER_STEP)
    hs = [slice(hh * hd, (hh + 1) * hd) for hh in heads]
    row = lax.broadcasted_iota(jnp.int32, (t, t), 0)
    col = lax.broadcasted_iota(jnp.int32, (t, t), 1)
    after = (row > col).astype(BF16)
    causal = col < row

    def keys(kb):
        return pl.ds(pl.multiple_of(kb * t, t), t)

    def score(i, kb):
        return [lax.dot_general(q_ref[keys(i), hs[hh]], k_ref[keys(kb), hs[hh]],
                                (((1,), (1,)), ((), ())), preferred_element_type=F32)
                for hh in heads]

    def keep_stage(w, masked):
        keep = jnp.maximum(w, 0.0) + jnp.log2(1.0 + jnp.exp2(-jnp.abs(w)))
        if masked:
            keep = jnp.where(causal, keep, 0.0)
        tail = jnp.dot(keep.astype(BF16), after, preferred_element_type=F32)
        return w - keep, tail, tail[:, :1] + keep[:, :1]

    def weigh(kb, hh, own, tail, run, acc, masked):
        a = jnp.exp2(own - tail - run)
        if masked:
            a = jnp.where(causal, a, 0.0)
        return acc + jnp.dot(a.astype(BF16), v_ref[keys(kb), hs[hh]],
                             preferred_element_type=F32)

    def least(runs):
        m = runs[0]
        for r in runs[1:]:
            m = jnp.minimum(m, r)
        return jnp.min(m)

    def live(carry):
        kb, low, _, _ = carry
        return jnp.logical_and(kb >= 0, low < SB_UNDERFLOW_BITS)

    def query_tile(i, carry):
        has_prev = i > 0
        prev = jnp.maximum(i - 1, 0)
        ws_d = score(i, i)
        ws_p = score(i, prev)
        st_d = [keep_stage(ws_d[hh], True) for hh in heads]
        st_p = [keep_stage(ws_p[hh], False) for hh in heads]
        acc_d = [weigh(i, hh, st_d[hh][0], st_d[hh][1], jnp.zeros((t, 1), F32),
                       jnp.zeros((t, hd), F32), True) for hh in heads]
        acc_p = [weigh(prev, hh, st_p[hh][0], st_p[hh][1], st_d[hh][2], acc_d[hh], False)
                 for hh in heads]
        accs = tuple(jnp.where(has_prev, acc_p[hh], acc_d[hh]) for hh in heads)
        runs = tuple(st_d[hh][2] + jnp.where(has_prev, st_p[hh][2], 0.0) for hh in heads)

        def step(state):
            kb, _, runs, accs = state
            ws = score(i, kb)
            stages = [keep_stage(ws[hh], False) for hh in heads]
            accs = tuple(weigh(kb, hh, stages[hh][0], stages[hh][1], runs[hh], accs[hh],
                               False) for hh in heads)
            runs = tuple(runs[hh] + stages[hh][2] for hh in heads)
            return kb - 1, least(runs), runs, accs

        _, _, _, accs = lax.while_loop(live, step, (i - 2, least(runs), runs, accs))
        for hh in heads:
            o_ref[keys(i), hs[hh]] = accs[hh].astype(o_ref.dtype)
        return carry

    lax.fori_loop(0, q_ref.shape[0] // t, query_tile, 0)


def _sb_attention(qkv):
    b, s, width = qkv.shape
    t = SB_TILE
    wd = SB_HEAD_DIM * SB_HEADS_PER_STEP
    groups = SB_HEADS // SB_HEADS_PER_STEP
    assert width == 3 * SB_HEADS * SB_HEAD_DIM and s % t == 0 and SB_HEADS % SB_HEADS_PER_STEP == 0
    vmem = (8 * _nbytes((s, wd), BF16) + 2 * 4 * SB_HEADS_PER_STEP * _nbytes((t, t), F32))
    return pl.pallas_call(
        _sb_body,
        grid=(b, groups),
        in_specs=[
            pl.BlockSpec((None, s, wd), lambda bi, h: (bi, 0, h)),
            pl.BlockSpec((None, s, wd), lambda bi, h: (bi, 0, groups + h)),
            pl.BlockSpec((None, s, wd), lambda bi, h: (bi, 0, 2 * groups + h)),
        ],
        out_specs=pl.BlockSpec((None, s, wd), lambda bi, h: (bi, 0, h)),
        out_shape=jax.ShapeDtypeStruct((b, s, SB_HEADS * SB_HEAD_DIM), BF16),
        compiler_params=_params(2, vmem),
    )(qkv, qkv, qkv)


def _gm_body(u_ref, v_ref, vg_ref, ws_ref, bs_ref, o_ref, vn_ref):
    rows = u_ref.shape[0]
    c = GM_CHUNK
    row = lax.broadcasted_iota(jnp.int32, (c, c), 0)
    col = lax.broadcasted_iota(jnp.int32, (c, c), 1)
    lower = row >= col
    for ci in range(rows // c):
        sl = slice(ci * c, (ci + 1) * c)
        vf = v_ref[sl, :].astype(F32)
        mu = jnp.mean(vf, axis=-1, keepdims=True)
        xc = vf - mu
        y = xc * lax.rsqrt(jnp.mean(xc * xc, axis=-1, keepdims=True) + EPS)
        vn_ref[...] = (y * vg_ref[...]).astype(BF16)
        for g in range(GM_GROUPS):
            gs = slice(g * GM_GROUP_DIM, (g + 1) * GM_GROUP_DIM)
            ws = jnp.where(lower, ws_ref[g], 0.0).astype(BF16)
            mixed = jnp.dot(ws, vn_ref[:, gs], preferred_element_type=F32) + bs_ref[:, gs]
            o_ref[sl, gs] = (u_ref[sl, gs].astype(F32) * mixed).astype(o_ref.dtype)


def _gm_gate(uv, v_gain, w_s, b_full, *, tm=512):
    m, two_w = uv.shape
    w = two_w // 2
    c = GM_CHUNK
    assert w == GM_GROUPS * GM_GROUP_DIM and m % tm == 0 and tm % c == 0
    vmem = (6 * _nbytes((tm, w), BF16) + 2 * _nbytes((GM_GROUPS, c, c), F32)
            + 2 * _nbytes((c, w), F32) + 8 * _nbytes((c, w), F32))
    return pl.pallas_call(
        _gm_body,
        grid=(m // tm,),
        in_specs=[
            pl.BlockSpec((tm, w), lambda i: (i, 0)),
            pl.BlockSpec((tm, w), lambda i: (i, 1)),
            pl.BlockSpec((1, w), lambda i: (0, 0)),
            pl.BlockSpec((GM_GROUPS, c, c), lambda i: (0, 0, 0)),
            pl.BlockSpec((c, w), lambda i: (0, 0)),
        ],
        out_specs=pl.BlockSpec((tm, w), lambda i: (i, 0)),
        out_shape=jax.ShapeDtypeStruct((m, w), BF16),
        scratch_shapes=[pltpu.VMEM((c, w), BF16)],
        compiler_params=_params(1, vmem),
    )(uv, uv, v_gain.reshape(1, w), w_s, b_full)


def _ret_body(q_ref, k_ref, v_ref, g_ref, cos_ref, sin_ref, intra_ref, qd_ref, kd_ref,
              cd_ref, gn_ref, o_ref, state_ref):
    c = RET_CHUNK
    half = RET_QK_DIM // 2

    @pl.when(pl.program_id(2) == 0)
    def _():
        state_ref[...] = jnp.zeros_like(state_ref)

    def rotate(x, cos, sin):
        x1, x2 = x[:, :half], x[:, half:]
        return jnp.concatenate([x1 * cos - x2 * sin, x1 * sin + x2 * cos], axis=1)

    for ci in range(q_ref.shape[0] // c):
        sl = slice(ci * c, (ci + 1) * c)
        cos = cos_ref[sl, :]
        sin = sin_ref[sl, :]
        qr = rotate(q_ref[sl, :].astype(F32), cos, sin)
        kr = rotate(k_ref[sl, :].astype(F32), cos, sin) * (RET_QK_DIM ** -0.5)
        qb = qr.astype(BF16)
        kb = kr.astype(BF16)
        vb = v_ref[sl, :]
        scores = lax.dot_general(qb, kb, (((1,), (1,)), ((), ())),
                                 preferred_element_type=F32) * intra_ref[...]
        inner = jnp.dot(scores.astype(BF16), vb, preferred_element_type=F32)
        state = state_ref[...]
        cross = jnp.dot(qb, state.astype(BF16), preferred_element_type=F32) * qd_ref[...]
        kdec = (kr * kd_ref[...]).astype(BF16)
        state_ref[...] = state * cd_ref[...] + lax.dot_general(
            kdec, vb, (((0,), (0,)), ((), ())), preferred_element_type=F32)
        o = inner + cross
        y = _rms_rows(o, gn_ref[...])
        o_ref[sl, :] = (jax.nn.silu(g_ref[sl, :].astype(F32)) * y).astype(o_ref.dtype)


def _retention(qkvg, gn_gain, *, tc=1024):
    b, s, width = qkvg.shape
    hq, hv, c = RET_QK_DIM, RET_V_DIM, RET_CHUNK
    nh = RET_HEADS
    assert width == 2 * nh * (hq + hv) and s % tc == 0 and tc % c == 0
    half = hq // 2
    inv = ROPE_BASE ** (-jnp.linspace(0.0, 1.0, half, dtype=F32))
    ang = jnp.arange(s).astype(F32)[:, None] * inv[None, :]
    cos, sin = jnp.cos(ang), jnp.sin(ang)
    log_gamma = jnp.log(1.0 - 2.0 ** (-5.0 - jnp.arange(nh, dtype=F32)))
    idx = jnp.arange(c, dtype=F32)
    diff = idx[:, None] - idx[None, :]
    intra = jnp.where(diff[None] >= 0,
                      jnp.exp(jnp.maximum(diff, 0.0)[None] * log_gamma[:, None, None]), 0.0)
    qd = jnp.exp((idx + 1.0)[None, :] * log_gamma[:, None])[:, :, None]
    kd = jnp.exp((c - 1.0 - idx)[None, :] * log_gamma[:, None])[:, :, None]
    cd = jnp.broadcast_to(jnp.exp(c * log_gamma)[:, None, None], (nh, 1, hv))

    vmem = (4 * _nbytes((tc, hq), BF16) + 6 * _nbytes((tc, hv), BF16)
            + 4 * _nbytes((tc, half), F32) + 2 * _nbytes((c, c), F32)
            + 4 * _nbytes((c, 128), F32) + 3 * _nbytes((hq, hv), F32)
            + 10 * _nbytes((c, hv), F32))
    return pl.pallas_call(
        _ret_body,
        grid=(b, nh, s // tc),
        in_specs=[
            pl.BlockSpec((None, tc, hq), lambda bi, h, ci: (bi, ci, h)),
            pl.BlockSpec((None, tc, hq), lambda bi, h, ci: (bi, ci, nh + h)),
            pl.BlockSpec((None, tc, hv), lambda bi, h, ci: (bi, ci, nh + h)),
            pl.BlockSpec((None, tc, hv), lambda bi, h, ci: (bi, ci, 2 * nh + h)),
            pl.BlockSpec((tc, half), lambda bi, h, ci: (ci, 0)),
            pl.BlockSpec((tc, half), lambda bi, h, ci: (ci, 0)),
            pl.BlockSpec((None, c, c), lambda bi, h, ci: (h, 0, 0)),
            pl.BlockSpec((None, c, 1), lambda bi, h, ci: (h, 0, 0)),
            pl.BlockSpec((None, c, 1), lambda bi, h, ci: (h, 0, 0)),
            pl.BlockSpec((None, 1, hv), lambda bi, h, ci: (h, 0, 0)),
            pl.BlockSpec((1, hv), lambda bi, h, ci: (0, h)),
        ],
        out_specs=pl.BlockSpec((None, tc, hv), lambda bi, h, ci: (bi, ci, h)),
        out_shape=jax.ShapeDtypeStruct((b, s, nh * hv), BF16),
        scratch_shapes=[pltpu.VMEM((hq, hv), F32)],
        compiler_params=_params(3, vmem),
    )(qkvg, qkvg, qkvg, qkvg, cos, sin, intra, qd, kd, cd, gn_gain.reshape(1, nh * hv))


def kernel(x, sb_norm, sb_w_qkv, sb_w_o, gm_norm, gm_w_in, gm_v_norm, gm_w_s, gm_b_s, gm_w_o,
           ret_norm, ret_w_qkvg, ret_gn, ret_w_o, ffn_norm, ffn_w_gate_up, ffn_w_down,
           final_norm):
    b, s, d = x.shape
    n = b * s
    depth = ffn_norm.shape[0]
    mixer_stacks = ((sb_w_qkv, sb_w_o), (gm_w_in, gm_w_o), (ret_w_qkvg, ret_w_o))
    q_scale = jnp.concatenate([
        jnp.full((d,), (SB_HEAD_DIM ** -0.5) * LOG2_E, F32), jnp.ones((2 * d,), F32)])

    def layer_weights(i):
        kind, j = i % N_MIXERS, i // N_MIXERS
        return ((mixer_stacks[kind][0], j), (mixer_stacks[kind][1], j),
                (ffn_w_gate_up, i), (ffn_w_down, i))

    def project_in(i, h, w_in, round_jobs):
        kind, j = i % N_MIXERS, i // N_MIXERS
        if kind == 0:
            return _matmul(h, w_in, gain=sb_norm[j], col_scale=q_scale,
                           round_jobs=round_jobs, out_dtype=BF16, tn=1536)
        if kind == 1:
            return _matmul(h, w_in, gain=gm_norm[j], act="gelu", round_jobs=round_jobs,
                           out_dtype=BF16)
        return _matmul(h, w_in, gain=ret_norm[j], round_jobs=round_jobs, out_dtype=BF16,
                       tn=1536)

    def mix(i, proj, w_out, h):
        kind, j = i % N_MIXERS, i // N_MIXERS
        if kind == 0:
            o = _sb_attention(proj.reshape(b, s, 3 * d))
            return _matmul(o.reshape(n, d), w_out, residual=h, out_dtype=F32)
        if kind == 1:
            b_full = jnp.repeat(gm_b_s[j].T, GM_GROUP_DIM, axis=1)
            gated = _gm_gate(proj, gm_v_norm[j], gm_w_s[j], b_full)
            return _matmul(gated, w_out, residual=h, out_dtype=F32)
        o = _retention(proj.reshape(b, s, 6 * d), ret_gn[j])
        return _matmul(o.reshape(n, 2 * d), w_out, residual=h, out_dtype=F32, tn=512)

    stack, layer = layer_weights(0)[0]
    w_in = stack[layer].astype(BF16)
    h = x.reshape(n, d)
    proj, (w_out, w_gate_up, w_down) = project_in(0, h, w_in, layer_weights(0)[1:])
    for i in range(depth):
        if i > 0:
            proj = project_in(i, h, w_in, ())
        h = mix(i, proj, w_out, h)
        if i == depth - 1:
            h = _ffn(h, ffn_norm[i], w_gate_up, w_down, final_gain=final_norm)
        else:
            hidden, (w_in, w_out, next_gate_up, next_down) = _matmul(
                h, w_gate_up, gain=ffn_norm[i], act="swiglu",
                round_jobs=layer_weights(i + 1), out_dtype=BF16, tn=512)
            h = _matmul(hidden, w_down, residual=h, out_dtype=F32, tn=512)
            w_gate_up, w_down = next_gate_up, next_down
    return h.reshape(b, s, d)
```

```python
import functools

import jax
import jax.numpy as jnp
from jax import lax
from jax.experimental import pallas as pl
from jax.experimental.pallas import tpu as pltpu

F32 = jnp.float32
BF16 = jnp.bfloat16

EPS = 1e-6
N_MIXERS = 3
SB_HEADS = 16
SB_HEAD_DIM = 128
SB_TILE = 256
SB_HEADS_PER_STEP = 4
LOG2_E = 1.4426950408889634
SB_UNDERFLOW_BITS = 160.0
GM_GROUPS = 16
GM_GROUP_DIM = 128
GM_CHUNK = 128
RET_HEADS = 8
RET_QK_DIM = 256
RET_V_DIM = 512
RET_CHUNK = 128
RET_BLOCK = 1024
ROPE_BASE = 10000.0

V7X_VMEM_LIMIT_BYTES = 60000 * 1024
NORM_ROWS = 256
BF16_SUBLANES = 16
WIDE_K_TN = 512


def _nbytes(shape, dtype):
    n = 1
    for s in shape:
        n *= s
    return n * jnp.dtype(dtype).itemsize


def _params(n_axes, vmem_bytes):
    return pltpu.CompilerParams(
        dimension_semantics=("arbitrary",) * n_axes,
        vmem_limit_bytes=int(min(vmem_bytes, V7X_VMEM_LIMIT_BYTES)),
    )


def _rms_rows(xf, gain):
    y = xf * lax.rsqrt(jnp.mean(xf * xf, axis=-1, keepdims=True) + EPS)
    return y * gain


def _norm_into(x_ref, g_ref, xn_ref):
    rows = x_ref.shape[0]
    step = min(NORM_ROWS, rows)
    for r in range(0, rows, step):
        xn_ref[r:r + step, :] = _rms_rows(x_ref[r:r + step, :], g_ref[...]).astype(xn_ref.dtype)


def _mm_body(*refs, norm, act, has_res, has_scale, n_jobs):
    it = iter(refs)
    x_ref = next(it)
    g_ref = next(it) if norm else None
    w_ref = next(it)
    wu_ref = next(it) if act == "swiglu" else None
    s_ref = next(it) if has_scale else None
    r_ref = next(it) if has_res else None
    raw_refs = [next(it) for _ in range(n_jobs)]
    o_ref = next(it)
    rounded_refs = [next(it) for _ in range(n_jobs)]
    xn_ref = next(it) if norm else None

    def project():
        for raw_ref, rounded_ref in zip(raw_refs, rounded_refs):
            if len(rounded_ref.shape) == 2:
                rounded_ref[...] = raw_ref[...].astype(rounded_ref.dtype)
            else:
                width = rounded_ref.shape[2]
                for cb in range(rounded_ref.shape[0]):
                    rounded_ref[cb] = raw_ref[:, cb * width:(cb + 1) * width].astype(
                        rounded_ref.dtype)
        lhs = xn_ref[...] if norm else x_ref[...]
        acc = jnp.dot(lhs, w_ref[...], preferred_element_type=F32)
        if act == "gelu":
            acc = jax.nn.gelu(acc)
        if act == "swiglu":
            acc = jax.nn.silu(acc) * jnp.dot(lhs, wu_ref[...], preferred_element_type=F32)
        if has_scale:
            acc = acc * s_ref[...]
        if has_res:
            acc = acc + r_ref[...]
        o_ref[...] = acc.astype(o_ref.dtype)

    if not norm:
        project()
        return

    first = pl.program_id(1) == 0

    @pl.when(first)
    def _():
        _norm_into(x_ref, g_ref, xn_ref)
        project()

    @pl.when(jnp.logical_not(first))
    def _():
        project()


def _rows_per_step(rows, steps):
    for rb in range(BF16_SUBLANES, rows + 1, BF16_SUBLANES):
        if rows % rb == 0 and rows // rb <= steps:
            return rb
    raise ValueError((rows, steps))


def _matmul(x, w, *, gain=None, residual=None, act=None, col_scale=None, round_jobs=(),
            out_dtype, tm=1024, tn=1024):
    m, k = x.shape
    blocked = w.ndim == 3
    n = w.shape[0] * w.shape[2] if blocked else w.shape[1]
    if act == "swiglu":
        n //= 2
    assert m % tm == 0 and n % tn == 0, (x.shape, w.shape, tm, tn)
    assert w.shape[1:] == (k, tn) if blocked else w.shape[0] == k, (x.shape, w.shape, tn)
    nj = n // tn
    steps = (m // tm) * nj
    norm = gain is not None
    has_res = residual is not None
    has_scale = col_scale is not None
    in_specs = [pl.BlockSpec((tm, k), lambda i, j: (i, 0))]
    args = [x]
    if norm:
        in_specs.append(pl.BlockSpec((1, k), lambda i, j: (0, 0)))
        args.append(gain.reshape(1, k))
    for first in (0, nj) if act == "swiglu" else (0,):
        if blocked:
            in_specs.append(pl.BlockSpec((None, k, tn), lambda i, j, first=first: (first + j, 0, 0)))
        else:
            in_specs.append(pl.BlockSpec((k, tn), lambda i, j, first=first: (0, first + j)))
        args.append(w)
    if has_scale:
        in_specs.append(pl.BlockSpec((1, tn), lambda i, j: (0, j)))
        args.append(col_scale.reshape(1, n))
    if has_res:
        in_specs.append(pl.BlockSpec((tm, tn), lambda i, j: (i, j)))
        args.append(residual)
    out_specs = [pl.BlockSpec((tm, tn), lambda i, j: (i, j))]
    out_shape = [jax.ShapeDtypeStruct((m, n), out_dtype)]
    scratch = [pltpu.VMEM((tm, k), BF16)] if norm else []
    n_w = 2 if act == "swiglu" else 1
    vmem = (2 * _nbytes((tm, k), x.dtype) + 2 * n_w * _nbytes((k, tn), BF16)
            + 2 * _nbytes((tm, tn), out_dtype) + 2 * n_w * _nbytes((tm, tn), F32)
            + (2 * _nbytes((tm, tn), F32) if has_res else 0)
            + (_nbytes((tm, k), BF16) + 4 * _nbytes((NORM_ROWS, k), F32) if norm else 0))
    for stack, layer, block in round_jobs:
        _, rows, cols = stack.shape
        rb = _rows_per_step(rows, steps)
        last = rows // rb - 1
        in_specs.append(pl.BlockSpec(
            (None, rb, cols),
            lambda i, j, layer=layer, last=last: (layer, jnp.minimum(i * nj + j, last), 0)))
        args.append(stack)
        if block is None:
            out_specs.append(pl.BlockSpec(
                (rb, cols), lambda i, j, last=last: (jnp.minimum(i * nj + j, last), 0)))
            out_shape.append(jax.ShapeDtypeStruct((rows, cols), BF16))
        else:
            out_specs.append(pl.BlockSpec(
                (cols // block, rb, block),
                lambda i, j, last=last: (0, jnp.minimum(i * nj + j, last), 0)))
            out_shape.append(jax.ShapeDtypeStruct((cols // block, rows, block), BF16))
        vmem += 2 * _nbytes((rb, cols), F32) + 3 * _nbytes((rb, cols), BF16)
    outs = pl.pallas_call(
        functools.partial(_mm_body, norm=norm, act=act, has_res=has_res,
                          has_scale=has_scale, n_jobs=len(round_jobs)),
        grid=(m // tm, nj),
        in_specs=in_specs,
        out_specs=out_specs,
        out_shape=out_shape,
        scratch_shapes=scratch,
        compiler_params=_params(2, vmem),
    )(*args)
    return (outs[0], list(outs[1:])) if round_jobs else outs[0]


def _ffn_body(*refs, final):
    if final:
        h_ref, g_ref, wg_ref, wu_ref, wd_ref, fg_ref, o_ref, xn_ref = refs
    else:
        h_ref, g_ref, wg_ref, wu_ref, wd_ref, o_ref, xn_ref = refs
        fg_ref = None
    f = pl.program_id(1)

    def hidden():
        xn = xn_ref[...]
        gate = jnp.dot(xn, wg_ref[...], preferred_element_type=F32)
        up = jnp.dot(xn, wu_ref[...], preferred_element_type=F32)
        return (jax.nn.silu(gate) * up).astype(BF16)

    @pl.when(f == 0)
    def _():
        _norm_into(h_ref, g_ref, xn_ref)
        o_ref[...] = h_ref[...] + jnp.dot(hidden(), wd_ref[...], preferred_element_type=F32)

    @pl.when(f != 0)
    def _():
        o_ref[...] += jnp.dot(hidden(), wd_ref[...], preferred_element_type=F32)

    if final:
        @pl.when(f == pl.num_programs(1) - 1)
        def _():
            rows = o_ref.shape[0]
            step = min(NORM_ROWS, rows)

            def body(r, carry):
                sl = pl.ds(pl.multiple_of(r * step, step), step)
                o_ref[sl, :] = _rms_rows(o_ref[sl, :], fg_ref[...])
                return carry

            lax.fori_loop(0, rows // step, body, 0)


def _ffn(h, gain, w_gate_up, w_down, *, final_gain=None, tm=512, tf=512):
    m, d = h.shape
    ff = w_down.shape[0]
    assert w_gate_up.shape == (d, 2 * ff) and m % tm == 0 and ff % tf == 0
    nf = ff // tf
    final = final_gain is not None
    in_specs = [
        pl.BlockSpec((tm, d), lambda i, f: (i, 0)),
        pl.BlockSpec((1, d), lambda i, f: (0, 0)),
        pl.BlockSpec((d, tf), lambda i, f: (0, f)),
        pl.BlockSpec((d, tf), lambda i, f: (0, nf + f)),
        pl.BlockSpec((tf, d), lambda i, f: (f, 0)),
    ]
    args = [h, gain.reshape(1, d), w_gate_up, w_gate_up, w_down]
    if final:
        in_specs.append(pl.BlockSpec((1, d), lambda i, f: (0, 0)))
        args.append(final_gain.reshape(1, d))
    vmem = (4 * _nbytes((tm, d), F32) + _nbytes((tm, d), BF16)
            + 6 * _nbytes((d, tf), BF16) + 3 * _nbytes((tm, tf), F32)
            + 2 * _nbytes((tm, d), F32) + 4 * _nbytes((NORM_ROWS, d), F32))
    return pl.pallas_call(
        functools.partial(_ffn_body, final=final),
        grid=(m // tm, nf),
        in_specs=in_specs,
        out_specs=pl.BlockSpec((tm, d), lambda i, f: (i, 0)),
        out_shape=jax.ShapeDtypeStruct((m, d), F32),
        scratch_shapes=[pltpu.VMEM((tm, d), BF16)],
        compiler_params=_params(2, vmem),
    )(*args)


def _sb_body(q_ref, k_ref, v_ref, o_ref):
    t = SB_TILE
    hd = SB_HEAD_DIM
    heads = range(SB_HEADS_PER_STEP)
    hs = [slice(hh * hd, (hh + 1) * hd) for hh in heads]
    row = lax.broadcasted_iota(jnp.int32, (t, t), 0)
    col = lax.broadcasted_iota(jnp.int32, (t, t), 1)
    after = (row > col).astype(BF16)
    causal = col < row

    def keys(kb):
        return pl.ds(pl.multiple_of(kb * t, t), t)

    def score(i, kb):
        return [lax.dot_general(q_ref[keys(i), hs[hh]], k_ref[keys(kb), hs[hh]],
                                (((1,), (1,)), ((), ())), preferred_element_type=F32)
                for hh in heads]

    def keep_stage(w, masked):
        keep = jnp.maximum(w, 0.0) + jnp.log2(1.0 + jnp.exp2(-jnp.abs(w)))
        if masked:
            keep = jnp.where(causal, keep, 0.0)
        tail = jnp.dot(keep.astype(BF16), after, preferred_element_type=F32)
        return w - keep, tail, tail[:, :1] + keep[:, :1]

    def weigh(kb, hh, own, tail, run, acc, masked):
        a = jnp.exp2(own - tail - run)
        if masked:
            a = jnp.where(causal, a, 0.0)
        return acc + jnp.dot(a.astype(BF16), v_ref[keys(kb), hs[hh]],
                             preferred_element_type=F32)

    def least(runs):
        m = runs[0]
        for r in runs[1:]:
            m = jnp.minimum(m, r)
        return jnp.min(m)

    def live(carry):
        kb, low, _, _ = carry
        return jnp.logical_and(kb >= 0, low < SB_UNDERFLOW_BITS)

    def query_tile(i, carry):
        has_prev = i > 0
        prev = jnp.maximum(i - 1, 0)
        ws_d = score(i, i)
        ws_p = score(i, prev)
        st_d = [keep_stage(ws_d[hh], True) for hh in heads]
        st_p = [keep_stage(ws_p[hh], False) for hh in heads]
        acc_d = [weigh(i, hh, st_d[hh][0], st_d[hh][1], jnp.zeros((t, 1), F32),
                       jnp.zeros((t, hd), F32), True) for hh in heads]
        acc_p = [weigh(prev, hh, st_p[hh][0], st_p[hh][1], st_d[hh][2], acc_d[hh], False)
                 for hh in heads]
        accs = tuple(jnp.where(has_prev, acc_p[hh], acc_d[hh]) for hh in heads)
        runs = tuple(st_d[hh][2] + jnp.where(has_prev, st_p[hh][2], 0.0) for hh in heads)

        def step(state):
            kb, _, runs, accs = state
            ws = score(i, kb)
            stages = [keep_stage(ws[hh], False) for hh in heads]
            accs = tuple(weigh(kb, hh, stages[hh][0], stages[hh][1], runs[hh], accs[hh],
                               False) for hh in heads)
            runs = tuple(runs[hh] + stages[hh][2] for hh in heads)
            return kb - 1, least(runs), runs, accs

        _, _, _, accs = lax.while_loop(live, step, (i - 2, least(runs), runs, accs))
        for hh in heads:
            o_ref[keys(i), hs[hh]] = accs[hh].astype(o_ref.dtype)
        return carry

    lax.fori_loop(0, q_ref.shape[0] // t, query_tile, 0)


def _sb_attention(qkv):
    b, s, width = qkv.shape
    t = SB_TILE
    wd = SB_HEAD_DIM * SB_HEADS_PER_STEP
    groups = SB_HEADS // SB_HEADS_PER_STEP
    assert width == 3 * SB_HEADS * SB_HEAD_DIM and s % t == 0 and SB_HEADS % SB_HEADS_PER_STEP == 0
    vmem = (8 * _nbytes((s, wd), BF16) + 2 * 4 * SB_HEADS_PER_STEP * _nbytes((t, t), F32))
    return pl.pallas_call(
        _sb_body,
        grid=(b, groups),
        in_specs=[
            pl.BlockSpec((None, s, wd), lambda bi, h: (bi, 0, h)),
            pl.BlockSpec((None, s, wd), lambda bi, h: (bi, 0, groups + h)),
            pl.BlockSpec((None, s, wd), lambda bi, h: (bi, 0, 2 * groups + h)),
        ],
        out_specs=pl.BlockSpec((None, s, wd), lambda bi, h: (bi, 0, h)),
        out_shape=jax.ShapeDtypeStruct((b, s, SB_HEADS * SB_HEAD_DIM), BF16),
        compiler_params=_params(2, vmem),
    )(qkv, qkv, qkv)


def _gm_body(u_ref, v_ref, vg_ref, ws_ref, bs_ref, o_ref, vn_ref):
    rows = u_ref.shape[0]
    c = GM_CHUNK
    row = lax.broadcasted_iota(jnp.int32, (c, c), 0)
    col = lax.broadcasted_iota(jnp.int32, (c, c), 1)
    lower = row >= col
    for ci in range(rows // c):
        sl = slice(ci * c, (ci + 1) * c)
        vf = v_ref[sl, :].astype(F32)
        mu = jnp.mean(vf, axis=-1, keepdims=True)
        xc = vf - mu
        y = xc * lax.rsqrt(jnp.mean(xc * xc, axis=-1, keepdims=True) + EPS)
        vn_ref[...] = (y * vg_ref[...]).astype(BF16)
        for g in range(GM_GROUPS):
            gs = slice(g * GM_GROUP_DIM, (g + 1) * GM_GROUP_DIM)
            ws = jnp.where(lower, ws_ref[g], 0.0).astype(BF16)
            mixed = jnp.dot(ws, vn_ref[:, gs], preferred_element_type=F32) + bs_ref[:, gs]
            o_ref[sl, gs] = (u_ref[sl, gs].astype(F32) * mixed).astype(o_ref.dtype)


def _gm_gate(uv, v_gain, w_s, b_full, *, tm=512):
    m, two_w = uv.shape
    w = two_w // 2
    c = GM_CHUNK
    assert w == GM_GROUPS * GM_GROUP_DIM and m % tm == 0 and tm % c == 0
    vmem = (6 * _nbytes((tm, w), BF16) + 2 * _nbytes((GM_GROUPS, c, c), F32)
            + 2 * _nbytes((c, w), F32) + 8 * _nbytes((c, w), F32))
    return pl.pallas_call(
        _gm_body,
        grid=(m // tm,),
        in_specs=[
            pl.BlockSpec((tm, w), lambda i: (i, 0)),
            pl.BlockSpec((tm, w), lambda i: (i, 1)),
            pl.BlockSpec((1, w), lambda i: (0, 0)),
            pl.BlockSpec((GM_GROUPS, c, c), lambda i: (0, 0, 0)),
            pl.BlockSpec((c, w), lambda i: (0, 0)),
        ],
        out_specs=pl.BlockSpec((tm, w), lambda i: (i, 0)),
        out_shape=jax.ShapeDtypeStruct((m, w), BF16),
        scratch_shapes=[pltpu.VMEM((c, w), BF16)],
        compiler_params=_params(1, vmem),
    )(uv, uv, v_gain.reshape(1, w), w_s, b_full)


def _ret_body(q_ref, k_ref, v_ref, g_ref, cos_ref, sin_ref, intra_ref, qd_ref, kd_ref,
              cd_ref, gn_ref, o_ref, state_ref):
    c = RET_CHUNK
    half = RET_QK_DIM // 2
    state_ref[...] = jnp.zeros_like(state_ref)

    def rotate(x, cos, sin):
        x1, x2 = x[:, :half], x[:, half:]
        return jnp.concatenate([x1 * cos - x2 * sin, x1 * sin + x2 * cos], axis=1)

    def block(it, carry):
        for ci in range(RET_BLOCK // c):
            chunk(pl.ds(pl.multiple_of(it * RET_BLOCK + ci * c, c), c))
        return carry

    def chunk(sl):
        cos = cos_ref[sl, :]
        sin = sin_ref[sl, :]
        qr = rotate(q_ref[sl, :].astype(F32), cos, sin)
        kr = rotate(k_ref[sl, :].astype(F32), cos, sin) * (RET_QK_DIM ** -0.5)
        qb = qr.astype(BF16)
        kb = kr.astype(BF16)
        vb = v_ref[sl, :]
        scores = lax.dot_general(qb, kb, (((1,), (1,)), ((), ())),
                                 preferred_element_type=F32) * intra_ref[...]
        inner = jnp.dot(scores.astype(BF16), vb, preferred_element_type=F32)
        state = state_ref[...]
        cross = jnp.dot(qb, state.astype(BF16), preferred_element_type=F32) * qd_ref[...]
        kdec = (kr * kd_ref[...]).astype(BF16)
        state_ref[...] = state * cd_ref[...] + lax.dot_general(
            kdec, vb, (((0,), (0,)), ((), ())), preferred_element_type=F32)
        o = inner + cross
        y = _rms_rows(o, gn_ref[...])
        o_ref[sl, :] = (jax.nn.silu(g_ref[sl, :].astype(F32)) * y).astype(o_ref.dtype)

    lax.fori_loop(0, q_ref.shape[0] // RET_BLOCK, block, 0)


def _retention(qkvg, gn_gain):
    b, s, width = qkvg.shape
    hq, hv, c = RET_QK_DIM, RET_V_DIM, RET_CHUNK
    nh = RET_HEADS
    assert width == 2 * nh * (hq + hv) and s % RET_BLOCK == 0 and RET_BLOCK % c == 0
    half = hq // 2
    inv = ROPE_BASE ** (-jnp.linspace(0.0, 1.0, half, dtype=F32))
    ang = jnp.arange(s).astype(F32)[:, None] * inv[None, :]
    cos, sin = jnp.cos(ang), jnp.sin(ang)
    log_gamma = jnp.log(1.0 - 2.0 ** (-5.0 - jnp.arange(nh, dtype=F32)))
    idx = jnp.arange(c, dtype=F32)
    diff = idx[:, None] - idx[None, :]
    intra = jnp.where(diff[None] >= 0,
                      jnp.exp(jnp.maximum(diff, 0.0)[None] * log_gamma[:, None, None]), 0.0)
    qd = jnp.exp((idx + 1.0)[None, :] * log_gamma[:, None])[:, :, None]
    kd = jnp.exp((c - 1.0 - idx)[None, :] * log_gamma[:, None])[:, :, None]
    cd = jnp.broadcast_to(jnp.exp(c * log_gamma)[:, None, None], (nh, 1, hv))

    vmem = (4 * _nbytes((s, hq), BF16) + 6 * _nbytes((s, hv), BF16)
            + 4 * _nbytes((s, half), F32) + 2 * _nbytes((c, c), F32)
            + 4 * _nbytes((c, 128), F32) + 3 * _nbytes((hq, hv), F32)
            + 10 * _nbytes((c, hv), F32))
    return pl.pallas_call(
        _ret_body,
        grid=(b, nh),
        in_specs=[
            pl.BlockSpec((None, s, hq), lambda bi, h: (bi, 0, h)),
            pl.BlockSpec((None, s, hq), lambda bi, h: (bi, 0, nh + h)),
            pl.BlockSpec((None, s, hv), lambda bi, h: (bi, 0, nh + h)),
            pl.BlockSpec((None, s, hv), lambda bi, h: (bi, 0, 2 * nh + h)),
            pl.BlockSpec((s, half), lambda bi, h: (0, 0)),
            pl.BlockSpec((s, half), lambda bi, h: (0, 0)),
            pl.BlockSpec((None, c, c), lambda bi, h: (h, 0, 0)),
            pl.BlockSpec((None, c, 1), lambda bi, h: (h, 0, 0)),
            pl.BlockSpec((None, c, 1), lambda bi, h: (h, 0, 0)),
            pl.BlockSpec((None, 1, hv), lambda bi, h: (h, 0, 0)),
            pl.BlockSpec((1, hv), lambda bi, h: (0, h)),
        ],
        out_specs=pl.BlockSpec((None, s, hv), lambda bi, h: (bi, 0, h)),
        out_shape=jax.ShapeDtypeStruct((b, s, nh * hv), BF16),
        scratch_shapes=[pltpu.VMEM((hq, hv), F32)],
        compiler_params=_params(2, vmem),
    )(qkvg, qkvg, qkvg, qkvg, cos, sin, intra, qd, kd, cd, gn_gain.reshape(1, nh * hv))


def kernel(x, sb_norm, sb_w_qkv, sb_w_o, gm_norm, gm_w_in, gm_v_norm, gm_w_s, gm_b_s, gm_w_o,
           ret_norm, ret_w_qkvg, ret_gn, ret_w_o, ffn_norm, ffn_w_gate_up, ffn_w_down,
           final_norm):
    b, s, d = x.shape
    n = b * s
    depth = ffn_norm.shape[0]
    mixer_stacks = ((sb_w_qkv, sb_w_o), (gm_w_in, gm_w_o), (ret_w_qkvg, ret_w_o))
    q_scale = jnp.concatenate([
        jnp.full((d,), (SB_HEAD_DIM ** -0.5) * LOG2_E, F32), jnp.ones((2 * d,), F32)])

    def layer_weights(i):
        kind, j = i % N_MIXERS, i // N_MIXERS
        return ((mixer_stacks[kind][0], j, None),
                (mixer_stacks[kind][1], j, WIDE_K_TN if kind == 2 else None),
                (ffn_w_gate_up, i, None),
                (ffn_w_down, i, WIDE_K_TN if i < depth - 1 else None))

    def project_in(i, h, w_in, round_jobs):
        kind, j = i % N_MIXERS, i // N_MIXERS
        if kind == 0:
            return _matmul(h, w_in, gain=sb_norm[j], col_scale=q_scale,
                           round_jobs=round_jobs, out_dtype=BF16, tn=1536)
        if kind == 1:
            return _matmul(h, w_in, gain=gm_norm[j], act="gelu", round_jobs=round_jobs,
                           out_dtype=BF16)
        return _matmul(h, w_in, gain=ret_norm[j], round_jobs=round_jobs, out_dtype=BF16,
                       tn=1536)

    def mix(i, proj, w_out, h):
        kind, j = i % N_MIXERS, i // N_MIXERS
        if kind == 0:
            o = _sb_attention(proj.reshape(b, s, 3 * d))
            return _matmul(o.reshape(n, d), w_out, residual=h, out_dtype=F32)
        if kind == 1:
            b_full = jnp.repeat(gm_b_s[j].T, GM_GROUP_DIM, axis=1)
            gated = _gm_gate(proj, gm_v_norm[j], gm_w_s[j], b_full)
            return _matmul(gated, w_out, residual=h, out_dtype=F32)
        o = _retention(proj.reshape(b, s, 6 * d), ret_gn[j])
        return _matmul(o.reshape(n, 2 * d), w_out, residual=h, out_dtype=F32, tn=WIDE_K_TN)

    stack, layer, _ = layer_weights(0)[0]
    w_in = stack[layer].astype(BF16)
    h = x.reshape(n, d)
    proj, (w_out, w_gate_up, w_down) = project_in(0, h, w_in, layer_weights(0)[1:])
    for i in range(depth):
        if i > 0:
            proj = project_in(i, h, w_in, ())
        h = mix(i, proj, w_out, h)
        if i == depth - 1:
            h = _ffn(h, ffn_norm[i], w_gate_up, w_down, final_gain=final_norm)
        else:
            hidden, (w_in, w_out, next_gate_up, next_down) = _matmul(
                h, w_gate_up, gain=ffn_norm[i], act="swiglu",
                round_jobs=layer_weights(i + 1), out_dtype=BF16, tn=512)
            h = _matmul(hidden, w_down, residual=h, out_dtype=F32, tn=WIDE_K_TN)
            w_gate_up, w_down = next_gate_up, next_down
    return h.reshape(b, s, d)
```

```python
import functools

import jax
import jax.numpy as jnp
from jax import lax
from jax.experimental import pallas as pl
from jax.experimental.pallas import tpu as pltpu

F32 = jnp.float32
BF16 = jnp.bfloat16

EPS = 1e-6
N_MIXERS = 3
SB_HEADS = 16
SB_HEAD_DIM = 128
SB_TILE = 256
SB_HEADS_PER_STEP = 4
LOG2_E = 1.4426950408889634
SB_UNDERFLOW_BITS = 160.0
GM_GROUPS = 16
GM_GROUP_DIM = 128
GM_CHUNK = 128
RET_HEADS = 8
RET_QK_DIM = 256
RET_V_DIM = 512
RET_CHUNK = 128
RET_BLOCK = 1024
ROPE_BASE = 10000.0

V7X_VMEM_LIMIT_BYTES = 60000 * 1024
NORM_ROWS = 256
BF16_SUBLANES = 16
OUT_PROJ_TN = (1024, 1024, 512)


def _nbytes(shape, dtype):
    n = 1
    for s in shape:
        n *= s
    return n * jnp.dtype(dtype).itemsize


def _params(n_axes, vmem_bytes):
    return pltpu.CompilerParams(
        dimension_semantics=("arbitrary",) * n_axes,
        vmem_limit_bytes=int(min(vmem_bytes, V7X_VMEM_LIMIT_BYTES)),
    )


def _rms_rows(xf, gain):
    y = xf * lax.rsqrt(jnp.mean(xf * xf, axis=-1, keepdims=True) + EPS)
    return y * gain


def _norm_into(x_ref, g_ref, xn_ref):
    rows = x_ref.shape[0]
    step = min(NORM_ROWS, rows)
    for r in range(0, rows, step):
        xn_ref[r:r + step, :] = _rms_rows(x_ref[r:r + step, :], g_ref[...]).astype(xn_ref.dtype)


def _mm_body(*refs, norm, act, has_res, has_scale, n_jobs):
    it = iter(refs)
    x_ref = next(it)
    g_ref = next(it) if norm else None
    w_ref = next(it)
    wu_ref = next(it) if act == "swiglu" else None
    s_ref = next(it) if has_scale else None
    r_ref = next(it) if has_res else None
    raw_refs = [next(it) for _ in range(n_jobs)]
    o_ref = next(it)
    rounded_refs = [next(it) for _ in range(n_jobs)]
    xn_ref = next(it) if norm else None

    def project():
        for raw_ref, rounded_ref in zip(raw_refs, rounded_refs):
            if len(rounded_ref.shape) == 2:
                rounded_ref[...] = raw_ref[...].astype(rounded_ref.dtype)
            else:
                width = rounded_ref.shape[2]
                for cb in range(rounded_ref.shape[0]):
                    rounded_ref[cb] = raw_ref[:, cb * width:(cb + 1) * width].astype(
                        rounded_ref.dtype)
        lhs = xn_ref[...] if norm else x_ref[...]
        acc = jnp.dot(lhs, w_ref[...], preferred_element_type=F32)
        if act == "gelu":
            acc = jax.nn.gelu(acc)
        if act == "swiglu":
            acc = jax.nn.silu(acc) * jnp.dot(lhs, wu_ref[...], preferred_element_type=F32)
        if has_scale:
            acc = acc * s_ref[...]
        if has_res:
            acc = acc + r_ref[...]
        o_ref[...] = acc.astype(o_ref.dtype)

    if not norm:
        project()
        return

    first = pl.program_id(1) == 0

    @pl.when(first)
    def _():
        _norm_into(x_ref, g_ref, xn_ref)
        project()

    @pl.when(jnp.logical_not(first))
    def _():
        project()


def _rows_per_step(rows, steps):
    for rb in range(BF16_SUBLANES, rows + 1, BF16_SUBLANES):
        if rows % rb == 0 and rows // rb <= steps:
            return rb
    raise ValueError((rows, steps))


def _matmul(x, w, *, gain=None, residual=None, act=None, col_scale=None, round_jobs=(),
            out_dtype, tm=1024, tn=1024):
    m, k = x.shape
    blocked = w.ndim == 3
    n = w.shape[0] * w.shape[2] if blocked else w.shape[1]
    if act == "swiglu":
        n //= 2
    assert m % tm == 0 and n % tn == 0, (x.shape, w.shape, tm, tn)
    assert w.shape[1:] == (k, tn) if blocked else w.shape[0] == k, (x.shape, w.shape, tn)
    nj = n // tn
    steps = (m // tm) * nj
    norm = gain is not None
    has_res = residual is not None
    has_scale = col_scale is not None
    in_specs = [pl.BlockSpec((tm, k), lambda i, j: (i, 0))]
    args = [x]
    if norm:
        in_specs.append(pl.BlockSpec((1, k), lambda i, j: (0, 0)))
        args.append(gain.reshape(1, k))
    for first in (0, nj) if act == "swiglu" else (0,):
        if blocked:
            in_specs.append(pl.BlockSpec((None, k, tn), lambda i, j, first=first: (first + j, 0, 0)))
        else:
            in_specs.append(pl.BlockSpec((k, tn), lambda i, j, first=first: (0, first + j)))
        args.append(w)
    if has_scale:
        in_specs.append(pl.BlockSpec((1, tn), lambda i, j: (0, j)))
        args.append(col_scale.reshape(1, n))
    if has_res:
        in_specs.append(pl.BlockSpec((tm, tn), lambda i, j: (i, j)))
        args.append(residual)
    out_specs = [pl.BlockSpec((tm, tn), lambda i, j: (i, j))]
    out_shape = [jax.ShapeDtypeStruct((m, n), out_dtype)]
    scratch = [pltpu.VMEM((tm, k), BF16)] if norm else []
    n_w = 2 if act == "swiglu" else 1
    vmem = (2 * _nbytes((tm, k), x.dtype) + 2 * n_w * _nbytes((k, tn), BF16)
            + 2 * _nbytes((tm, tn), out_dtype) + 2 * n_w * _nbytes((tm, tn), F32)
            + (2 * _nbytes((tm, tn), F32) if has_res else 0)
            + (_nbytes((tm, k), BF16) + 4 * _nbytes((NORM_ROWS, k), F32) if norm else 0))
    for stack, layer, block in round_jobs:
        _, rows, cols = stack.shape
        rb = _rows_per_step(rows, steps)
        last = rows // rb - 1
        in_specs.append(pl.BlockSpec(
            (None, rb, cols),
            lambda i, j, layer=layer, last=last: (layer, jnp.minimum(i * nj + j, last), 0)))
        args.append(stack)
        if block is None:
            out_specs.append(pl.BlockSpec(
                (rb, cols), lambda i, j, last=last: (jnp.minimum(i * nj + j, last), 0)))
            out_shape.append(jax.ShapeDtypeStruct((rows, cols), BF16))
        else:
            out_specs.append(pl.BlockSpec(
                (cols // block, rb, block),
                lambda i, j, last=last: (0, jnp.minimum(i * nj + j, last), 0)))
            out_shape.append(jax.ShapeDtypeStruct((cols // block, rows, block), BF16))
        vmem += 2 * _nbytes((rb, cols), F32) + 3 * _nbytes((rb, cols), BF16)
    outs = pl.pallas_call(
        functools.partial(_mm_body, norm=norm, act=act, has_res=has_res,
                          has_scale=has_scale, n_jobs=len(round_jobs)),
        grid=(m // tm, nj),
        in_specs=in_specs,
        out_specs=out_specs,
        out_shape=out_shape,
        scratch_shapes=scratch,
        compiler_params=_params(2, vmem),
    )(*args)
    return (outs[0], list(outs[1:])) if round_jobs else outs[0]


def _ffn_body(*refs, final):
    if final:
        h_ref, g_ref, wg_ref, wu_ref, wd_ref, fg_ref, o_ref, xn_ref = refs
    else:
        h_ref, g_ref, wg_ref, wu_ref, wd_ref, o_ref, xn_ref = refs
        fg_ref = None
    f = pl.program_id(1)

    def hidden():
        xn = xn_ref[...]
        gate = jnp.dot(xn, wg_ref[...], preferred_element_type=F32)
        up = jnp.dot(xn, wu_ref[...], preferred_element_type=F32)
        return (jax.nn.silu(gate) * up).astype(BF16)

    @pl.when(f == 0)
    def _():
        _norm_into(h_ref, g_ref, xn_ref)
        o_ref[...] = h_ref[...] + jnp.dot(hidden(), wd_ref[...], preferred_element_type=F32)

    @pl.when(f != 0)
    def _():
        o_ref[...] += jnp.dot(hidden(), wd_ref[...], preferred_element_type=F32)

    if final:
        @pl.when(f == pl.num_programs(1) - 1)
        def _():
            rows = o_ref.shape[0]
            step = min(NORM_ROWS, rows)

            def body(r, carry):
                sl = pl.ds(pl.multiple_of(r * step, step), step)
                o_ref[sl, :] = _rms_rows(o_ref[sl, :], fg_ref[...])
                return carry

            lax.fori_loop(0, rows // step, body, 0)


def _ffn(h, gain, w_gate_up, w_down, *, final_gain=None, tm=512, tf=512):
    m, d = h.shape
    ff = w_down.shape[0]
    assert w_gate_up.shape == (d, 2 * ff) and m % tm == 0 and ff % tf == 0
    nf = ff // tf
    final = final_gain is not None
    in_specs = [
        pl.BlockSpec((tm, d), lambda i, f: (i, 0)),
        pl.BlockSpec((1, d), lambda i, f: (0, 0)),
        pl.BlockSpec((d, tf), lambda i, f: (0, f)),
        pl.BlockSpec((d, tf), lambda i, f: (0, nf + f)),
        pl.BlockSpec((tf, d), lambda i, f: (f, 0)),
    ]
    args = [h, gain.reshape(1, d), w_gate_up, w_gate_up, w_down]
    if final:
        in_specs.append(pl.BlockSpec((1, d), lambda i, f: (0, 0)))
        args.append(final_gain.reshape(1, d))
    vmem = (4 * _nbytes((tm, d), F32) + _nbytes((tm, d), BF16)
            + 6 * _nbytes((d, tf), BF16) + 3 * _nbytes((tm, tf), F32)
            + 2 * _nbytes((tm, d), F32) + 4 * _nbytes((NORM_ROWS, d), F32))
    return pl.pallas_call(
        functools.partial(_ffn_body, final=final),
        grid=(m // tm, nf),
        in_specs=in_specs,
        out_specs=pl.BlockSpec((tm, d), lambda i, f: (i, 0)),
        out_shape=jax.ShapeDtypeStruct((m, d), F32),
        scratch_shapes=[pltpu.VMEM((tm, d), BF16)],
        compiler_params=_params(2, vmem),
    )(*args)


def _sb_body(q_ref, k_ref, v_ref, o_ref):
    t = SB_TILE
    hd = SB_HEAD_DIM
    heads = range(SB_HEADS_PER_STEP)
    hs = [slice(hh * hd, (hh + 1) * hd) for hh in heads]
    row = lax.broadcasted_iota(jnp.int32, (t, t), 0)
    col = lax.broadcasted_iota(jnp.int32, (t, t), 1)
    after = (row > col).astype(BF16)
    causal = col < row

    def keys(kb):
        return pl.ds(pl.multiple_of(kb * t, t), t)

    def score(i, kb):
        return [lax.dot_general(q_ref[keys(i), hs[hh]], k_ref[keys(kb), hs[hh]],
                                (((1,), (1,)), ((), ())), preferred_element_type=F32)
                for hh in heads]

    def keep_stage(w, masked):
        keep = jnp.maximum(w, 0.0) + jnp.log2(1.0 + jnp.exp2(-jnp.abs(w)))
        if masked:
            keep = jnp.where(causal, keep, 0.0)
        tail = jnp.dot(keep.astype(BF16), after, preferred_element_type=F32)
        return w - keep, tail, tail[:, :1] + keep[:, :1]

    def weigh(kb, hh, own, tail, run, acc, masked):
        a = jnp.exp2(own - tail - run)
        if masked:
            a = jnp.where(causal, a, 0.0)
        return acc + jnp.dot(a.astype(BF16), v_ref[keys(kb), hs[hh]],
                             preferred_element_type=F32)

    def least(runs):
        m = runs[0]
        for r in runs[1:]:
            m = jnp.minimum(m, r)
        return jnp.min(m)

    def live(carry):
        kb, low, _, _ = carry
        return jnp.logical_and(kb >= 0, low < SB_UNDERFLOW_BITS)

    def query_tile(i, carry):
        has_prev = i > 0
        prev = jnp.maximum(i - 1, 0)
        ws_d = score(i, i)
        ws_p = score(i, prev)
        st_d = [keep_stage(ws_d[hh], True) for hh in heads]
        st_p = [keep_stage(ws_p[hh], False) for hh in heads]
        acc_d = [weigh(i, hh, st_d[hh][0], st_d[hh][1], jnp.zeros((t, 1), F32),
                       jnp.zeros((t, hd), F32), True) for hh in heads]
        acc_p = [weigh(prev, hh, st_p[hh][0], st_p[hh][1], st_d[hh][2], acc_d[hh], False)
                 for hh in heads]
        accs = tuple(jnp.where(has_prev, acc_p[hh], acc_d[hh]) for hh in heads)
        runs = tuple(st_d[hh][2] + jnp.where(has_prev, st_p[hh][2], 0.0) for hh in heads)

        def step(state):
            kb, _, runs, accs = state
            ws = score(i, kb)
            stages = [keep_stage(ws[hh], False) for hh in heads]
            accs = tuple(weigh(kb, hh, stages[hh][0], stages[hh][1], runs[hh], accs[hh],
                               False) for hh in heads)
            runs = tuple(runs[hh] + stages[hh][2] for hh in heads)
            return kb - 1, least(runs), runs, accs

        _, _, _, accs = lax.while_loop(live, step, (i - 2, least(runs), runs, accs))
        for hh in heads:
            o_ref[keys(i), hs[hh]] = accs[hh].astype(o_ref.dtype)
        return carry

    lax.fori_loop(0, q_ref.shape[0] // t, query_tile, 0)


def _sb_attention(qkv):
    b, s, width = qkv.shape
    t = SB_TILE
    wd = SB_HEAD_DIM * SB_HEADS_PER_STEP
    groups = SB_HEADS // SB_HEADS_PER_STEP
    assert width == 3 * SB_HEADS * SB_HEAD_DIM and s % t == 0 and SB_HEADS % SB_HEADS_PER_STEP == 0
    vmem = (8 * _nbytes((s, wd), BF16) + 2 * 4 * SB_HEADS_PER_STEP * _nbytes((t, t), F32))
    return pl.pallas_call(
        _sb_body,
        grid=(b, groups),
        in_specs=[
            pl.BlockSpec((None, s, wd), lambda bi, h: (bi, 0, h)),
            pl.BlockSpec((None, s, wd), lambda bi, h: (bi, 0, groups + h)),
            pl.BlockSpec((None, s, wd), lambda bi, h: (bi, 0, 2 * groups + h)),
        ],
        out_specs=pl.BlockSpec((None, s, wd), lambda bi, h: (bi, 0, h)),
        out_shape=jax.ShapeDtypeStruct((b, s, SB_HEADS * SB_HEAD_DIM), BF16),
        compiler_params=_params(2, vmem),
    )(qkv, qkv, qkv)


def _gm_body(u_ref, v_ref, vg_ref, ws_ref, bs_ref, o_ref, vn_ref):
    rows = u_ref.shape[0]
    c = GM_CHUNK
    row = lax.broadcasted_iota(jnp.int32, (c, c), 0)
    col = lax.broadcasted_iota(jnp.int32, (c, c), 1)
    lower = row >= col
    for ci in range(rows // c):
        sl = slice(ci * c, (ci + 1) * c)
        vf = v_ref[sl, :].astype(F32)
        mu = jnp.mean(vf, axis=-1, keepdims=True)
        xc = vf - mu
        y = xc * lax.rsqrt(jnp.mean(xc * xc, axis=-1, keepdims=True) + EPS)
        vn_ref[...] = (y * vg_ref[...]).astype(BF16)
        for g in range(GM_GROUPS):
            gs = slice(g * GM_GROUP_DIM, (g + 1) * GM_GROUP_DIM)
            ws = jnp.where(lower, ws_ref[g], 0.0).astype(BF16)
            mixed = jnp.dot(ws, vn_ref[:, gs], preferred_element_type=F32) + bs_ref[:, gs]
            o_ref[sl, gs] = (u_ref[sl, gs].astype(F32) * mixed).astype(o_ref.dtype)


def _gm_gate(uv, v_gain, w_s, b_full, *, tm=512):
    m, two_w = uv.shape
    w = two_w // 2
    c = GM_CHUNK
    assert w == GM_GROUPS * GM_GROUP_DIM and m % tm == 0 and tm % c == 0
    vmem = (6 * _nbytes((tm, w), BF16) + 2 * _nbytes((GM_GROUPS, c, c), F32)
            + 2 * _nbytes((c, w), F32) + 8 * _nbytes((c, w), F32))
    return pl.pallas_call(
        _gm_body,
        grid=(m // tm,),
        in_specs=[
            pl.BlockSpec((tm, w), lambda i: (i, 0)),
            pl.BlockSpec((tm, w), lambda i: (i, 1)),
            pl.BlockSpec((1, w), lambda i: (0, 0)),
            pl.BlockSpec((GM_GROUPS, c, c), lambda i: (0, 0, 0)),
            pl.BlockSpec((c, w), lambda i: (0, 0)),
        ],
        out_specs=pl.BlockSpec((tm, w), lambda i: (i, 0)),
        out_shape=jax.ShapeDtypeStruct((m, w), BF16),
        scratch_shapes=[pltpu.VMEM((c, w), BF16)],
        compiler_params=_params(1, vmem),
    )(uv, uv, v_gain.reshape(1, w), w_s, b_full)


def _ret_body(q_ref, k_ref, v_ref, g_ref, cos_ref, sin_ref, intra_ref, qd_ref, kd_ref,
              cd_ref, gn_ref, o_ref, state_ref):
    c = RET_CHUNK
    half = RET_QK_DIM // 2
    state_ref[...] = jnp.zeros_like(state_ref)

    def rotate(x, cos, sin):
        x1, x2 = x[:, :half], x[:, half:]
        return jnp.concatenate([x1 * cos - x2 * sin, x1 * sin + x2 * cos], axis=1)

    def block(it, carry):
        for ci in range(RET_BLOCK // c):
            chunk(pl.ds(pl.multiple_of(it * RET_BLOCK + ci * c, c), c))
        return carry

    def chunk(sl):
        cos = cos_ref[sl, :]
        sin = sin_ref[sl, :]
        qr = rotate(q_ref[sl, :].astype(F32), cos, sin)
        kr = rotate(k_ref[sl, :].astype(F32), cos, sin) * (RET_QK_DIM ** -0.5)
        qb = qr.astype(BF16)
        kb = kr.astype(BF16)
        vb = v_ref[sl, :]
        scores = lax.dot_general(qb, kb, (((1,), (1,)), ((), ())),
                                 preferred_element_type=F32) * intra_ref[...]
        inner = jnp.dot(scores.astype(BF16), vb, preferred_element_type=F32)
        state = state_ref[...]
        cross = jnp.dot(qb, state.astype(BF16), preferred_element_type=F32) * qd_ref[...]
        kdec = (kr * kd_ref[...]).astype(BF16)
        state_ref[...] = state * cd_ref[...] + lax.dot_general(
            kdec, vb, (((0,), (0,)), ((), ())), preferred_element_type=F32)
        o = inner + cross
        y = _rms_rows(o, gn_ref[...])
        o_ref[sl, :] = (jax.nn.silu(g_ref[sl, :].astype(F32)) * y).astype(o_ref.dtype)

    lax.fori_loop(0, q_ref.shape[0] // RET_BLOCK, block, 0)


def _retention(qkvg, gn_gain):
    b, s, width = qkvg.shape
    hq, hv, c = RET_QK_DIM, RET_V_DIM, RET_CHUNK
    nh = RET_HEADS
    assert width == 2 * nh * (hq + hv) and s % RET_BLOCK == 0 and RET_BLOCK % c == 0
    half = hq // 2
    inv = ROPE_BASE ** (-jnp.linspace(0.0, 1.0, half, dtype=F32))
    ang = jnp.arange(s).astype(F32)[:, None] * inv[None, :]
    cos, sin = jnp.cos(ang), jnp.sin(ang)
    log_gamma = jnp.log(1.0 - 2.0 ** (-5.0 - jnp.arange(nh, dtype=F32)))
    idx = jnp.arange(c, dtype=F32)
    diff = idx[:, None] - idx[None, :]
    intra = jnp.where(diff[None] >= 0,
                      jnp.exp(jnp.maximum(diff, 0.0)[None] * log_gamma[:, None, None]), 0.0)
    qd = jnp.exp((idx + 1.0)[None, :] * log_gamma[:, None])[:, :, None]
    kd = jnp.exp((c - 1.0 - idx)[None, :] * log_gamma[:, None])[:, :, None]
    cd = jnp.broadcast_to(jnp.exp(c * log_gamma)[:, None, None], (nh, 1, hv))

    vmem = (4 * _nbytes((s, hq), BF16) + 6 * _nbytes((s, hv), BF16)
            + 4 * _nbytes((s, half), F32) + 2 * _nbytes((c, c), F32)
            + 4 * _nbytes((c, 128), F32) + 3 * _nbytes((hq, hv), F32)
            + 10 * _nbytes((c, hv), F32))
    return pl.pallas_call(
        _ret_body,
        grid=(b, nh),
        in_specs=[
            pl.BlockSpec((None, s, hq), lambda bi, h: (bi, 0, h)),
            pl.BlockSpec((None, s, hq), lambda bi, h: (bi, 0, nh + h)),
            pl.BlockSpec((None, s, hv), lambda bi, h: (bi, 0, nh + h)),
            pl.BlockSpec((None, s, hv), lambda bi, h: (bi, 0, 2 * nh + h)),
            pl.BlockSpec((s, half), lambda bi, h: (0, 0)),
            pl.BlockSpec((s, half), lambda bi, h: (0, 0)),
            pl.BlockSpec((None, c, c), lambda bi, h: (h, 0, 0)),
            pl.BlockSpec((None, c, 1), lambda bi, h: (h, 0, 0)),
            pl.BlockSpec((None, c, 1), lambda bi, h: (h, 0, 0)),
            pl.BlockSpec((None, 1, hv), lambda bi, h: (h, 0, 0)),
            pl.BlockSpec((1, hv), lambda bi, h: (0, h)),
        ],
        out_specs=pl.BlockSpec((None, s, hv), lambda bi, h: (bi, 0, h)),
        out_shape=jax.ShapeDtypeStruct((b, s, nh * hv), BF16),
        scratch_shapes=[pltpu.VMEM((hq, hv), F32)],
        compiler_params=_params(2, vmem),
    )(qkvg, qkvg, qkvg, qkvg, cos, sin, intra, qd, kd, cd, gn_gain.reshape(1, nh * hv))


def kernel(x, sb_norm, sb_w_qkv, sb_w_o, gm_norm, gm_w_in, gm_v_norm, gm_w_s, gm_b_s, gm_w_o,
           ret_norm, ret_w_qkvg, ret_gn, ret_w_o, ffn_norm, ffn_w_gate_up, ffn_w_down,
           final_norm):
    b, s, d = x.shape
    n = b * s
    depth = ffn_norm.shape[0]
    mixer_stacks = ((sb_w_qkv, sb_w_o), (gm_w_in, gm_w_o), (ret_w_qkvg, ret_w_o))
    q_scale = jnp.concatenate([
        jnp.full((d,), (SB_HEAD_DIM ** -0.5) * LOG2_E, F32), jnp.ones((2 * d,), F32)])

    def layer_weights(i):
        kind, j = i % N_MIXERS, i // N_MIXERS
        return ((mixer_stacks[kind][0], j, None),
                (mixer_stacks[kind][1], j, OUT_PROJ_TN[kind]),
                (ffn_w_gate_up, i, None),
                (ffn_w_down, i, None))

    def project_in(i, h, w_in, round_jobs):
        kind, j = i % N_MIXERS, i // N_MIXERS
        if kind == 0:
            return _matmul(h, w_in, gain=sb_norm[j], col_scale=q_scale,
                           round_jobs=round_jobs, out_dtype=BF16, tn=1536)
        if kind == 1:
            return _matmul(h, w_in, gain=gm_norm[j], act="gelu", round_jobs=round_jobs,
                           out_dtype=BF16)
        return _matmul(h, w_in, gain=ret_norm[j], round_jobs=round_jobs, out_dtype=BF16,
                       tn=1536)

    def mix(i, proj, w_out, h):
        kind, j = i % N_MIXERS, i // N_MIXERS
        if kind == 0:
            o = _sb_attention(proj.reshape(b, s, 3 * d)).reshape(n, d)
        elif kind == 1:
            b_full = jnp.repeat(gm_b_s[j].T, GM_GROUP_DIM, axis=1)
            o = _gm_gate(proj, gm_v_norm[j], gm_w_s[j], b_full)
        else:
            o = _retention(proj.reshape(b, s, 6 * d), ret_gn[j]).reshape(n, 2 * d)
        return _matmul(o, w_out, residual=h, out_dtype=F32, tn=OUT_PROJ_TN[kind])

    stack, layer, _ = layer_weights(0)[0]
    w_in = stack[layer].astype(BF16)
    h = x.reshape(n, d)
    proj, (w_out, w_gate_up, w_down) = project_in(0, h, w_in, layer_weights(0)[1:])
    for i in range(depth):
        if i > 0:
            proj = project_in(i, h, w_in, ())
        h = mix(i, proj, w_out, h)
        if i == depth - 1:
            h = _ffn(h, ffn_norm[i], w_gate_up, w_down, final_gain=final_norm)
        else:
            hidden, (w_in, w_out, next_gate_up, next_down) = _matmul(
                h, w_gate_up, gain=ffn_norm[i], act="swiglu",
                round_jobs=layer_weights(i + 1), out_dtype=BF16, tn=512)
            h = _matmul(hidden, w_down, residual=h, out_dtype=F32, tn=512)
            w_gate_up, w_down = next_gate_up, next_down
    return h.reshape(b, s, d)
```

```python
import functools

import jax
import jax.numpy as jnp
from jax import lax
from jax.experimental import pallas as pl
from jax.experimental.pallas import tpu as pltpu

F32 = jnp.float32
BF16 = jnp.bfloat16

EPS = 1e-6
N_MIXERS = 3
SB_HEADS = 16
SB_HEAD_DIM = 128
SB_TILE = 256
SB_HEADS_PER_STEP = 4
LOG2_E = 1.4426950408889634
SB_UNDERFLOW_BITS = 160.0
GM_GROUPS = 16
GM_GROUP_DIM = 128
GM_CHUNK = 128
RET_HEADS = 8
RET_QK_DIM = 256
RET_V_DIM = 512
RET_CHUNK = 128
RET_BLOCK = 1024
ROPE_BASE = 10000.0

V7X_VMEM_LIMIT_BYTES = 60000 * 1024
NORM_ROWS = 256
BF16_SUBLANES = 16
OUT_PROJ_TN = (1024, 1024, 512)


def _nbytes(shape, dtype):
    n = 1
    for s in shape:
        n *= s
    return n * jnp.dtype(dtype).itemsize


def _params(n_axes, vmem_bytes):
    return pltpu.CompilerParams(
        dimension_semantics=("arbitrary",) * n_axes,
        vmem_limit_bytes=int(min(vmem_bytes, V7X_VMEM_LIMIT_BYTES)),
    )


def _rms_rows(xf, gain):
    y = xf * lax.rsqrt(jnp.mean(xf * xf, axis=-1, keepdims=True) + EPS)
    return y * gain


def _norm_into(x_ref, g_ref, xn_ref):
    rows = x_ref.shape[0]
    step = min(NORM_ROWS, rows)
    for r in range(0, rows, step):
        xn_ref[r:r + step, :] = _rms_rows(x_ref[r:r + step, :], g_ref[...]).astype(xn_ref.dtype)


def _mm_body(*refs, norm, act, has_res, has_scale, n_jobs):
    it = iter(refs)
    x_ref = next(it)
    g_ref = next(it) if norm else None
    w_ref = next(it)
    wu_ref = next(it) if act == "swiglu" else None
    s_ref = next(it) if has_scale else None
    r_ref = next(it) if has_res else None
    raw_refs = [next(it) for _ in range(n_jobs)]
    o_ref = next(it)
    rounded_refs = [next(it) for _ in range(n_jobs)]
    xn_ref = next(it) if norm else None

    def project():
        for raw_ref, rounded_ref in zip(raw_refs, rounded_refs):
            if len(rounded_ref.shape) == 2:
                rounded_ref[...] = raw_ref[...].astype(rounded_ref.dtype)
            else:
                width = rounded_ref.shape[2]
                for cb in range(rounded_ref.shape[0]):
                    rounded_ref[cb] = raw_ref[:, cb * width:(cb + 1) * width].astype(
                        rounded_ref.dtype)
        lhs = xn_ref[...] if norm else x_ref[...]
        acc = jnp.dot(lhs, w_ref[...], preferred_element_type=F32)
        if act == "gelu":
            acc = jax.nn.gelu(acc)
        if act == "swiglu":
            acc = jax.nn.silu(acc) * jnp.dot(lhs, wu_ref[...], preferred_element_type=F32)
        if has_scale:
            acc = acc * s_ref[...]
        if has_res:
            acc = acc + r_ref[...]
        o_ref[...] = acc.astype(o_ref.dtype)

    if not norm:
        project()
        return

    first = pl.program_id(1) == 0

    @pl.when(first)
    def _():
        _norm_into(x_ref, g_ref, xn_ref)
        project()

    @pl.when(jnp.logical_not(first))
    def _():
        project()


def _rows_per_step(rows, steps):
    for rb in range(BF16_SUBLANES, rows + 1, BF16_SUBLANES):
        if rows % rb == 0 and rows // rb <= steps:
            return rb
    raise ValueError((rows, steps))


def _matmul(x, w, *, gain=None, residual=None, act=None, col_scale=None, round_jobs=(),
            out_dtype, tm=1024, tn=1024):
    m, k = x.shape
    blocked = w.ndim == 3
    n = w.shape[0] * w.shape[2] if blocked else w.shape[1]
    if act == "swiglu":
        n //= 2
    assert m % tm == 0 and n % tn == 0, (x.shape, w.shape, tm, tn)
    assert w.shape[1:] == (k, tn) if blocked else w.shape[0] == k, (x.shape, w.shape, tn)
    nj = n // tn
    steps = (m // tm) * nj
    norm = gain is not None
    has_res = residual is not None
    has_scale = col_scale is not None
    in_specs = [pl.BlockSpec((tm, k), lambda i, j: (i, 0))]
    args = [x]
    if norm:
        in_specs.append(pl.BlockSpec((1, k), lambda i, j: (0, 0)))
        args.append(gain.reshape(1, k))
    for first in (0, nj) if act == "swiglu" else (0,):
        if blocked:
            in_specs.append(pl.BlockSpec((None, k, tn), lambda i, j, first=first: (first + j, 0, 0)))
        else:
            in_specs.append(pl.BlockSpec((k, tn), lambda i, j, first=first: (0, first + j)))
        args.append(w)
    if has_scale:
        in_specs.append(pl.BlockSpec((1, tn), lambda i, j: (0, j)))
        args.append(col_scale.reshape(1, n))
    if has_res:
        in_specs.append(pl.BlockSpec((tm, tn), lambda i, j: (i, j)))
        args.append(residual)
    out_specs = [pl.BlockSpec((tm, tn), lambda i, j: (i, j))]
    out_shape = [jax.ShapeDtypeStruct((m, n), out_dtype)]
    scratch = [pltpu.VMEM((tm, k), BF16)] if norm else []
    n_w = 2 if act == "swiglu" else 1
    vmem = (2 * _nbytes((tm, k), x.dtype) + 2 * n_w * _nbytes((k, tn), BF16)
            + 2 * _nbytes((tm, tn), out_dtype) + 2 * n_w * _nbytes((tm, tn), F32)
            + (2 * _nbytes((tm, tn), F32) if has_res else 0)
            + (_nbytes((tm, k), BF16) + 4 * _nbytes((NORM_ROWS, k), F32) if norm else 0))
    for stack, layer, block in round_jobs:
        _, rows, cols = stack.shape
        rb = _rows_per_step(rows, steps)
        last = rows // rb - 1
        in_specs.append(pl.BlockSpec(
            (None, rb, cols),
            lambda i, j, layer=layer, last=last: (layer, jnp.minimum(i * nj + j, last), 0)))
        args.append(stack)
        if block is None:
            out_specs.append(pl.BlockSpec(
                (rb, cols), lambda i, j, last=last: (jnp.minimum(i * nj + j, last), 0)))
            out_shape.append(jax.ShapeDtypeStruct((rows, cols), BF16))
        else:
            out_specs.append(pl.BlockSpec(
                (cols // block, rb, block),
                lambda i, j, last=last: (0, jnp.minimum(i * nj + j, last), 0)))
            out_shape.append(jax.ShapeDtypeStruct((cols // block, rows, block), BF16))
        vmem += 2 * _nbytes((rb, cols), F32) + 3 * _nbytes((rb, cols), BF16)
    outs = pl.pallas_call(
        functools.partial(_mm_body, norm=norm, act=act, has_res=has_res,
                          has_scale=has_scale, n_jobs=len(round_jobs)),
        grid=(m // tm, nj),
        in_specs=in_specs,
        out_specs=out_specs,
        out_shape=out_shape,
        scratch_shapes=scratch,
        compiler_params=_params(2, vmem),
    )(*args)
    return (outs[0], list(outs[1:])) if round_jobs else outs[0]


def _ffn_body(*refs, final):
    if final:
        h_ref, g_ref, wg_ref, wu_ref, wd_ref, fg_ref, o_ref, xn_ref = refs
    else:
        h_ref, g_ref, wg_ref, wu_ref, wd_ref, o_ref, xn_ref = refs
        fg_ref = None
    f = pl.program_id(1)

    def hidden():
        xn = xn_ref[...]
        gate = jnp.dot(xn, wg_ref[...], preferred_element_type=F32)
        up = jnp.dot(xn, wu_ref[...], preferred_element_type=F32)
        return (jax.nn.silu(gate) * up).astype(BF16)

    @pl.when(f == 0)
    def _():
        _norm_into(h_ref, g_ref, xn_ref)
        o_ref[...] = h_ref[...] + jnp.dot(hidden(), wd_ref[...], preferred_element_type=F32)

    @pl.when(f != 0)
    def _():
        o_ref[...] += jnp.dot(hidden(), wd_ref[...], preferred_element_type=F32)

    if final:
        @pl.when(f == pl.num_programs(1) - 1)
        def _():
            rows = o_ref.shape[0]
            step = min(NORM_ROWS, rows)

            def body(r, carry):
                sl = pl.ds(pl.multiple_of(r * step, step), step)
                o_ref[sl, :] = _rms_rows(o_ref[sl, :], fg_ref[...])
                return carry

            lax.fori_loop(0, rows // step, body, 0)


def _ffn(h, gain, w_gate_up, w_down, *, final_gain=None, tm=512, tf=512):
    m, d = h.shape
    ff = w_down.shape[0]
    assert w_gate_up.shape == (d, 2 * ff) and m % tm == 0 and ff % tf == 0
    nf = ff // tf
    final = final_gain is not None
    in_specs = [
        pl.BlockSpec((tm, d), lambda i, f: (i, 0)),
        pl.BlockSpec((1, d), lambda i, f: (0, 0)),
        pl.BlockSpec((d, tf), lambda i, f: (0, f)),
        pl.BlockSpec((d, tf), lambda i, f: (0, nf + f)),
        pl.BlockSpec((tf, d), lambda i, f: (f, 0)),
    ]
    args = [h, gain.reshape(1, d), w_gate_up, w_gate_up, w_down]
    if final:
        in_specs.append(pl.BlockSpec((1, d), lambda i, f: (0, 0)))
        args.append(final_gain.reshape(1, d))
    vmem = (4 * _nbytes((tm, d), F32) + _nbytes((tm, d), BF16)
            + 6 * _nbytes((d, tf), BF16) + 3 * _nbytes((tm, tf), F32)
            + 2 * _nbytes((tm, d), F32) + 4 * _nbytes((NORM_ROWS, d), F32))
    return pl.pallas_call(
        functools.partial(_ffn_body, final=final),
        grid=(m // tm, nf),
        in_specs=in_specs,
        out_specs=pl.BlockSpec((tm, d), lambda i, f: (i, 0)),
        out_shape=jax.ShapeDtypeStruct((m, d), F32),
        scratch_shapes=[pltpu.VMEM((tm, d), BF16)],
        compiler_params=_params(2, vmem),
    )(*args)


def _sb_body(q_ref, k_ref, v_ref, o_ref):
    t = SB_TILE
    hd = SB_HEAD_DIM
    heads = range(SB_HEADS_PER_STEP)
    hs = [slice(hh * hd, (hh + 1) * hd) for hh in heads]
    row = lax.broadcasted_iota(jnp.int32, (t, t), 0)
    col = lax.broadcasted_iota(jnp.int32, (t, t), 1)
    after = (row > col).astype(BF16)
    causal = col < row

    def keys(kb):
        return pl.ds(pl.multiple_of(kb * t, t), t)

    def score(i, kb):
        return [lax.dot_general(q_ref[keys(i), hs[hh]], k_ref[keys(kb), hs[hh]],
                                (((1,), (1,)), ((), ())), preferred_element_type=F32)
                for hh in heads]

    def keep_stage(w, masked):
        keep = jnp.maximum(w, 0.0) + jnp.log2(1.0 + jnp.exp2(-jnp.abs(w)))
        if masked:
            keep = jnp.where(causal, keep, 0.0)
        tail = jnp.dot(keep.astype(BF16), after, preferred_element_type=F32)
        return w - keep, tail, tail[:, :1] + keep[:, :1]

    def weigh(kb, hh, own, tail, run, acc, masked):
        a = jnp.exp2(own - tail - run)
        if masked:
            a = jnp.where(causal, a, 0.0)
        return acc + jnp.dot(a.astype(BF16), v_ref[keys(kb), hs[hh]],
                             preferred_element_type=F32)

    def least(runs):
        m = runs[0]
        for r in runs[1:]:
            m = jnp.minimum(m, r)
        return jnp.min(m)

    def live(carry):
        kb, low, _, _ = carry
        return jnp.logical_and(kb >= 0, low < SB_UNDERFLOW_BITS)

    def query_tile(i, carry):
        has_prev = i > 0
        prev = jnp.maximum(i - 1, 0)
        ws_d = score(i, i)
        ws_p = score(i, prev)
        st_d = [keep_stage(ws_d[hh], True) for hh in heads]
        st_p = [keep_stage(ws_p[hh], False) for hh in heads]
        acc_d = [weigh(i, hh, st_d[hh][0], st_d[hh][1], jnp.zeros((t, 1), F32),
                       jnp.zeros((t, hd), F32), True) for hh in heads]
        acc_p = [weigh(prev, hh, st_p[hh][0], st_p[hh][1], st_d[hh][2], acc_d[hh], False)
                 for hh in heads]
        accs = tuple(jnp.where(has_prev, acc_p[hh], acc_d[hh]) for hh in heads)
        runs = tuple(st_d[hh][2] + jnp.where(has_prev, st_p[hh][2], 0.0) for hh in heads)

        def step(state):
            kb, _, runs, accs = state
            ws = score(i, kb)
            stages = [keep_stage(ws[hh], False) for hh in heads]
            accs = tuple(weigh(kb, hh, stages[hh][0], stages[hh][1], runs[hh], accs[hh],
                               False) for hh in heads)
            runs = tuple(runs[hh] + stages[hh][2] for hh in heads)
            return kb - 1, least(runs), runs, accs

        _, _, _, accs = lax.while_loop(live, step, (i - 2, least(runs), runs, accs))
        for hh in heads:
            o_ref[keys(i), hs[hh]] = accs[hh].astype(o_ref.dtype)
        return carry

    lax.fori_loop(0, q_ref.shape[0] // t, query_tile, 0)


def _sb_attention(qkv):
    b, s, width = qkv.shape
    t = SB_TILE
    wd = SB_HEAD_DIM * SB_HEADS_PER_STEP
    groups = SB_HEADS // SB_HEADS_PER_STEP
    assert width == 3 * SB_HEADS * SB_HEAD_DIM and s % t == 0 and SB_HEADS % SB_HEADS_PER_STEP == 0
    vmem = (8 * _nbytes((s, wd), BF16) + 2 * 4 * SB_HEADS_PER_STEP * _nbytes((t, t), F32))
    return pl.pallas_call(
        _sb_body,
        grid=(b, groups),
        in_specs=[
            pl.BlockSpec((None, s, wd), lambda bi, h: (bi, 0, h)),
            pl.BlockSpec((None, s, wd), lambda bi, h: (bi, 0, groups + h)),
            pl.BlockSpec((None, s, wd), lambda bi, h: (bi, 0, 2 * groups + h)),
        ],
        out_specs=pl.BlockSpec((None, s, wd), lambda bi, h: (bi, 0, h)),
        out_shape=jax.ShapeDtypeStruct((b, s, SB_HEADS * SB_HEAD_DIM), BF16),
        compiler_params=_params(2, vmem),
    )(qkv, qkv, qkv)


def _gm_body(u_ref, v_ref, vg_ref, ws_ref, bs_ref, o_ref, vn_ref):
    rows = u_ref.shape[0]
    c = GM_CHUNK
    row = lax.broadcasted_iota(jnp.int32, (c, c), 0)
    col = lax.broadcasted_iota(jnp.int32, (c, c), 1)
    lower = row >= col
    for ci in range(rows // c):
        sl = slice(ci * c, (ci + 1) * c)
        vf = v_ref[sl, :].astype(F32)
        mu = jnp.mean(vf, axis=-1, keepdims=True)
        xc = vf - mu
        y = xc * lax.rsqrt(jnp.mean(xc * xc, axis=-1, keepdims=True) + EPS)
        vn_ref[...] = (y * vg_ref[...]).astype(BF16)
        for g in range(GM_GROUPS):
            gs = slice(g * GM_GROUP_DIM, (g + 1) * GM_GROUP_DIM)
            ws = jnp.where(lower, ws_ref[g], 0.0).astype(BF16)
            mixed = jnp.dot(ws, vn_ref[:, gs], preferred_element_type=F32) + bs_ref[:, gs]
            o_ref[sl, gs] = (u_ref[sl, gs].astype(F32) * mixed).astype(o_ref.dtype)


def _gm_gate(uv, v_gain, w_s, b_full, *, tm=512):
    m, two_w = uv.shape
    w = two_w // 2
    c = GM_CHUNK
    assert w == GM_GROUPS * GM_GROUP_DIM and m % tm == 0 and tm % c == 0
    vmem = (6 * _nbytes((tm, w), BF16) + 2 * _nbytes((GM_GROUPS, c, c), F32)
            + 2 * _nbytes((c, w), F32) + 8 * _nbytes((c, w), F32))
    return pl.pallas_call(
        _gm_body,
        grid=(m // tm,),
        in_specs=[
            pl.BlockSpec((tm, w), lambda i: (i, 0)),
            pl.BlockSpec((tm, w), lambda i: (i, 1)),
            pl.BlockSpec((1, w), lambda i: (0, 0)),
            pl.BlockSpec((GM_GROUPS, c, c), lambda i: (0, 0, 0)),
            pl.BlockSpec((c, w), lambda i: (0, 0)),
        ],
        out_specs=pl.BlockSpec((tm, w), lambda i: (i, 0)),
        out_shape=jax.ShapeDtypeStruct((m, w), BF16),
        scratch_shapes=[pltpu.VMEM((c, w), BF16)],
        compiler_params=_params(1, vmem),
    )(uv, uv, v_gain.reshape(1, w), w_s, b_full)


def _ret_body(q_ref, k_ref, v_ref, g_ref, cos_ref, sin_ref, intra_ref, qd_ref, kd_ref,
              cd_ref, gn_ref, o_ref, state_ref):
    c = RET_CHUNK
    half = RET_QK_DIM // 2
    state_ref[...] = jnp.zeros_like(state_ref)

    def rotate(x, cos, sin):
        x1, x2 = x[:, :half], x[:, half:]
        return jnp.concatenate([x1 * cos - x2 * sin, x1 * sin + x2 * cos], axis=1)

    def block(it, carry):
        for ci in range(RET_BLOCK // c):
            chunk(pl.ds(pl.multiple_of(it * RET_BLOCK + ci * c, c), c))
        return carry

    def chunk(sl):
        cos = cos_ref[sl, :]
        sin = sin_ref[sl, :]
        qr = rotate(q_ref[sl, :].astype(F32), cos, sin)
        kr = rotate(k_ref[sl, :].astype(F32), cos, sin) * (RET_QK_DIM ** -0.5)
        qb = qr.astype(BF16)
        kb = kr.astype(BF16)
        vb = v_ref[sl, :]
        scores = lax.dot_general(qb, kb, (((1,), (1,)), ((), ())),
                                 preferred_element_type=F32) * intra_ref[...]
        inner = jnp.dot(scores.astype(BF16), vb, preferred_element_type=F32)
        state = state_ref[...]
        cross = jnp.dot(qb, state.astype(BF16), preferred_element_type=F32) * qd_ref[...]
        kdec = (kr * kd_ref[...]).astype(BF16)
        state_ref[...] = state * cd_ref[...] + lax.dot_general(
            kdec, vb, (((0,), (0,)), ((), ())), preferred_element_type=F32)
        o = inner + cross
        y = _rms_rows(o, gn_ref[...])
        o_ref[sl, :] = (jax.nn.silu(g_ref[sl, :].astype(F32)) * y).astype(o_ref.dtype)

    lax.fori_loop(0, q_ref.shape[0] // RET_BLOCK, block, 0)


def _retention(qkvg, gn_gain):
    b, s, width = qkvg.shape
    hq, hv, c = RET_QK_DIM, RET_V_DIM, RET_CHUNK
    nh = RET_HEADS
    assert width == 2 * nh * (hq + hv) and s % RET_BLOCK == 0 and RET_BLOCK % c == 0
    half = hq // 2
    inv = ROPE_BASE ** (-jnp.linspace(0.0, 1.0, half, dtype=F32))
    ang = jnp.arange(s).astype(F32)[:, None] * inv[None, :]
    cos, sin = jnp.cos(ang), jnp.sin(ang)
    log_gamma = jnp.log(1.0 - 2.0 ** (-5.0 - jnp.arange(nh, dtype=F32)))
    idx = jnp.arange(c, dtype=F32)
    diff = idx[:, None] - idx[None, :]
    intra = jnp.where(diff[None] >= 0,
                      jnp.exp(jnp.maximum(diff, 0.0)[None] * log_gamma[:, None, None]), 0.0)
    qd = jnp.exp((idx + 1.0)[None, :] * log_gamma[:, None])[:, :, None]
    kd = jnp.exp((c - 1.0 - idx)[None, :] * log_gamma[:, None])[:, :, None]
    cd = jnp.broadcast_to(jnp.exp(c * log_gamma)[:, None, None], (nh, 1, hv))

    vmem = (4 * _nbytes((s, hq), BF16) + 6 * _nbytes((s, hv), BF16)
            + 4 * _nbytes((s, half), F32) + 2 * _nbytes((c, c), F32)
            + 4 * _nbytes((c, 128), F32) + 3 * _nbytes((hq, hv), F32)
            + 10 * _nbytes((c, hv), F32))
    return pl.pallas_call(
        _ret_body,
        grid=(b, nh),
        in_specs=[
            pl.BlockSpec((None, s, hq), lambda bi, h: (bi, 0, h)),
            pl.BlockSpec((None, s, hq), lambda bi, h: (bi, 0, nh + h)),
            pl.BlockSpec((None, s, hv), lambda bi, h: (bi, 0, nh + h)),
            pl.BlockSpec((None, s, hv), lambda bi, h: (bi, 0, 2 * nh + h)),
            pl.BlockSpec((s, half), lambda bi, h: (0, 0)),
            pl.BlockSpec((s, half), lambda bi, h: (0, 0)),
            pl.BlockSpec((None, c, c), lambda bi, h: (h, 0, 0)),
            pl.BlockSpec((None, c, 1), lambda bi, h: (h, 0, 0)),
            pl.BlockSpec((None, c, 1), lambda bi, h: (h, 0, 0)),
            pl.BlockSpec((None, 1, hv), lambda bi, h: (h, 0, 0)),
            pl.BlockSpec((1, hv), lambda bi, h: (0, h)),
        ],
        out_specs=pl.BlockSpec((None, s, hv), lambda bi, h: (bi, 0, h)),
        out_shape=jax.ShapeDtypeStruct((b, s, nh * hv), BF16),
        scratch_shapes=[pltpu.VMEM((hq, hv), F32)],
        compiler_params=_params(2, vmem),
    )(qkvg, qkvg, qkvg, qkvg, cos, sin, intra, qd, kd, cd, gn_gain.reshape(1, nh * hv))


def kernel(x, sb_norm, sb_w_qkv, sb_w_o, gm_norm, gm_w_in, gm_v_norm, gm_w_s, gm_b_s, gm_w_o,
           ret_norm, ret_w_qkvg, ret_gn, ret_w_o, ffn_norm, ffn_w_gate_up, ffn_w_down,
           final_norm):
    b, s, d = x.shape
    n = b * s
    depth = ffn_norm.shape[0]
    mixer_stacks = ((sb_w_qkv, sb_w_o), (gm_w_in, gm_w_o), (ret_w_qkvg, ret_w_o))
    q_scale = jnp.concatenate([
        jnp.full((d,), (SB_HEAD_DIM ** -0.5) * LOG2_E, F32), jnp.ones((2 * d,), F32)])

    def layer_weights(i):
        kind, j = i % N_MIXERS, i // N_MIXERS
        return ((mixer_stacks[kind][0], j, None),
                (mixer_stacks[kind][1], j, OUT_PROJ_TN[kind] if kind == 2 else None),
                (ffn_w_gate_up, i, None),
                (ffn_w_down, i, None))

    def project_in(i, h, w_in, round_jobs):
        kind, j = i % N_MIXERS, i // N_MIXERS
        if kind == 0:
            return _matmul(h, w_in, gain=sb_norm[j], col_scale=q_scale,
                           round_jobs=round_jobs, out_dtype=BF16, tn=1536)
        if kind == 1:
            return _matmul(h, w_in, gain=gm_norm[j], act="gelu", round_jobs=round_jobs,
                           out_dtype=BF16)
        return _matmul(h, w_in, gain=ret_norm[j], round_jobs=round_jobs, out_dtype=BF16,
                       tn=1536)

    def mix(i, proj, w_out, h):
        kind, j = i % N_MIXERS, i // N_MIXERS
        if kind == 0:
            o = _sb_attention(proj.reshape(b, s, 3 * d)).reshape(n, d)
        elif kind == 1:
            b_full = jnp.repeat(gm_b_s[j].T, GM_GROUP_DIM, axis=1)
            o = _gm_gate(proj, gm_v_norm[j], gm_w_s[j], b_full)
        else:
            o = _retention(proj.reshape(b, s, 6 * d), ret_gn[j]).reshape(n, 2 * d)
        return _matmul(o, w_out, residual=h, out_dtype=F32, tn=OUT_PROJ_TN[kind])

    stack, layer, _ = layer_weights(0)[0]
    w_in = stack[layer].astype(BF16)
    h = x.reshape(n, d)
    proj, (w_out, w_gate_up, w_down) = project_in(0, h, w_in, layer_weights(0)[1:])
    for i in range(depth):
        if i > 0:
            proj = project_in(i, h, w_in, ())
        h = mix(i, proj, w_out, h)
        if i == depth - 1:
            h = _ffn(h, ffn_norm[i], w_gate_up, w_down, final_gain=final_norm)
        else:
            hidden, (w_in, w_out, next_gate_up, next_down) = _matmul(
                h, w_gate_up, gain=ffn_norm[i], act="swiglu",
                round_jobs=layer_weights(i + 1), out_dtype=BF16, tn=512)
            h = _matmul(hidden, w_down, residual=h, out_dtype=F32, tn=512)
            w_gate_up, w_down = next_gate_up, next_down
    return h.reshape(b, s, d)
```

```python
import functools

import jax
import jax.numpy as jnp
from jax import lax
from jax.experimental import pallas as pl
from jax.experimental.pallas import tpu as pltpu

F32 = jnp.float32
BF16 = jnp.bfloat16

EPS = 1e-6
N_MIXERS = 3
SB_HEADS = 16
SB_HEAD_DIM = 128
SB_TILE = 256
SB_HEADS_PER_STEP = 4
LOG2_E = 1.4426950408889634
SB_UNDERFLOW_BITS = 160.0
GM_GROUPS = 16
GM_GROUP_DIM = 128
GM_CHUNK = 128
RET_HEADS = 8
RET_QK_DIM = 256
RET_V_DIM = 512
RET_CHUNK = 128
RET_BLOCK = 1024
ROPE_BASE = 10000.0

V7X_VMEM_LIMIT_BYTES = 60000 * 1024
NORM_ROWS = 256
BF16_SUBLANES = 16
OUT_PROJ_TN = (1024, 1024, 512)


def _nbytes(shape, dtype):
    n = 1
    for s in shape:
        n *= s
    return n * jnp.dtype(dtype).itemsize


def _params(n_axes, vmem_bytes):
    return pltpu.CompilerParams(
        dimension_semantics=("arbitrary",) * n_axes,
        vmem_limit_bytes=int(min(vmem_bytes, V7X_VMEM_LIMIT_BYTES)),
    )


def _rms_rows(xf, gain):
    y = xf * lax.rsqrt(jnp.mean(xf * xf, axis=-1, keepdims=True) + EPS)
    return y * gain


def _norm_into(x_ref, g_ref, xn_ref):
    rows = x_ref.shape[0]
    step = min(NORM_ROWS, rows)
    for r in range(0, rows, step):
        xn_ref[r:r + step, :] = _rms_rows(x_ref[r:r + step, :], g_ref[...]).astype(xn_ref.dtype)


def _mm_body(*refs, norm, act, has_res, has_scale, n_jobs):
    it = iter(refs)
    x_ref = next(it)
    g_ref = next(it) if norm else None
    w_ref = next(it)
    wu_ref = next(it) if act == "swiglu" else None
    s_ref = next(it) if has_scale else None
    r_ref = next(it) if has_res else None
    raw_refs = [next(it) for _ in range(n_jobs)]
    o_ref = next(it)
    rounded_refs = [next(it) for _ in range(n_jobs)]
    xn_ref = next(it) if norm else None

    def project():
        for raw_ref, rounded_ref in zip(raw_refs, rounded_refs):
            if len(rounded_ref.shape) == 2:
                rounded_ref[...] = raw_ref[...].astype(rounded_ref.dtype)
            else:
                width = rounded_ref.shape[2]
                for cb in range(rounded_ref.shape[0]):
                    rounded_ref[cb] = raw_ref[:, cb * width:(cb + 1) * width].astype(
                        rounded_ref.dtype)
        lhs = xn_ref[...] if norm else x_ref[...]
        acc = jnp.dot(lhs, w_ref[...], preferred_element_type=F32)
        if act == "gelu":
            acc = jax.nn.gelu(acc)
        if act == "swiglu":
            acc = jax.nn.silu(acc) * jnp.dot(lhs, wu_ref[...], preferred_element_type=F32)
        if has_scale:
            acc = acc * s_ref[...]
        if has_res:
            acc = acc + r_ref[...]
        o_ref[...] = acc.astype(o_ref.dtype)

    if not norm:
        project()
        return

    first = pl.program_id(1) == 0

    @pl.when(first)
    def _():
        _norm_into(x_ref, g_ref, xn_ref)
        project()

    @pl.when(jnp.logical_not(first))
    def _():
        project()


def _rows_per_step(rows, steps):
    for rb in range(BF16_SUBLANES, rows + 1, BF16_SUBLANES):
        if rows % rb == 0 and rows // rb <= steps:
            return rb
    raise ValueError((rows, steps))


def _matmul(x, w, *, gain=None, residual=None, act=None, col_scale=None, round_jobs=(),
            out_dtype, tm=1024, tn=1024):
    m, k = x.shape
    blocked = w.ndim == 3
    n = w.shape[0] * w.shape[2] if blocked else w.shape[1]
    if act == "swiglu":
        n //= 2
    assert m % tm == 0 and n % tn == 0, (x.shape, w.shape, tm, tn)
    assert w.shape[1:] == (k, tn) if blocked else w.shape[0] == k, (x.shape, w.shape, tn)
    nj = n // tn
    steps = (m // tm) * nj
    norm = gain is not None
    has_res = residual is not None
    has_scale = col_scale is not None
    in_specs = [pl.BlockSpec((tm, k), lambda i, j: (i, 0))]
    args = [x]
    if norm:
        in_specs.append(pl.BlockSpec((1, k), lambda i, j: (0, 0)))
        args.append(gain.reshape(1, k))
    for first in (0, nj) if act == "swiglu" else (0,):
        if blocked:
            in_specs.append(pl.BlockSpec((None, k, tn), lambda i, j, first=first: (first + j, 0, 0)))
        else:
            in_specs.append(pl.BlockSpec((k, tn), lambda i, j, first=first: (0, first + j)))
        args.append(w)
    if has_scale:
        in_specs.append(pl.BlockSpec((1, tn), lambda i, j: (0, j)))
        args.append(col_scale.reshape(1, n))
    if has_res:
        in_specs.append(pl.BlockSpec((tm, tn), lambda i, j: (i, j)))
        args.append(residual)
    out_specs = [pl.BlockSpec((tm, tn), lambda i, j: (i, j))]
    out_shape = [jax.ShapeDtypeStruct((m, n), out_dtype)]
    scratch = [pltpu.VMEM((tm, k), BF16)] if norm else []
    n_w = 2 if act == "swiglu" else 1
    vmem = (2 * _nbytes((tm, k), x.dtype) + 2 * n_w * _nbytes((k, tn), BF16)
            + 2 * _nbytes((tm, tn), out_dtype) + 2 * n_w * _nbytes((tm, tn), F32)
            + (2 * _nbytes((tm, tn), F32) if has_res else 0)
            + (_nbytes((tm, k), BF16) + 4 * _nbytes((NORM_ROWS, k), F32) if norm else 0))
    for stack, layer, block in round_jobs:
        _, rows, cols = stack.shape
        rb = _rows_per_step(rows, steps)
        last = rows // rb - 1
        in_specs.append(pl.BlockSpec(
            (None, rb, cols),
            lambda i, j, layer=layer, last=last: (layer, jnp.minimum(i * nj + j, last), 0)))
        args.append(stack)
        if block is None:
            out_specs.append(pl.BlockSpec(
                (rb, cols), lambda i, j, last=last: (jnp.minimum(i * nj + j, last), 0)))
            out_shape.append(jax.ShapeDtypeStruct((rows, cols), BF16))
        else:
            out_specs.append(pl.BlockSpec(
                (cols // block, rb, block),
                lambda i, j, last=last: (0, jnp.minimum(i * nj + j, last), 0)))
            out_shape.append(jax.ShapeDtypeStruct((cols // block, rows, block), BF16))
        vmem += 2 * _nbytes((rb, cols), F32) + 3 * _nbytes((rb, cols), BF16)
    outs = pl.pallas_call(
        functools.partial(_mm_body, norm=norm, act=act, has_res=has_res,
                          has_scale=has_scale, n_jobs=len(round_jobs)),
        grid=(m // tm, nj),
        in_specs=in_specs,
        out_specs=out_specs,
        out_shape=out_shape,
        scratch_shapes=scratch,
        compiler_params=_params(2, vmem),
    )(*args)
    return (outs[0], list(outs[1:])) if round_jobs else outs[0]


def _ffn_body(*refs, final):
    if final:
        h_ref, g_ref, wg_ref, wu_ref, wd_ref, fg_ref, o_ref, xn_ref = refs
    else:
        h_ref, g_ref, wg_ref, wu_ref, wd_ref, o_ref, xn_ref = refs
        fg_ref = None
    f = pl.program_id(1)

    def hidden():
        xn = xn_ref[...]
        gate = jnp.dot(xn, wg_ref[...], preferred_element_type=F32)
        up = jnp.dot(xn, wu_ref[...], preferred_element_type=F32)
        return (jax.nn.silu(gate) * up).astype(BF16)

    @pl.when(f == 0)
    def _():
        _norm_into(h_ref, g_ref, xn_ref)
        o_ref[...] = h_ref[...] + jnp.dot(hidden(), wd_ref[...], preferred_element_type=F32)

    @pl.when(f != 0)
    def _():
        o_ref[...] += jnp.dot(hidden(), wd_ref[...], preferred_element_type=F32)

    if final:
        @pl.when(f == pl.num_programs(1) - 1)
        def _():
            rows = o_ref.shape[0]
            step = min(NORM_ROWS, rows)

            def body(r, carry):
                sl = pl.ds(pl.multiple_of(r * step, step), step)
                o_ref[sl, :] = _rms_rows(o_ref[sl, :], fg_ref[...])
                return carry

            lax.fori_loop(0, rows // step, body, 0)


def _ffn(h, gain, w_gate_up, w_down, *, final_gain=None, tm=512, tf=512):
    m, d = h.shape
    ff = w_down.shape[0]
    assert w_gate_up.shape == (d, 2 * ff) and m % tm == 0 and ff % tf == 0
    nf = ff // tf
    final = final_gain is not None
    in_specs = [
        pl.BlockSpec((tm, d), lambda i, f: (i, 0)),
        pl.BlockSpec((1, d), lambda i, f: (0, 0)),
        pl.BlockSpec((d, tf), lambda i, f: (0, f)),
        pl.BlockSpec((d, tf), lambda i, f: (0, nf + f)),
        pl.BlockSpec((tf, d), lambda i, f: (f, 0)),
    ]
    args = [h, gain.reshape(1, d), w_gate_up, w_gate_up, w_down]
    if final:
        in_specs.append(pl.BlockSpec((1, d), lambda i, f: (0, 0)))
        args.append(final_gain.reshape(1, d))
    vmem = (4 * _nbytes((tm, d), F32) + _nbytes((tm, d), BF16)
            + 6 * _nbytes((d, tf), BF16) + 3 * _nbytes((tm, tf), F32)
            + 2 * _nbytes((tm, d), F32) + 4 * _nbytes((NORM_ROWS, d), F32))
    return pl.pallas_call(
        functools.partial(_ffn_body, final=final),
        grid=(m // tm, nf),
        in_specs=in_specs,
        out_specs=pl.BlockSpec((tm, d), lambda i, f: (i, 0)),
        out_shape=jax.ShapeDtypeStruct((m, d), F32),
        scratch_shapes=[pltpu.VMEM((tm, d), BF16)],
        compiler_params=_params(2, vmem),
    )(*args)


def _sb_body(q_ref, k_ref, v_ref, o_ref):
    t = SB_TILE
    hd = SB_HEAD_DIM
    heads = range(SB_HEADS_PER_STEP)
    hs = [slice(hh * hd, (hh + 1) * hd) for hh in heads]
    row = lax.broadcasted_iota(jnp.int32, (t, t), 0)
    col = lax.broadcasted_iota(jnp.int32, (t, t), 1)
    after = (row > col).astype(BF16)
    causal = col < row

    def keys(kb):
        return pl.ds(pl.multiple_of(kb * t, t), t)

    def score(i, kb):
        return [lax.dot_general(q_ref[keys(i), hs[hh]], k_ref[keys(kb), hs[hh]],
                                (((1,), (1,)), ((), ())), preferred_element_type=F32)
                for hh in heads]

    def keep_stage(w, masked):
        keep = jnp.maximum(w, 0.0) + jnp.log2(1.0 + jnp.exp2(-jnp.abs(w)))
        if masked:
            keep = jnp.where(causal, keep, 0.0)
        tail = jnp.dot(keep.astype(BF16), after, preferred_element_type=F32)
        return w - keep, tail, tail[:, :1] + keep[:, :1]

    def weigh(kb, hh, own, tail, run, acc, masked):
        a = jnp.exp2(own - tail - run)
        if masked:
            a = jnp.where(causal, a, 0.0)
        return acc + jnp.dot(a.astype(BF16), v_ref[keys(kb), hs[hh]],
                             preferred_element_type=F32)

    def least(runs):
        m = runs[0]
        for r in runs[1:]:
            m = jnp.minimum(m, r)
        return jnp.min(m)

    def live(carry):
        kb, low, _, _ = carry
        return jnp.logical_and(kb >= 0, low < SB_UNDERFLOW_BITS)

    def query_tile(i, carry):
        has_prev = i > 0
        prev = jnp.maximum(i - 1, 0)
        ws_d = score(i, i)
        ws_p = score(i, prev)
        st_d = [keep_stage(ws_d[hh], True) for hh in heads]
        st_p = [keep_stage(ws_p[hh], False) for hh in heads]
        acc_d = [weigh(i, hh, st_d[hh][0], st_d[hh][1], jnp.zeros((t, 1), F32),
                       jnp.zeros((t, hd), F32), True) for hh in heads]
        acc_p = [weigh(prev, hh, st_p[hh][0], st_p[hh][1], st_d[hh][2], acc_d[hh], False)
                 for hh in heads]
        accs = tuple(jnp.where(has_prev, acc_p[hh], acc_d[hh]) for hh in heads)
        runs = tuple(st_d[hh][2] + jnp.where(has_prev, st_p[hh][2], 0.0) for hh in heads)

        def step(state):
            kb, _, runs, accs = state
            ws = score(i, kb)
            stages = [keep_stage(ws[hh], False) for hh in heads]
            accs = tuple(weigh(kb, hh, stages[hh][0], stages[hh][1], runs[hh], accs[hh],
                               False) for hh in heads)
            runs = tuple(runs[hh] + stages[hh][2] for hh in heads)
            return kb - 1, least(runs), runs, accs

        _, _, _, accs = lax.while_loop(live, step, (i - 2, least(runs), runs, accs))
        for hh in heads:
            o_ref[keys(i), hs[hh]] = accs[hh].astype(o_ref.dtype)
        return carry

    lax.fori_loop(0, q_ref.shape[0] // t, query_tile, 0)


def _sb_attention(qkv):
    b, s, width = qkv.shape
    t = SB_TILE
    wd = SB_HEAD_DIM * SB_HEADS_PER_STEP
    groups = SB_HEADS // SB_HEADS_PER_STEP
    assert width == 3 * SB_HEADS * SB_HEAD_DIM and s % t == 0 and SB_HEADS % SB_HEADS_PER_STEP == 0
    vmem = (8 * _nbytes((s, wd), BF16) + 2 * 4 * SB_HEADS_PER_STEP * _nbytes((t, t), F32))
    return pl.pallas_call(
        _sb_body,
        grid=(b, groups),
        in_specs=[
            pl.BlockSpec((None, s, wd), lambda bi, h: (bi, 0, h)),
            pl.BlockSpec((None, s, wd), lambda bi, h: (bi, 0, groups + h)),
            pl.BlockSpec((None, s, wd), lambda bi, h: (bi, 0, 2 * groups + h)),
        ],
        out_specs=pl.BlockSpec((None, s, wd), lambda bi, h: (bi, 0, h)),
        out_shape=jax.ShapeDtypeStruct((b, s, SB_HEADS * SB_HEAD_DIM), BF16),
        compiler_params=_params(2, vmem),
    )(qkv, qkv, qkv)


def _gm_body(u_ref, v_ref, vg_ref, ws_ref, bs_ref, wo_ref, r_ref, o_ref, vn_ref, slots_ref):
    rows = u_ref.shape[0]
    c = GM_CHUNK
    i = pl.program_id(0)

    @pl.when(i == 0)
    def _():
        slots_ref[...] = jnp.zeros_like(slots_ref)

    done_ref = slots_ref.at[(i + 1) % 2]
    gated_ref = slots_ref.at[i % 2]
    o_ref[...] = r_ref[...] + jnp.dot(done_ref[...], wo_ref[...], preferred_element_type=F32)

    row = lax.broadcasted_iota(jnp.int32, (c, c), 0)
    col = lax.broadcasted_iota(jnp.int32, (c, c), 1)
    lower = row >= col
    for ci in range(rows // c):
        sl = slice(ci * c, (ci + 1) * c)
        vf = v_ref[sl, :].astype(F32)
        mu = jnp.mean(vf, axis=-1, keepdims=True)
        xc = vf - mu
        y = xc * lax.rsqrt(jnp.mean(xc * xc, axis=-1, keepdims=True) + EPS)
        vn_ref[sl, :] = (y * vg_ref[...]).astype(BF16)
        for g in range(GM_GROUPS):
            gs = slice(g * GM_GROUP_DIM, (g + 1) * GM_GROUP_DIM)
            ws = jnp.where(lower, ws_ref[g], 0.0).astype(BF16)
            mixed = jnp.dot(ws, vn_ref[sl, gs], preferred_element_type=F32) + bs_ref[:, gs]
            gated_ref[sl, gs] = (u_ref[sl, gs].astype(F32) * mixed).astype(BF16)


def _gm_mix(uv, v_gain, w_s, b_full, w_out, residual, *, tm=256):
    m, two_w = uv.shape
    w = two_w // 2
    d = w_out.shape[1]
    c = GM_CHUNK
    assert w == GM_GROUPS * GM_GROUP_DIM and m % tm == 0 and tm % c == 0
    assert w_out.shape[0] == w and residual.shape == (m, d)
    nt = m // tm
    vmem = (7 * _nbytes((tm, w), BF16) + 2 * _nbytes((w, d), BF16)
            + 5 * _nbytes((tm, d), F32) + 2 * _nbytes((GM_GROUPS, c, c), F32)
            + 2 * _nbytes((c, w), F32) + 8 * _nbytes((c, w), F32))

    def gated_tile(i):
        return jnp.minimum(i, nt - 1)

    def projected_tile(i):
        return jnp.maximum(i - 1, 0)

    return pl.pallas_call(
        _gm_body,
        grid=(nt + 1,),
        in_specs=[
            pl.BlockSpec((tm, w), lambda i: (gated_tile(i), 0)),
            pl.BlockSpec((tm, w), lambda i: (gated_tile(i), 1)),
            pl.BlockSpec((1, w), lambda i: (0, 0)),
            pl.BlockSpec((GM_GROUPS, c, c), lambda i: (0, 0, 0)),
            pl.BlockSpec((c, w), lambda i: (0, 0)),
            pl.BlockSpec((w, d), lambda i: (0, 0)),
            pl.BlockSpec((tm, d), lambda i: (projected_tile(i), 0)),
        ],
        out_specs=pl.BlockSpec((tm, d), lambda i: (projected_tile(i), 0)),
        out_shape=jax.ShapeDtypeStruct((m, d), F32),
        scratch_shapes=[pltpu.VMEM((tm, w), BF16), pltpu.VMEM((2, tm, w), BF16)],
        compiler_params=_params(1, vmem),
    )(uv, uv, v_gain.reshape(1, w), w_s, b_full, w_out, residual)


def _ret_body(q_ref, k_ref, v_ref, g_ref, cos_ref, sin_ref, intra_ref, qd_ref, kd_ref,
              cd_ref, gn_ref, o_ref, state_ref):
    c = RET_CHUNK
    half = RET_QK_DIM // 2
    state_ref[...] = jnp.zeros_like(state_ref)

    def rotate(x, cos, sin):
        x1, x2 = x[:, :half], x[:, half:]
        return jnp.concatenate([x1 * cos - x2 * sin, x1 * sin + x2 * cos], axis=1)

    def block(it, carry):
        for ci in range(RET_BLOCK // c):
            chunk(pl.ds(pl.multiple_of(it * RET_BLOCK + ci * c, c), c))
        return carry

    def chunk(sl):
        cos = cos_ref[sl, :]
        sin = sin_ref[sl, :]
        qr = rotate(q_ref[sl, :].astype(F32), cos, sin)
        kr = rotate(k_ref[sl, :].astype(F32), cos, sin) * (RET_QK_DIM ** -0.5)
        qb = qr.astype(BF16)
        kb = kr.astype(BF16)
        vb = v_ref[sl, :]
        scores = lax.dot_general(qb, kb, (((1,), (1,)), ((), ())),
                                 preferred_element_type=F32) * intra_ref[...]
        inner = jnp.dot(scores.astype(BF16), vb, preferred_element_type=F32)
        state = state_ref[...]
        cross = jnp.dot(qb, state.astype(BF16), preferred_element_type=F32) * qd_ref[...]
        kdec = (kr * kd_ref[...]).astype(BF16)
        state_ref[...] = state * cd_ref[...] + lax.dot_general(
            kdec, vb, (((0,), (0,)), ((), ())), preferred_element_type=F32)
        o = inner + cross
        y = _rms_rows(o, gn_ref[...])
        o_ref[sl, :] = (jax.nn.silu(g_ref[sl, :].astype(F32)) * y).astype(o_ref.dtype)

    lax.fori_loop(0, q_ref.shape[0] // RET_BLOCK, block, 0)


def _retention(qkvg, gn_gain):
    b, s, width = qkvg.shape
    hq, hv, c = RET_QK_DIM, RET_V_DIM, RET_CHUNK
    nh = RET_HEADS
    assert width == 2 * nh * (hq + hv) and s % RET_BLOCK == 0 and RET_BLOCK % c == 0
    half = hq // 2
    inv = ROPE_BASE ** (-jnp.linspace(0.0, 1.0, half, dtype=F32))
    ang = jnp.arange(s).astype(F32)[:, None] * inv[None, :]
    cos, sin = jnp.cos(ang), jnp.sin(ang)
    log_gamma = jnp.log(1.0 - 2.0 ** (-5.0 - jnp.arange(nh, dtype=F32)))
    idx = jnp.arange(c, dtype=F32)
    diff = idx[:, None] - idx[None, :]
    intra = jnp.where(diff[None] >= 0,
                      jnp.exp(jnp.maximum(diff, 0.0)[None] * log_gamma[:, None, None]), 0.0)
    qd = jnp.exp((idx + 1.0)[None, :] * log_gamma[:, None])[:, :, None]
    kd = jnp.exp((c - 1.0 - idx)[None, :] * log_gamma[:, None])[:, :, None]
    cd = jnp.broadcast_to(jnp.exp(c * log_gamma)[:, None, None], (nh, 1, hv))

    vmem = (4 * _nbytes((s, hq), BF16) + 6 * _nbytes((s, hv), BF16)
            + 4 * _nbytes((s, half), F32) + 2 * _nbytes((c, c), F32)
            + 4 * _nbytes((c, 128), F32) + 3 * _nbytes((hq, hv), F32)
            + 10 * _nbytes((c, hv), F32))
    return pl.pallas_call(
        _ret_body,
        grid=(b, nh),
        in_specs=[
            pl.BlockSpec((None, s, hq), lambda bi, h: (bi, 0, h)),
            pl.BlockSpec((None, s, hq), lambda bi, h: (bi, 0, nh + h)),
            pl.BlockSpec((None, s, hv), lambda bi, h: (bi, 0, nh + h)),
            pl.BlockSpec((None, s, hv), lambda bi, h: (bi, 0, 2 * nh + h)),
            pl.BlockSpec((s, half), lambda bi, h: (0, 0)),
            pl.BlockSpec((s, half), lambda bi, h: (0, 0)),
            pl.BlockSpec((None, c, c), lambda bi, h: (h, 0, 0)),
            pl.BlockSpec((None, c, 1), lambda bi, h: (h, 0, 0)),
            pl.BlockSpec((None, c, 1), lambda bi, h: (h, 0, 0)),
            pl.BlockSpec((None, 1, hv), lambda bi, h: (h, 0, 0)),
            pl.BlockSpec((1, hv), lambda bi, h: (0, h)),
        ],
        out_specs=pl.BlockSpec((None, s, hv), lambda bi, h: (bi, 0, h)),
        out_shape=jax.ShapeDtypeStruct((b, s, nh * hv), BF16),
        scratch_shapes=[pltpu.VMEM((hq, hv), F32)],
        compiler_params=_params(2, vmem),
    )(qkvg, qkvg, qkvg, qkvg, cos, sin, intra, qd, kd, cd, gn_gain.reshape(1, nh * hv))


def kernel(x, sb_norm, sb_w_qkv, sb_w_o, gm_norm, gm_w_in, gm_v_norm, gm_w_s, gm_b_s, gm_w_o,
           ret_norm, ret_w_qkvg, ret_gn, ret_w_o, ffn_norm, ffn_w_gate_up, ffn_w_down,
           final_norm):
    b, s, d = x.shape
    n = b * s
    depth = ffn_norm.shape[0]
    mixer_stacks = ((sb_w_qkv, sb_w_o), (gm_w_in, gm_w_o), (ret_w_qkvg, ret_w_o))
    q_scale = jnp.concatenate([
        jnp.full((d,), (SB_HEAD_DIM ** -0.5) * LOG2_E, F32), jnp.ones((2 * d,), F32)])

    def layer_weights(i):
        kind, j = i % N_MIXERS, i // N_MIXERS
        return ((mixer_stacks[kind][0], j, None),
                (mixer_stacks[kind][1], j, OUT_PROJ_TN[kind] if kind == 2 else None),
                (ffn_w_gate_up, i, None),
                (ffn_w_down, i, None))

    def project_in(i, h, w_in, round_jobs):
        kind, j = i % N_MIXERS, i // N_MIXERS
        if kind == 0:
            return _matmul(h, w_in, gain=sb_norm[j], col_scale=q_scale,
                           round_jobs=round_jobs, out_dtype=BF16, tn=1536)
        if kind == 1:
            return _matmul(h, w_in, gain=gm_norm[j], act="gelu", round_jobs=round_jobs,
                           out_dtype=BF16)
        return _matmul(h, w_in, gain=ret_norm[j], round_jobs=round_jobs, out_dtype=BF16,
                       tn=1536)

    def mix(i, proj, w_out, h):
        kind, j = i % N_MIXERS, i // N_MIXERS
        if kind == 1:
            b_full = jnp.repeat(gm_b_s[j].T, GM_GROUP_DIM, axis=1)
            return _gm_mix(proj, gm_v_norm[j], gm_w_s[j], b_full, w_out, h)
        if kind == 0:
            o = _sb_attention(proj.reshape(b, s, 3 * d)).reshape(n, d)
        else:
            o = _retention(proj.reshape(b, s, 6 * d), ret_gn[j]).reshape(n, 2 * d)
        return _matmul(o, w_out, residual=h, out_dtype=F32, tn=OUT_PROJ_TN[kind])

    stack, layer, _ = layer_weights(0)[0]
    w_in = stack[layer].astype(BF16)
    h = x.reshape(n, d)
    proj, (w_out, w_gate_up, w_down) = project_in(0, h, w_in, layer_weights(0)[1:])
    for i in range(depth):
        if i > 0:
            proj = project_in(i, h, w_in, ())
        h = mix(i, proj, w_out, h)
        if i == depth - 1:
            h = _ffn(h, ffn_norm[i], w_gate_up, w_down, final_gain=final_norm)
        else:
            hidden, (w_in, w_out, next_gate_up, next_down) = _matmul(
                h, w_gate_up, gain=ffn_norm[i], act="swiglu",
                round_jobs=layer_weights(i + 1), out_dtype=BF16, tn=512)
            h = _matmul(hidden, w_down, residual=h, out_dtype=F32, tn=512)
            w_gate_up, w_down = next_gate_up, next_down
    return h.reshape(b, s, d)
```
